```python
import math
import jax, jax.numpy as jnp
from jax import lax
import numpy as np

D_MODEL = 1024
BATCH = 1
SEQ = 16384
DEPTH = 2

HEAD_DIM = 64
N_MIX_HEADS = 12
SWA_KV_HEADS = 3
SWA_WINDOW = 128
DIL_GROUPS = ((128, 1), (512, 4), (2048, 16))
DIL_HEADS = 4
MAX_DIL = 16
N_MEM = 256
MEM_HEADS = 4
BLOCK = 128
N_BUCKETS = 32
MAX_DISTANCE = 2048
D_FF = 3584
N_EXPERTS = 8
TOP_K = 2
EPS = 1e-5
NEG = -1e30

SWA_IN = N_MIX_HEADS * HEAD_DIM + 2 * SWA_KV_HEADS * HEAD_DIM + MEM_HEADS * HEAD_DIM
DIL_IN = len(DIL_GROUPS) * 3 * DIL_HEADS * HEAD_DIM + MEM_HEADS * HEAD_DIM
SWA_CAT = N_MIX_HEADS * HEAD_DIM + MEM_HEADS * HEAD_DIM
DIL_CAT = DIL_HEADS * HEAD_DIM + MEM_HEADS * HEAD_DIM

kernel_name = "hybrid_swa_sink_dilated_moe_trunk"


def rmsnorm(x, g):
    xf = x.astype(jnp.float32)
    y = xf * lax.rsqrt(jnp.mean(xf * xf, axis=-1, keepdims=True) + EPS)
    return (y * g.astype(jnp.float32)).astype(x.dtype)


def t5_bucket(dist):
    max_exact = N_BUCKETS // 2
    d = jnp.maximum(dist, 0)
    ratio = jnp.maximum(d, 1).astype(jnp.float32) / max_exact
    large = max_exact + (jnp.log(ratio) / math.log(MAX_DISTANCE / max_exact)
                         * (N_BUCKETS - max_exact)).astype(jnp.int32)
    return jnp.where(d < max_exact, d, jnp.minimum(large, N_BUCKETS - 1))


def rel_bias(table, dist_tok, heads):
    b = table[t5_bucket(dist_tok)][..., heads]
    return jnp.moveaxis(b, -1, 0).astype(jnp.float32)


def band_offsets():
    qi = jnp.arange(BLOCK)[:, None]
    kj = jnp.arange(2 * BLOCK)[None, :]
    return qi + BLOCK - kj


def banded_attention(q, k, v, bias, max_dist, sink):
    nb = q.shape[2]
    shift = lambda a: jnp.pad(a, ((0, 0), (0, 0), (1, 0)) + ((0, 0),) * (a.ndim - 3))[:, :, :-1]
    kk = jnp.concatenate([shift(k), k], axis=3)
    vv = jnp.concatenate([shift(v), v], axis=3)
    dist = band_offsets()
    kj = jnp.arange(2 * BLOCK)
    has_prev = (jnp.arange(nb) > 0)[:, None, None] | (kj >= BLOCK)[None, None, :]
    valid = (dist >= 0) & (dist <= max_dist) & has_prev
    s = jnp.einsum('brnqhgd,brnkhd->brnhgqk', q * (HEAD_DIM ** -0.5), kk).astype(jnp.float32) + bias
    s = jnp.where(valid[:, None, None], s, NEG)
    m = jnp.max(s, axis=-1)
    if sink is not None:
        m = jnp.maximum(m, sink[:, :, None])
    p = jnp.exp(s - m[..., None])
    denom = jnp.sum(p, axis=-1)
    if sink is not None:
        denom = denom + jnp.exp(sink[:, :, None] - m)
    p = p / denom[..., None]
    o = jnp.einsum('brnhgqk,brnkhd->brnqhgd', p.astype(vv.dtype), vv)
    return o, m + jnp.log(denom)


def to_dilated(a, d):
    B, sp = a.shape[:2]
    a = a.reshape(B, sp // d, d, *a.shape[2:])
    a = jnp.swapaxes(a, 1, 2)
    return a.reshape(B, d, sp // (d * BLOCK), BLOCK, *a.shape[3:])


def from_dilated(a):
    B, d, nb, L = a.shape[:4]
    a = a.reshape(B, d, nb * L, *a.shape[4:])
    a = jnp.swapaxes(a, 1, 2)
    return a.reshape(B, nb * L * d, *a.shape[3:])


def memory_attention(qm, mem_n, w_mem_kv):
    B, S, _ = qm.shape
    kv = jnp.einsum('bmd,de->bme', mem_n, w_mem_kv).reshape(B, mem_n.shape[1], 2, MEM_HEADS, HEAD_DIM)
    q = qm.reshape(B, S, MEM_HEADS, HEAD_DIM) * (HEAD_DIM ** -0.5)
    s = jnp.einsum('bshd,bmhd->bhsm', q, kv[:, :, 0]).astype(jnp.float32)
    p = jax.nn.softmax(s, axis=-1).astype(qm.dtype)
    return jnp.einsum('bhsm,bmhd->bshd', p, kv[:, :, 1]).reshape(B, S, MEM_HEADS * HEAD_DIM)


def swa_sink_mixer(h, mem_n, table, w_in, sinks, w_mem_kv, w_out):
    B, S, _ = h.shape
    nb = S // BLOCK
    g = N_MIX_HEADS // SWA_KV_HEADS
    dq, dkv = N_MIX_HEADS * HEAD_DIM, SWA_KV_HEADS * HEAD_DIM
    p = jnp.einsum('bsd,de->bse', h, w_in)
    q, k, v, qm = jnp.split(p, [dq, dq + dkv, dq + 2 * dkv], axis=-1)
    q = q.reshape(B, 1, nb, BLOCK, SWA_KV_HEADS, g, HEAD_DIM)
    k = k.reshape(B, 1, nb, BLOCK, SWA_KV_HEADS, HEAD_DIM)
    v = v.reshape(B, 1, nb, BLOCK, SWA_KV_HEADS, HEAD_DIM)
    bias = rel_bias(table, band_offsets(), slice(0, N_MIX_HEADS)).reshape(SWA_KV_HEADS, g, BLOCK, 2 * BLOCK)
    o, _ = banded_attention(q, k, v, bias, SWA_WINDOW - 1,
                            sinks.astype(jnp.float32).reshape(SWA_KV_HEADS, g))
    o = o.reshape(B, S, dq)
    om = memory_attention(qm, mem_n, w_mem_kv)
    return jnp.einsum('bse,ed->bsd', jnp.concatenate([o, om], axis=-1), w_out)


def dilated_mixer(h, mem_n, table, w_in, w_mem_kv, w_out):
    B, S, _ = h.shape
    n_grp = len(DIL_GROUPS)
    dg = 3 * DIL_HEADS * HEAD_DIM
    p = jnp.einsum('bsd,de->bse', h, w_in)
    pg, qm = p[..., :n_grp * dg], p[..., n_grp * dg:]
    unit = MAX_DIL * BLOCK
    sp = -(-S // unit) * unit
    pg = jnp.pad(pg, ((0, 0), (0, sp - S), (0, 0))).reshape(B, sp, n_grp, 3, DIL_HEADS, HEAD_DIM)
    outs, lses = [], []
    for gi, (window, dil) in enumerate(DIL_GROUPS):
        q, k, v = (to_dilated(pg[:, :, gi, c], dil) for c in range(3))
        bias = rel_bias(table, band_offsets() * dil, slice(gi * DIL_HEADS, (gi + 1) * DIL_HEADS))[:, None]
        o, lse = banded_attention(q[:, :, :, :, :, None], k, v, bias, window // dil, None)
        outs.append(from_dilated(o[..., 0, :]))
        lses.append(from_dilated(jnp.moveaxis(lse[..., 0, :], -1, 3)))
    w = jax.nn.softmax(jnp.stack(lses), axis=0)
    o = jnp.sum(w[..., None] * jnp.stack(outs).astype(jnp.float32), axis=0)[:, :S]
    o = o.astype(h.dtype).reshape(B, S, DIL_HEADS * HEAD_DIM)
    om = memory_attention(qm, mem_n, w_mem_kv)
    return jnp.einsum('bse,ed->bsd', jnp.concatenate([o, om], axis=-1), w_out)


def swiglu(h, w_gate, w_up, w_down):
    a = jnp.einsum('bsd,df->bsf', h, w_gate)
    b = jnp.einsum('bsd,df->bsf', h, w_up)
    return jnp.einsum('bsf,fd->bsd', jax.nn.silu(a) * b, w_down)


def moe_swiglu(h, router, w_gate, w_up, w_down):
    logits = jnp.einsum('bsd,de->bse', h, router).astype(jnp.float32)
    vals, idx = lax.top_k(logits, TOP_K)
    gates = jax.nn.softmax(vals, axis=-1)
    combine = jnp.sum(jax.nn.one_hot(idx, N_EXPERTS, dtype=jnp.float32) * gates[..., None], axis=-2)
    y = jnp.zeros_like(h)
    for e in range(N_EXPERTS):
        y = y + combine[..., e:e + 1].astype(h.dtype) * swiglu(h, w_gate[e], w_up[e], w_down[e])
    return y


def setup_inputs(seed: int = 0) -> dict:
    key = jax.random.key(seed)
    ks = iter(jax.random.split(key, 32))
    f32 = jnp.float32
    n_even, n_odd = (DEPTH + 1) // 2, DEPTH // 2

    def nrm(shape, scale):
        return jax.random.normal(next(ks), shape, f32) * scale

    def gain(shape):
        return 1.0 + nrm(shape, 0.02)

    d = D_MODEL
    return {
        "x": nrm((BATCH, SEQ, d), 1.0),
        "mem": nrm((BATCH, N_MEM, d), 1.0),
        "rel_bias_table": nrm((N_BUCKETS, N_MIX_HEADS), 0.2),
        "mem_norm": gain((d,)),
        "norm_mix": gain((DEPTH, d)),
        "norm_ffn": gain((DEPTH, d)),
        "final_norm": gain((d,)),
        "swa_w_in": nrm((n_even, d, SWA_IN), d ** -0.5),
        "swa_sinks": nrm((n_even, N_MIX_HEADS), 0.5),
        "swa_w_mem_kv": nrm((n_even, d, 2 * MEM_HEADS * HEAD_DIM), d ** -0.5),
        "swa_w_out": nrm((n_even, SWA_CAT, d), SWA_CAT ** -0.5),
        "dil_w_in": nrm((n_odd, d, DIL_IN), d ** -0.5),
        "dil_w_mem_kv": nrm((n_odd, d, 2 * MEM_HEADS * HEAD_DIM), d ** -0.5),
        "dil_w_out": nrm((n_odd, DIL_CAT, d), DIL_CAT ** -0.5),
        "ffn_gate": nrm((n_even, d, D_FF), d ** -0.5),
        "ffn_up": nrm((n_even, d, D_FF), d ** -0.5),
        "ffn_down": nrm((n_even, D_FF, d), D_FF ** -0.5),
        "router": nrm((n_odd, d, N_EXPERTS), d ** -0.5),
        "moe_gate": nrm((n_odd, N_EXPERTS, d, D_FF), d ** -0.5),
        "moe_up": nrm((n_odd, N_EXPERTS, d, D_FF), d ** -0.5),
        "moe_down": nrm((n_odd, N_EXPERTS, D_FF, d), D_FF ** -0.5),
    }


def reference(x, mem, rel_bias_table, mem_norm, norm_mix, norm_ffn, final_norm,
              swa_w_in, swa_sinks, swa_w_mem_kv, swa_w_out,
              dil_w_in, dil_w_mem_kv, dil_w_out,
              ffn_gate, ffn_up, ffn_down,
              router, moe_gate, moe_up, moe_down):
    h = x
    mem_n = rmsnorm(mem, mem_norm)
    for i in range(DEPTH):
        j = i // 2
        hn = rmsnorm(h, norm_mix[i])
        if i % 2 == 0:
            h = h + swa_sink_mixer(hn, mem_n, rel_bias_table, swa_w_in[j], swa_sinks[j],
                                   swa_w_mem_kv[j], swa_w_out[j])
            h = h + swiglu(rmsnorm(h, norm_ffn[i]), ffn_gate[j], ffn_up[j], ffn_down[j])
        else:
            h = h + dilated_mixer(hn, mem_n, rel_bias_table, dil_w_in[j], dil_w_mem_kv[j], dil_w_out[j])
            h = h + moe_swiglu(rmsnorm(h, norm_ffn[i]), router[j], moe_gate[j], moe_up[j], moe_down[j])
    return rmsnorm(h, final_norm)
```

```python
import functools
import math

import jax
import jax.numpy as jnp
import numpy as np
from jax import lax
from jax.experimental import pallas as pl
from jax.experimental.pallas import tpu as pltpu

F32 = jnp.float32
BF16 = jnp.bfloat16

D_MODEL = 1024
SEQ = 16384
HEAD_DIM = 64
N_MIX_HEADS = 12
SWA_KV_HEADS = 3
SWA_GROUP = N_MIX_HEADS // SWA_KV_HEADS
SWA_WINDOW = 128
DIL_GROUPS = ((128, 1), (512, 4), (2048, 16))
DIL_HEADS = 4
N_MEM = 256
MEM_HEADS = 4
BLOCK = 128
N_BUCKETS = 32
MAX_DISTANCE = 2048
D_FF = 3584
N_EXPERTS = 8
EPS = 1e-5
NEG = -1e30
Q_SCALE = HEAD_DIM ** -0.5

SWA_Q = N_MIX_HEADS * HEAD_DIM
SWA_KV = SWA_KV_HEADS * HEAD_DIM
SWA_IN = SWA_Q + 2 * SWA_KV + MEM_HEADS * HEAD_DIM
DIL_W = DIL_HEADS * HEAD_DIM
DIL_IN = len(DIL_GROUPS) * 3 * DIL_W + MEM_HEADS * HEAD_DIM
MEM_W = MEM_HEADS * HEAD_DIM

LANES = 128
VMEM_LIMIT = 56 * 1024 * 1024

ROW_TILE = 512
BLOCKS_PER_TILE = ROW_TILE // BLOCK
FFN_ROW_TILE = 1024
FFN_COL_TILE = 512


def _bucket_map(dil):
    qi = np.arange(BLOCK)[:, None]
    kj = np.arange(2 * BLOCK)[None, :]
    d = np.maximum((qi + BLOCK - kj) * dil, 0)
    max_exact = N_BUCKETS // 2
    ratio = np.maximum(d, 1).astype(np.float32) / np.float32(max_exact)
    large = max_exact + (np.log(ratio) / np.float32(math.log(MAX_DISTANCE / max_exact))
                         * np.float32(N_BUCKETS - max_exact)).astype(np.int32)
    return np.where(d < max_exact, d, np.minimum(large, N_BUCKETS - 1)).astype(np.int32)


def _rms(x, g):
    ms = jnp.mean(x * x, axis=-1, keepdims=True)
    return x * lax.rsqrt(ms + EPS) * g


def _dot_nt(a, b):
    return lax.dot_general(a, b, (((1,), (1,)), ((), ())), preferred_element_type=F32)


def _fill_bias(bias_sc, tab_ref, bucket, head0, n_heads):
    for h in range(n_heads):
        def body(k, b, h=h):
            return jnp.where(bucket == k, tab_ref[k, head0 + h], b)
        bias_sc[h] = lax.fori_loop(0, N_BUCKETS, body, jnp.zeros(bucket.shape, F32))


def _band_mask(max_dist):
    qi = lax.broadcasted_iota(jnp.int32, (BLOCK, 2 * BLOCK), 0)
    kj = lax.broadcasted_iota(jnp.int32, (BLOCK, 2 * BLOCK), 1)
    dist = qi + BLOCK - kj
    return (dist >= 0) & (dist <= max_dist), kj >= BLOCK


def _band_head(q, kk, vv, bias, valid, sink):
    s = _dot_nt(q, kk)
    s = jnp.where(valid, s + bias, NEG)
    m = jnp.max(s, axis=-1, keepdims=True)
    if sink is not None:
        m = jnp.maximum(m, sink)
    e = jnp.exp(s - m)
    den = jnp.sum(e, axis=-1, keepdims=True)
    if sink is not None:
        den = den + jnp.exp(sink - m)
    p = (e * (1.0 / den)).astype(BF16)
    return jnp.dot(p, vv, preferred_element_type=F32), m + jnp.log(den)


def _mem_attention(qm, kvm):
    outs = []
    for h in range(MEM_HEADS):
        q = qm[:, h * HEAD_DIM:(h + 1) * HEAD_DIM] * Q_SCALE
        km = kvm[:, h * HEAD_DIM:(h + 1) * HEAD_DIM]
        vm = kvm[:, MEM_W + h * HEAD_DIM:MEM_W + (h + 1) * HEAD_DIM]
        s = _dot_nt(q, km)
        e = jnp.exp(s - jnp.max(s, axis=-1, keepdims=True))
        p = (e * (1.0 / jnp.sum(e, axis=-1, keepdims=True))).astype(BF16)
        outs.append(jnp.dot(p, vm, preferred_element_type=F32).astype(BF16))
    return jnp.concatenate(outs, axis=-1)


def _memkv_kernel(mem_ref, g_ref, w_ref, o_ref):
    mn = _rms(mem_ref[...], g_ref[...]).astype(BF16)
    o_ref[...] = jnp.dot(mn, w_ref[...], preferred_element_type=F32).astype(BF16)


def _memkv(mem, g, w):
    return pl.pallas_call(
        _memkv_kernel,
        out_shape=jax.ShapeDtypeStruct((N_MEM, w.shape[1]), BF16),
        name="mem_kv",
    )(mem, g.reshape(1, D_MODEL), w)


def _proj_kernel(h_ref, g_ref, w_ref, o_ref):
    hn = _rms(h_ref[...], g_ref[...]).astype(BF16)
    o_ref[...] = jnp.dot(hn, w_ref[...], preferred_element_type=F32).astype(BF16)


def _proj(h, g, w, name):
    n = w.shape[1]
    return pl.pallas_call(
        _proj_kernel,
        grid=(SEQ // ROW_TILE,),
        in_specs=[pl.BlockSpec((ROW_TILE, D_MODEL), lambda i: (i, 0)),
                  pl.BlockSpec((1, D_MODEL), lambda i: (0, 0)),
                  pl.BlockSpec((D_MODEL, n), lambda i: (0, 0))],
        out_specs=pl.BlockSpec((ROW_TILE, n), lambda i: (i, 0)),
        out_shape=jax.ShapeDtypeStruct((SEQ, n), BF16),
        compiler_params=pltpu.CompilerParams(dimension_semantics=("arbitrary",), vmem_limit_bytes=VMEM_LIMIT),
        name=name,
    )(h, g.reshape(1, D_MODEL), w)


def _swa_kernel(tab_ref, sink_ref, bucket_ref, p_ref, pprev_ref, kvm_ref, wout_ref, h_ref, g_ref,
                h1_ref, hn1_ref, bias_sc, kv_sc, cat_sc):
    i = pl.program_id(0)

    @pl.when(i == 0)
    def _():
        _fill_bias(bias_sc, tab_ref, bucket_ref[...], 0, N_MIX_HEADS)

    kv_sc[0:BLOCK, :] = pprev_ref[...]
    kv_sc[BLOCK:, :] = p_ref[:, SWA_Q:SWA_Q + 2 * SWA_KV]
    band, own = _band_mask(SWA_WINDOW - 1)

    def block_body(b, carry):
        r0 = pl.multiple_of(b * BLOCK, BLOCK)
        valid = band & (own | (i * BLOCKS_PER_TILE + b > 0))
        qb = p_ref[pl.ds(r0, BLOCK), 0:SWA_Q]
        kvb = kv_sc[pl.ds(r0, 2 * BLOCK), :]
        outs = []
        for kvh in range(SWA_KV_HEADS):
            kk = kvb[:, kvh * HEAD_DIM:(kvh + 1) * HEAD_DIM]
            vv = kvb[:, SWA_KV + kvh * HEAD_DIM:SWA_KV + (kvh + 1) * HEAD_DIM]
            for g in range(SWA_GROUP):
                h = kvh * SWA_GROUP + g
                q = qb[:, h * HEAD_DIM:(h + 1) * HEAD_DIM] * Q_SCALE
                o, _ = _band_head(q, kk, vv, bias_sc[h], valid, sink_ref[h])
                outs.append(o.astype(BF16))
        cat_sc[pl.ds(r0, BLOCK), 0:SWA_Q] = jnp.concatenate(outs, axis=-1)
        return carry

    lax.fori_loop(0, BLOCKS_PER_TILE, block_body, 0)

    cat_sc[:, SWA_Q:] = _mem_attention(p_ref[:, SWA_Q + 2 * SWA_KV:], kvm_ref[...])
    out = h_ref[...] + jnp.dot(cat_sc[...], wout_ref[...], preferred_element_type=F32)
    h1_ref[...] = out
    hn1_ref[...] = _rms(out, g_ref[...]).astype(BF16)


def _swa_layer(p, kvm, table, sinks, wout, h, g_ffn):
    kv_cols = 2 * SWA_KV
    return pl.pallas_call(
        _swa_kernel,
        grid=(SEQ // ROW_TILE,),
        in_specs=[
            pl.BlockSpec(memory_space=pltpu.SMEM),
            pl.BlockSpec(memory_space=pltpu.SMEM),
            pl.BlockSpec((BLOCK, 2 * BLOCK), lambda i: (0, 0)),
            pl.BlockSpec((ROW_TILE, SWA_IN), lambda i: (i, 0)),
            pl.BlockSpec((BLOCK, kv_cols), lambda i: (jnp.maximum(i * BLOCKS_PER_TILE - 1, 0), SWA_Q // kv_cols)),
            pl.BlockSpec((N_MEM, 2 * MEM_W), lambda i: (0, 0)),
            pl.BlockSpec((D_MODEL, D_MODEL), lambda i: (0, 0)),
            pl.BlockSpec((ROW_TILE, D_MODEL), lambda i: (i, 0)),
            pl.BlockSpec((1, D_MODEL), lambda i: (0, 0)),
        ],
        out_specs=[pl.BlockSpec((ROW_TILE, D_MODEL), lambda i: (i, 0)),
                   pl.BlockSpec((ROW_TILE, D_MODEL), lambda i: (i, 0))],
        out_shape=[jax.ShapeDtypeStruct((SEQ, D_MODEL), F32), jax.ShapeDtypeStruct((SEQ, D_MODEL), BF16)],
        scratch_shapes=[pltpu.VMEM((N_MIX_HEADS, BLOCK, 2 * BLOCK), F32),
                        pltpu.VMEM((ROW_TILE + BLOCK, kv_cols), BF16),
                        pltpu.VMEM((ROW_TILE, D_MODEL), BF16)],
        compiler_params=pltpu.CompilerParams(dimension_semantics=("arbitrary",), vmem_limit_bytes=VMEM_LIMIT),
        name="swa_mixer",
    )(table, sinks, jnp.asarray(_bucket_map(1)), p, p, kvm, wout, h, g_ffn.reshape(1, D_MODEL))


def _dil_kernel(tab_ref, bucket_ref, q_ref, k_ref, v_ref, kp_ref, vp_ref, o_ref, l_ref,
                bias_sc, k_sc, v_sc, *, head0, max_dist):
    r = pl.program_id(0)
    n = pl.program_id(1)

    @pl.when((r == 0) & (n == 0))
    def _():
        _fill_bias(bias_sc, tab_ref, bucket_ref[...], head0, DIL_HEADS)

    k_sc[0:BLOCK, :] = kp_ref[...]
    k_sc[BLOCK:, :] = k_ref[...]
    v_sc[0:BLOCK, :] = vp_ref[...]
    v_sc[BLOCK:, :] = v_ref[...]
    band, own = _band_mask(max_dist)

    def block_body(b, carry):
        r0 = pl.multiple_of(b * BLOCK, BLOCK)
        valid = band & (own | (n * BLOCKS_PER_TILE + b > 0))
        qb = q_ref[pl.ds(r0, BLOCK), :]
        kb = k_sc[pl.ds(r0, 2 * BLOCK), :]
        vb = v_sc[pl.ds(r0, 2 * BLOCK), :]
        outs, lses = [], []
        for h in range(DIL_HEADS):
            sl = slice(h * HEAD_DIM, (h + 1) * HEAD_DIM)
            o, lse = _band_head(qb[:, sl] * Q_SCALE, kb[:, sl], vb[:, sl], bias_sc[h], valid, None)
            outs.append(o)
            lses.append(jnp.broadcast_to(lse, (BLOCK, HEAD_DIM)))
        o_ref[pl.ds(r0, BLOCK), :] = jnp.concatenate(outs, axis=-1)
        l_ref[pl.ds(r0, BLOCK), :] = jnp.concatenate(lses, axis=-1)
        return carry

    lax.fori_loop(0, BLOCKS_PER_TILE, block_body, 0)


def _dil_group(p, table, gi):
    window, d = DIL_GROUPS[gi]
    rows = SEQ // d
    slabs = DIL_IN // DIL_W
    pv = p.reshape(rows, d * DIL_IN)
    tiles = rows // ROW_TILE
    prev = lambda n: jnp.maximum(n * BLOCKS_PER_TILE - 1, 0)
    col = lambda r, c: r * slabs + gi * 3 + c
    o, l = pl.pallas_call(
        functools.partial(_dil_kernel, head0=gi * DIL_HEADS, max_dist=window // d),
        grid=(d, tiles),
        in_specs=[
            pl.BlockSpec(memory_space=pltpu.SMEM),
            pl.BlockSpec((BLOCK, 2 * BLOCK), lambda r, n: (0, 0)),
            pl.BlockSpec((ROW_TILE, DIL_W), lambda r, n: (n, col(r, 0))),
            pl.BlockSpec((ROW_TILE, DIL_W), lambda r, n: (n, col(r, 1))),
            pl.BlockSpec((ROW_TILE, DIL_W), lambda r, n: (n, col(r, 2))),
            pl.BlockSpec((BLOCK, DIL_W), lambda r, n: (prev(n), col(r, 1))),
            pl.BlockSpec((BLOCK, DIL_W), lambda r, n: (prev(n), col(r, 2))),
        ],
        out_specs=[pl.BlockSpec((ROW_TILE, DIL_W), lambda r, n: (n, r)),
                   pl.BlockSpec((ROW_TILE, DIL_W), lambda r, n: (n, r))],
        out_shape=[jax.ShapeDtypeStruct((rows, d * DIL_W), F32), jax.ShapeDtypeStruct((rows, d * DIL_W), F32)],
        scratch_shapes=[pltpu.VMEM((DIL_HEADS, BLOCK, 2 * BLOCK), F32),
                        pltpu.VMEM((ROW_TILE + BLOCK, DIL_W), BF16),
                        pltpu.VMEM((ROW_TILE + BLOCK, DIL_W), BF16)],
        compiler_params=pltpu.CompilerParams(dimension_semantics=("arbitrary", "arbitrary"),
                                             vmem_limit_bytes=VMEM_LIMIT),
        name=f"dil_attn_{gi}",
    )(table, jnp.asarray(_bucket_map(d)), pv, pv, pv, pv, pv)
    return o.reshape(SEQ, DIL_W), l.reshape(SEQ, DIL_W)


def _dilout_kernel(o0_ref, o1_ref, o2_ref, l0_ref, l1_ref, l2_ref, qm_ref, kvm_ref, wout_ref, h_ref, g_ref,
                   router_ref, h2_ref, hn2_ref, comb_ref):
    l0, l1, l2 = l0_ref[...], l1_ref[...], l2_ref[...]
    mx = jnp.maximum(jnp.maximum(l0, l1), l2)
    e0, e1, e2 = jnp.exp(l0 - mx), jnp.exp(l1 - mx), jnp.exp(l2 - mx)
    inv = 1.0 / (e0 + e1 + e2)
    o = (e0 * inv) * o0_ref[...] + (e1 * inv) * o1_ref[...] + (e2 * inv) * o2_ref[...]
    cat = jnp.concatenate([o.astype(BF16), _mem_attention(qm_ref[...], kvm_ref[...])], axis=-1)
    out = h_ref[...] + jnp.dot(cat, wout_ref[...], preferred_element_type=F32)
    h2_ref[...] = out
    hn = _rms(out, g_ref[...]).astype(BF16)
    hn2_ref[...] = hn

    logits = jnp.dot(hn, router_ref[...], preferred_element_type=F32)
    lane = lax.broadcasted_iota(jnp.int32, logits.shape, 1)
    masked = jnp.where(lane < N_EXPERTS, logits, -jnp.inf)
    v0 = jnp.max(masked, axis=-1, keepdims=True)
    i0 = jnp.min(jnp.where(masked == v0, lane, LANES), axis=-1, keepdims=True)
    rest = jnp.where(lane == i0, -jnp.inf, masked)
    v1 = jnp.max(rest, axis=-1, keepdims=True)
    i1 = jnp.min(jnp.where(rest == v1, lane, LANES), axis=-1, keepdims=True)
    ex = jnp.exp(v1 - v0)
    inv2 = 1.0 / (1.0 + ex)
    comb_ref[...] = jnp.where(lane == i0, inv2, 0.0) + jnp.where(lane == i1, ex * inv2, 0.0)


def _dilout(os, ls, p, kvm, wout, h, g_ffn, router):
    row = lambda i: (i, 0)
    const = lambda i: (0, 0)
    slab = pl.BlockSpec((ROW_TILE, DIL_W), row)
    return pl.pallas_call(
        _dilout_kernel,
        grid=(SEQ // ROW_TILE,),
        in_specs=[slab, slab, slab, slab, slab, slab,
                  pl.BlockSpec((ROW_TILE, MEM_W), lambda i: (i, DIL_IN // MEM_W - 1)),
                  pl.BlockSpec((N_MEM, 2 * MEM_W), lambda i: (0, 1)),
                  pl.BlockSpec((DIL_W + MEM_W, D_MODEL), const),
                  pl.BlockSpec((ROW_TILE, D_MODEL), row),
                  pl.BlockSpec((1, D_MODEL), const),
                  pl.BlockSpec((D_MODEL, LANES), const)],
        out_specs=[pl.BlockSpec((ROW_TILE, D_MODEL), row),
                   pl.BlockSpec((ROW_TILE, D_MODEL), row),
                   pl.BlockSpec((ROW_TILE, LANES), row)],
        out_shape=[jax.ShapeDtypeStruct((SEQ, D_MODEL), F32), jax.ShapeDtypeStruct((SEQ, D_MODEL), BF16),
                   jax.ShapeDtypeStruct((SEQ, LANES), F32)],
        compiler_params=pltpu.CompilerParams(dimension_semantics=("arbitrary",), vmem_limit_bytes=VMEM_LIMIT),
        name="dil_out",
    )(*os, *ls, p, kvm, wout, h, g_ffn.reshape(1, D_MODEL), router)


def _swiglu_partial(x, wg, wu, wd):
    a = jnp.dot(x, wg, preferred_element_type=F32)
    b = jnp.dot(x, wu, preferred_element_type=F32)
    hm = (a * jax.nn.sigmoid(a) * b).astype(BF16)
    return jnp.dot(hm, wd, preferred_element_type=F32)


def _ffn_kernel(x_ref, wg_ref, wu_ref, wd_ref, h_ref, o_ref, acc_sc):
    j = pl.program_id(1)

    @pl.when(j == 0)
    def _():
        acc_sc[...] = jnp.zeros_like(acc_sc)

    acc_sc[...] += _swiglu_partial(x_ref[...], wg_ref[...], wu_ref[...], wd_ref[...])

    @pl.when(j == pl.num_programs(1) - 1)
    def _():
        o_ref[...] = h_ref[...] + acc_sc[...]


def _ffn(x, h, wg, wu, wd):
    tm, tf = FFN_ROW_TILE, FFN_COL_TILE
    return pl.pallas_call(
        _ffn_kernel,
        grid=(SEQ // tm, D_FF // tf),
        in_specs=[pl.BlockSpec((tm, D_MODEL), lambda i, j: (i, 0)),
                  pl.BlockSpec((D_MODEL, tf), lambda i, j: (0, j)),
                  pl.BlockSpec((D_MODEL, tf), lambda i, j: (0, j)),
                  pl.BlockSpec((tf, D_MODEL), lambda i, j: (j, 0)),
                  pl.BlockSpec((tm, D_MODEL), lambda i, j: (i, 0))],
        out_specs=pl.BlockSpec((tm, D_MODEL), lambda i, j: (i, 0)),
        out_shape=jax.ShapeDtypeStruct((SEQ, D_MODEL), F32),
        scratch_shapes=[pltpu.VMEM((tm, D_MODEL), F32)],
        compiler_params=pltpu.CompilerParams(dimension_semantics=("arbitrary", "arbitrary"),
                                             vmem_limit_bytes=VMEM_LIMIT),
        name="ffn",
    )(x, wg, wu, wd, h)


def _moe_kernel(x_ref, c_ref, wg_ref, wu_ref, wd_ref, h_ref, gf_ref, o_ref, acc_sc, tot_sc):
    e = pl.program_id(1)
    j = pl.program_id(2)
    last_j = pl.num_programs(2) - 1

    @pl.when((e == 0) & (j == 0))
    def _():
        tot_sc[...] = jnp.zeros_like(tot_sc)

    @pl.when(j == 0)
    def _():
        acc_sc[...] = jnp.zeros_like(acc_sc)

    acc_sc[...] += _swiglu_partial(x_ref[...], wg_ref[0], wu_ref[0], wd_ref[0])

    @pl.when(j == last_j)
    def _():
        c = c_ref[...]
        lane = lax.broadcasted_iota(jnp.int32, c.shape, 1)
        ce = jnp.sum(jnp.where(lane == e, c, 0.0), axis=-1, keepdims=True)
        tot_sc[...] += ce * acc_sc[...]

    @pl.when((e == N_EXPERTS - 1) & (j == last_j))
    def _():
        o_ref[...] = _rms(h_ref[...] + tot_sc[...], gf_ref[...])


def _moe(x, comb, wg, wu, wd, h, g_final):
    tm, tf = FFN_ROW_TILE, FFN_COL_TILE
    return pl.pallas_call(
        _moe_kernel,
        grid=(SEQ // tm, N_EXPERTS, D_FF // tf),
        in_specs=[pl.BlockSpec((tm, D_MODEL), lambda i, e, j: (i, 0)),
                  pl.BlockSpec((tm, LANES), lambda i, e, j: (i, 0)),
                  pl.BlockSpec((1, D_MODEL, tf), lambda i, e, j: (e, 0, j)),
                  pl.BlockSpec((1, D_MODEL, tf), lambda i, e, j: (e, 0, j)),
                  pl.BlockSpec((1, tf, D_MODEL), lambda i, e, j: (e, j, 0)),
                  pl.BlockSpec((tm, D_MODEL), lambda i, e, j: (i, 0)),
                  pl.BlockSpec((1, D_MODEL), lambda i, e, j: (0, 0))],
        out_specs=pl.BlockSpec((tm, D_MODEL), lambda i, e, j: (i, 0)),
        out_shape=jax.ShapeDtypeStruct((SEQ, D_MODEL), F32),
        scratch_shapes=[pltpu.VMEM((tm, D_MODEL), F32), pltpu.VMEM((tm, D_MODEL), F32)],
        compiler_params=pltpu.CompilerParams(dimension_semantics=("arbitrary", "arbitrary", "arbitrary"),
                                             vmem_limit_bytes=VMEM_LIMIT),
        name="moe",
    )(x, comb, wg, wu, wd, h, g_final.reshape(1, D_MODEL))


def kernel(x, mem, rel_bias_table, mem_norm, norm_mix, norm_ffn, final_norm, swa_w_in, swa_sinks, swa_w_mem_kv,
           swa_w_out, dil_w_in, dil_w_mem_kv, dil_w_out, ffn_gate, ffn_up, ffn_down, router, moe_gate, moe_up,
           moe_down):
    assert x.shape == (1, SEQ, D_MODEL) and mem.shape == (1, N_MEM, D_MODEL)
    assert norm_mix.shape == (2, D_MODEL) and swa_w_in.shape == (1, D_MODEL, SWA_IN)
    assert dil_w_in.shape == (1, D_MODEL, DIL_IN) and moe_gate.shape == (1, N_EXPERTS, D_MODEL, D_FF)
    bf = lambda a: a.astype(BF16)
    h0 = x.reshape(SEQ, D_MODEL)

    kvm = _memkv(mem[0], mem_norm, bf(jnp.concatenate([swa_w_mem_kv[0], dil_w_mem_kv[0]], axis=1)))

    p0 = _proj(h0, norm_mix[0], bf(swa_w_in[0]), "swa_proj")
    h1, hn1 = _swa_layer(p0, kvm, rel_bias_table, swa_sinks[0], bf(swa_w_out[0]), h0, norm_ffn[0])
    h2 = _ffn(hn1, h1, bf(ffn_gate[0]), bf(ffn_up[0]), bf(ffn_down[0]))

    p1 = _proj(h2, norm_mix[1], bf(dil_w_in[0]), "dil_proj")
    groups = [_dil_group(p1, rel_bias_table, gi) for gi in range(len(DIL_GROUPS))]
    router_p = jnp.pad(bf(router[0]), ((0, 0), (0, LANES - N_EXPERTS)))
    h3, hn3, comb = _dilout([g[0] for g in groups], [g[1] for g in groups], p1, kvm, bf(dil_w_out[0]), h2,
                            norm_ffn[1], router_p)
    out = _moe(hn3, comb, bf(moe_gate[0]), bf(moe_up[0]), bf(moe_down[0]), h3, final_norm)
    return out.reshape(1, SEQ, D_MODEL)
```

```python
import functools
import math

import jax
import jax.numpy as jnp
import numpy as np
from jax import lax
from jax.experimental import pallas as pl
from jax.experimental.pallas import tpu as pltpu

F32 = jnp.float32
BF16 = jnp.bfloat16

D_MODEL = 1024
SEQ = 16384
HEAD_DIM = 64
N_MIX_HEADS = 12
SWA_KV_HEADS = 3
SWA_GROUP = N_MIX_HEADS // SWA_KV_HEADS
SWA_WINDOW = 128
DIL_GROUPS = ((128, 1), (512, 4), (2048, 16))
DIL_HEADS = 4
N_MEM = 256
MEM_HEADS = 4
BLOCK = 128
N_BUCKETS = 32
MAX_DISTANCE = 2048
D_FF = 3584
N_EXPERTS = 8
EPS = 1e-5
NEG = -1e30
Q_SCALE = HEAD_DIM ** -0.5

SWA_Q = N_MIX_HEADS * HEAD_DIM
SWA_KV = SWA_KV_HEADS * HEAD_DIM
SWA_IN = SWA_Q + 2 * SWA_KV + MEM_HEADS * HEAD_DIM
DIL_W = DIL_HEADS * HEAD_DIM
DIL_IN = len(DIL_GROUPS) * 3 * DIL_W + MEM_HEADS * HEAD_DIM
MEM_W = MEM_HEADS * HEAD_DIM

LANES = 128
SUBLANES = 8
VMEM_LIMIT = 56 * 1024 * 1024

ROW_TILE = 512
BLOCKS_PER_TILE = ROW_TILE // BLOCK
FFN_ROW_TILE = 1024
FFN_COL_TILE = 512


def _bucket_map(dil):
    qi = np.arange(BLOCK)[:, None]
    kj = np.arange(2 * BLOCK)[None, :]
    d = np.maximum((qi + BLOCK - kj) * dil, 0)
    max_exact = N_BUCKETS // 2
    ratio = np.maximum(d, 1).astype(np.float32) / np.float32(max_exact)
    large = max_exact + (np.log(ratio) / np.float32(math.log(MAX_DISTANCE / max_exact))
                         * np.float32(N_BUCKETS - max_exact)).astype(np.int32)
    return np.where(d < max_exact, d, np.minimum(large, N_BUCKETS - 1)).astype(np.int32)


def _rms(x, g):
    ms = jnp.mean(x * x, axis=-1, keepdims=True)
    return x * lax.rsqrt(ms + EPS) * g


def _dot_nt(a, b):
    return lax.dot_general(a, b, (((1,), (1,)), ((), ())), preferred_element_type=F32)


def _fill_bias(bias_sc, tab_ref, bucket, head0, n_heads):
    for h in range(n_heads):
        def body(k, b, h=h):
            return jnp.where(bucket == k, tab_ref[k, head0 + h], b)
        bias_sc[h] = lax.fori_loop(0, N_BUCKETS, body, jnp.zeros(bucket.shape, F32))


def _band_mask(max_dist):
    qi = lax.broadcasted_iota(jnp.int32, (BLOCK, 2 * BLOCK), 0)
    kj = lax.broadcasted_iota(jnp.int32, (BLOCK, 2 * BLOCK), 1)
    dist = qi + BLOCK - kj
    return (dist >= 0) & (dist <= max_dist), kj >= BLOCK


def _band_head(q, kk, vv, bias, valid, sink):
    s = _dot_nt(q, kk)
    s = jnp.where(valid, s + bias, NEG)
    m = jnp.max(s, axis=-1, keepdims=True)
    if sink is not None:
        m = jnp.maximum(m, sink)
    e = jnp.exp(s - m)
    den = jnp.sum(e, axis=-1, keepdims=True)
    if sink is not None:
        den = den + jnp.exp(sink - m)
    p = (e * (1.0 / den)).astype(BF16)
    return jnp.dot(p, vv, preferred_element_type=F32), m + jnp.log(den)


def _mem_attention(qm, kvm):
    outs = []
    for h in range(MEM_HEADS):
        q = qm[:, h * HEAD_DIM:(h + 1) * HEAD_DIM] * Q_SCALE
        km = kvm[:, h * HEAD_DIM:(h + 1) * HEAD_DIM]
        vm = kvm[:, MEM_W + h * HEAD_DIM:MEM_W + (h + 1) * HEAD_DIM]
        s = _dot_nt(q, km)
        e = jnp.exp(s - jnp.max(s, axis=-1, keepdims=True))
        p = (e * (1.0 / jnp.sum(e, axis=-1, keepdims=True))).astype(BF16)
        outs.append(jnp.dot(p, vm, preferred_element_type=F32).astype(BF16))
    return jnp.concatenate(outs, axis=-1)


def _memkv_kernel(mem_ref, g_ref, w_ref, o_ref):
    mn = _rms(mem_ref[...], g_ref[...]).astype(BF16)
    o_ref[...] = jnp.dot(mn, w_ref[...], preferred_element_type=F32).astype(BF16)


def _memkv(mem, g, w):
    return pl.pallas_call(
        _memkv_kernel,
        out_shape=jax.ShapeDtypeStruct((N_MEM, w.shape[1]), BF16),
        name="mem_kv",
    )(mem, g.reshape(1, D_MODEL), w)


def _proj_kernel(h_ref, g_ref, w_ref, o_ref):
    hn = _rms(h_ref[...], g_ref[...]).astype(BF16)
    o_ref[...] = jnp.dot(hn, w_ref[...], preferred_element_type=F32).astype(BF16)


def _proj(h, g, w, name):
    n = w.shape[1]
    return pl.pallas_call(
        _proj_kernel,
        grid=(SEQ // ROW_TILE,),
        in_specs=[pl.BlockSpec((ROW_TILE, D_MODEL), lambda i: (i, 0)),
                  pl.BlockSpec((1, D_MODEL), lambda i: (0, 0)),
                  pl.BlockSpec((D_MODEL, n), lambda i: (0, 0))],
        out_specs=pl.BlockSpec((ROW_TILE, n), lambda i: (i, 0)),
        out_shape=jax.ShapeDtypeStruct((SEQ, n), BF16),
        compiler_params=pltpu.CompilerParams(dimension_semantics=("arbitrary",), vmem_limit_bytes=VMEM_LIMIT),
        name=name,
    )(h, g.reshape(1, D_MODEL), w)


def _swa_kernel(tab_ref, sink_ref, bucket_ref, p_ref, pprev_ref, kvm_ref, wout_ref, h_ref, g_ref,
                h1_ref, hn1_ref, bias_sc, kv_sc, cat_sc):
    i = pl.program_id(0)

    @pl.when(i == 0)
    def _():
        _fill_bias(bias_sc, tab_ref, bucket_ref[...], 0, N_MIX_HEADS)

    kv_sc[0:BLOCK, :] = pprev_ref[...]
    kv_sc[BLOCK:, :] = p_ref[:, SWA_Q:SWA_Q + 2 * SWA_KV]
    band, own = _band_mask(SWA_WINDOW - 1)

    def block_body(b, carry):
        r0 = pl.multiple_of(b * BLOCK, BLOCK)
        valid = band & (own | (i * BLOCKS_PER_TILE + b > 0))
        qb = p_ref[pl.ds(r0, BLOCK), 0:SWA_Q]
        kvb = kv_sc[pl.ds(r0, 2 * BLOCK), :]
        outs = []
        for kvh in range(SWA_KV_HEADS):
            kk = kvb[:, kvh * HEAD_DIM:(kvh + 1) * HEAD_DIM]
            vv = kvb[:, SWA_KV + kvh * HEAD_DIM:SWA_KV + (kvh + 1) * HEAD_DIM]
            for g in range(SWA_GROUP):
                h = kvh * SWA_GROUP + g
                q = qb[:, h * HEAD_DIM:(h + 1) * HEAD_DIM] * Q_SCALE
                o, _ = _band_head(q, kk, vv, bias_sc[h], valid, sink_ref[h])
                outs.append(o.astype(BF16))
        cat_sc[pl.ds(r0, BLOCK), 0:SWA_Q] = jnp.concatenate(outs, axis=-1)
        return carry

    lax.fori_loop(0, BLOCKS_PER_TILE, block_body, 0)

    cat_sc[:, SWA_Q:] = _mem_attention(p_ref[:, SWA_Q + 2 * SWA_KV:], kvm_ref[...])
    out = h_ref[...] + jnp.dot(cat_sc[...], wout_ref[...], preferred_element_type=F32)
    h1_ref[...] = out
    hn1_ref[...] = _rms(out, g_ref[...]).astype(BF16)


def _swa_layer(p, kvm, table, sinks, wout, h, g_ffn):
    kv_cols = 2 * SWA_KV
    return pl.pallas_call(
        _swa_kernel,
        grid=(SEQ // ROW_TILE,),
        in_specs=[
            pl.BlockSpec(memory_space=pltpu.SMEM),
            pl.BlockSpec(memory_space=pltpu.SMEM),
            pl.BlockSpec((BLOCK, 2 * BLOCK), lambda i: (0, 0)),
            pl.BlockSpec((ROW_TILE, SWA_IN), lambda i: (i, 0)),
            pl.BlockSpec((BLOCK, kv_cols), lambda i: (jnp.maximum(i * BLOCKS_PER_TILE - 1, 0), SWA_Q // kv_cols)),
            pl.BlockSpec((N_MEM, 2 * MEM_W), lambda i: (0, 0)),
            pl.BlockSpec((D_MODEL, D_MODEL), lambda i: (0, 0)),
            pl.BlockSpec((ROW_TILE, D_MODEL), lambda i: (i, 0)),
            pl.BlockSpec((1, D_MODEL), lambda i: (0, 0)),
        ],
        out_specs=[pl.BlockSpec((ROW_TILE, D_MODEL), lambda i: (i, 0)),
                   pl.BlockSpec((ROW_TILE, D_MODEL), lambda i: (i, 0))],
        out_shape=[jax.ShapeDtypeStruct((SEQ, D_MODEL), F32), jax.ShapeDtypeStruct((SEQ, D_MODEL), BF16)],
        scratch_shapes=[pltpu.VMEM((N_MIX_HEADS, BLOCK, 2 * BLOCK), F32),
                        pltpu.VMEM((ROW_TILE + BLOCK, kv_cols), BF16),
                        pltpu.VMEM((ROW_TILE, D_MODEL), BF16)],
        compiler_params=pltpu.CompilerParams(dimension_semantics=("arbitrary",), vmem_limit_bytes=VMEM_LIMIT),
        name="swa_mixer",
    )(table, sinks, jnp.asarray(_bucket_map(1)), p, p, kvm, wout, h, g_ffn.reshape(1, D_MODEL))


def _dil_kernel(tab_ref, bucket_ref, q_ref, k_ref, v_ref, kp_ref, vp_ref, o_ref, l_ref,
                bias_sc, k_sc, v_sc, *, head0, max_dist):
    r = pl.program_id(0)
    n = pl.program_id(1)

    @pl.when((r == 0) & (n == 0))
    def _():
        _fill_bias(bias_sc, tab_ref, bucket_ref[...], head0, DIL_HEADS)

    k_sc[0:BLOCK, :] = kp_ref[...]
    k_sc[BLOCK:, :] = k_ref[...]
    v_sc[0:BLOCK, :] = vp_ref[...]
    v_sc[BLOCK:, :] = v_ref[...]
    band, own = _band_mask(max_dist)

    def block_body(b, carry):
        r0 = pl.multiple_of(b * BLOCK, BLOCK)
        valid = band & (own | (n * BLOCKS_PER_TILE + b > 0))
        qb = q_ref[pl.ds(r0, BLOCK), :]
        kb = k_sc[pl.ds(r0, 2 * BLOCK), :]
        vb = v_sc[pl.ds(r0, 2 * BLOCK), :]
        outs, lses = [], []
        for h in range(DIL_HEADS):
            sl = slice(h * HEAD_DIM, (h + 1) * HEAD_DIM)
            o, lse = _band_head(qb[:, sl] * Q_SCALE, kb[:, sl], vb[:, sl], bias_sc[h], valid, None)
            outs.append(o)
            lses.append(jnp.broadcast_to(lse, (BLOCK, HEAD_DIM)))
        o_ref[pl.ds(r0, BLOCK), :] = jnp.concatenate(outs, axis=-1)
        l_ref[pl.ds(r0, BLOCK), :] = jnp.concatenate(lses, axis=-1)
        return carry

    lax.fori_loop(0, BLOCKS_PER_TILE, block_body, 0)


def _dil_group(p, table, gi):
    window, d = DIL_GROUPS[gi]
    rows = SEQ // d
    slabs = DIL_IN // DIL_W
    pv = p.reshape(rows, d * DIL_IN)
    tiles = rows // ROW_TILE
    prev = lambda n: jnp.maximum(n * BLOCKS_PER_TILE - 1, 0)
    col = lambda r, c: r * slabs + gi * 3 + c
    o, l = pl.pallas_call(
        functools.partial(_dil_kernel, head0=gi * DIL_HEADS, max_dist=window // d),
        grid=(d, tiles),
        in_specs=[
            pl.BlockSpec(memory_space=pltpu.SMEM),
            pl.BlockSpec((BLOCK, 2 * BLOCK), lambda r, n: (0, 0)),
            pl.BlockSpec((ROW_TILE, DIL_W), lambda r, n: (n, col(r, 0))),
            pl.BlockSpec((ROW_TILE, DIL_W), lambda r, n: (n, col(r, 1))),
            pl.BlockSpec((ROW_TILE, DIL_W), lambda r, n: (n, col(r, 2))),
            pl.BlockSpec((BLOCK, DIL_W), lambda r, n: (prev(n), col(r, 1))),
            pl.BlockSpec((BLOCK, DIL_W), lambda r, n: (prev(n), col(r, 2))),
        ],
        out_specs=[pl.BlockSpec((ROW_TILE, DIL_W), lambda r, n: (n, r)),
                   pl.BlockSpec((ROW_TILE, DIL_W), lambda r, n: (n, r))],
        out_shape=[jax.ShapeDtypeStruct((rows, d * DIL_W), F32), jax.ShapeDtypeStruct((rows, d * DIL_W), F32)],
        scratch_shapes=[pltpu.VMEM((DIL_HEADS, BLOCK, 2 * BLOCK), F32),
                        pltpu.VMEM((ROW_TILE + BLOCK, DIL_W), BF16),
                        pltpu.VMEM((ROW_TILE + BLOCK, DIL_W), BF16)],
        compiler_params=pltpu.CompilerParams(dimension_semantics=("arbitrary", "arbitrary"),
                                             vmem_limit_bytes=VMEM_LIMIT),
        name=f"dil_attn_{gi}",
    )(table, jnp.asarray(_bucket_map(d)), pv, pv, pv, pv, pv)
    return o.reshape(SEQ, DIL_W), l.reshape(SEQ, DIL_W)


def _dilout_kernel(o0_ref, o1_ref, o2_ref, l0_ref, l1_ref, l2_ref, qm_ref, kvm_ref, wout_ref, h_ref, g_ref,
                   router_ref, h2_ref, hn2_ref, meta_ref, cnt_ref, carry_sc):
    @pl.when(pl.program_id(0) == 0)
    def _():
        carry_sc[...] = jnp.zeros_like(carry_sc)

    l0, l1, l2 = l0_ref[...], l1_ref[...], l2_ref[...]
    mx = jnp.maximum(jnp.maximum(l0, l1), l2)
    e0, e1, e2 = jnp.exp(l0 - mx), jnp.exp(l1 - mx), jnp.exp(l2 - mx)
    inv = 1.0 / (e0 + e1 + e2)
    o = (e0 * inv) * o0_ref[...] + (e1 * inv) * o1_ref[...] + (e2 * inv) * o2_ref[...]
    cat = jnp.concatenate([o.astype(BF16), _mem_attention(qm_ref[...], kvm_ref[...])], axis=-1)
    out = h_ref[...] + jnp.dot(cat, wout_ref[...], preferred_element_type=F32)
    h2_ref[...] = out
    hn = _rms(out, g_ref[...]).astype(BF16)
    hn2_ref[...] = hn

    logits = jnp.dot(hn, router_ref[...], preferred_element_type=F32)
    lane = lax.broadcasted_iota(jnp.int32, logits.shape, 1)
    masked = jnp.where(lane < N_EXPERTS, logits, -jnp.inf)
    v0 = jnp.max(masked, axis=-1, keepdims=True)
    i0 = jnp.min(jnp.where(masked == v0, lane, LANES), axis=-1, keepdims=True)
    rest = jnp.where(lane == i0, -jnp.inf, masked)
    v1 = jnp.max(rest, axis=-1, keepdims=True)
    i1 = jnp.min(jnp.where(rest == v1, lane, LANES), axis=-1, keepdims=True)
    ex = jnp.exp(v1 - v0)
    inv2 = 1.0 / (1.0 + ex)

    oh0, oh1 = lane == i0, lane == i1
    sel = (oh0 | oh1).astype(F32)
    tok_r = lax.broadcasted_iota(jnp.int32, (ROW_TILE, ROW_TILE), 0)
    tok_c = lax.broadcasted_iota(jnp.int32, (ROW_TILE, ROW_TILE), 1)
    earlier = (tok_c < tok_r).astype(F32).astype(BF16)
    before = jnp.dot(earlier, sel.astype(BF16), preferred_element_type=F32) + carry_sc[...]
    rank0 = jnp.sum(jnp.where(oh0, before, 0.0), axis=-1, keepdims=True)
    rank1 = jnp.sum(jnp.where(oh1, before, 0.0), axis=-1, keepdims=True)
    count = carry_sc[...] + jnp.sum(sel, axis=0, keepdims=True)
    carry_sc[...] = count
    cnt_ref[...] = jnp.broadcast_to(count, cnt_ref.shape)

    fields = (i0.astype(F32), i1.astype(F32), inv2, ex * inv2, rank0, rank1)
    meta = jnp.zeros(logits.shape, F32)
    for k, f in enumerate(fields):
        meta = jnp.where(lane == k, f, meta)
    meta_ref[...] = meta


META_E0, META_E1, META_G0, META_G1, META_RANK0, META_RANK1 = range(6)


def _dilout(os, ls, p, kvm, wout, h, g_ffn, router):
    row = lambda i: (i, 0)
    const = lambda i: (0, 0)
    slab = pl.BlockSpec((ROW_TILE, DIL_W), row)
    return pl.pallas_call(
        _dilout_kernel,
        grid=(SEQ // ROW_TILE,),
        in_specs=[slab, slab, slab, slab, slab, slab,
                  pl.BlockSpec((ROW_TILE, MEM_W), lambda i: (i, DIL_IN // MEM_W - 1)),
                  pl.BlockSpec((N_MEM, 2 * MEM_W), lambda i: (0, 1)),
                  pl.BlockSpec((DIL_W + MEM_W, D_MODEL), const),
                  pl.BlockSpec((ROW_TILE, D_MODEL), row),
                  pl.BlockSpec((1, D_MODEL), const),
                  pl.BlockSpec((D_MODEL, LANES), const)],
        out_specs=[pl.BlockSpec((ROW_TILE, D_MODEL), row),
                   pl.BlockSpec((ROW_TILE, D_MODEL), row),
                   pl.BlockSpec((ROW_TILE, LANES), row),
                   pl.BlockSpec((SUBLANES, LANES), const)],
        out_shape=[jax.ShapeDtypeStruct((SEQ, D_MODEL), F32), jax.ShapeDtypeStruct((SEQ, D_MODEL), BF16),
                   jax.ShapeDtypeStruct((SEQ, LANES), F32), jax.ShapeDtypeStruct((SUBLANES, LANES), F32)],
        scratch_shapes=[pltpu.VMEM((1, LANES), F32)],
        compiler_params=pltpu.CompilerParams(dimension_semantics=("arbitrary",), vmem_limit_bytes=VMEM_LIMIT),
        name="dil_out",
    )(*os, *ls, p, kvm, wout, h, g_ffn.reshape(1, D_MODEL), router)


def _swiglu_partial(x, wg, wu, wd):
    a = jnp.dot(x, wg, preferred_element_type=F32)
    b = jnp.dot(x, wu, preferred_element_type=F32)
    hm = (a * jax.nn.sigmoid(a) * b).astype(BF16)
    return jnp.dot(hm, wd, preferred_element_type=F32)


def _ffn_kernel(x_ref, wg_ref, wu_ref, wd_ref, h_ref, o_ref, acc_sc):
    j = pl.program_id(1)

    @pl.when(j == 0)
    def _():
        acc_sc[...] = jnp.zeros_like(acc_sc)

    acc_sc[...] += _swiglu_partial(x_ref[...], wg_ref[...], wu_ref[...], wd_ref[...])

    @pl.when(j == pl.num_programs(1) - 1)
    def _():
        o_ref[...] = h_ref[...] + acc_sc[...]


def _ffn(x, h, wg, wu, wd):
    tm, tf = FFN_ROW_TILE, FFN_COL_TILE
    return pl.pallas_call(
        _ffn_kernel,
        grid=(SEQ // tm, D_FF // tf),
        in_specs=[pl.BlockSpec((tm, D_MODEL), lambda i, j: (i, 0)),
                  pl.BlockSpec((D_MODEL, tf), lambda i, j: (0, j)),
                  pl.BlockSpec((D_MODEL, tf), lambda i, j: (0, j)),
                  pl.BlockSpec((tf, D_MODEL), lambda i, j: (j, 0)),
                  pl.BlockSpec((tm, D_MODEL), lambda i, j: (i, 0))],
        out_specs=pl.BlockSpec((tm, D_MODEL), lambda i, j: (i, 0)),
        out_shape=jax.ShapeDtypeStruct((SEQ, D_MODEL), F32),
        scratch_shapes=[pltpu.VMEM((tm, D_MODEL), F32)],
        compiler_params=pltpu.CompilerParams(dimension_semantics=("arbitrary", "arbitrary"),
                                             vmem_limit_bytes=VMEM_LIMIT),
        name="ffn",
    )(x, wg, wu, wd, h)


MOE_SUB = 256
MOE_ROWS = 2048
MOE_SUBS = MOE_ROWS // MOE_SUB
MOE_SLOTS = 2 * SEQ + N_EXPERTS * MOE_SUB
MOE_TILES = MOE_SLOTS // MOE_ROWS
MOE_ITEMS = MOE_TILES + N_EXPERTS - 1
PACK_W = D_MODEL // 2
ITEM_TILE, ITEM_EXPERT, ITEM_LO, ITEM_HI, ITEM_FIRST = range(5)


def _pack_bf16_pairs(x):
    lo = lax.bitcast_convert_type(x[:, :PACK_W].astype(F32), jnp.uint32) >> 16
    hi = lax.bitcast_convert_type(x[:, PACK_W:].astype(F32), jnp.uint32) & jnp.uint32(0xFFFF0000)
    return hi | lo


def _unpack_bf16_pairs(pk):
    lo = lax.bitcast_convert_type(pk << 16, F32).astype(BF16)
    hi = lax.bitcast_convert_type(pk & jnp.uint32(0xFFFF0000), F32).astype(BF16)
    return lo, hi


def _row_copy(src, src_row, dst, dst_row, sem):
    return pltpu.make_async_copy(src.at[pl.ds(src_row, 1), :], dst.at[pl.ds(dst_row, 1), :], sem)


def _scatter_kernel(pos0_ref, pos1_ref, x_ref, xs_in_ref, xs_ref, pk_sc, sem):
    del xs_in_ref
    pk_sc[...] = _pack_bf16_pairs(x_ref[...])

    def issue(t, c):
        _row_copy(pk_sc, t, xs_ref, pos0_ref[t], sem).start()
        _row_copy(pk_sc, t, xs_ref, pos1_ref[t], sem).start()
        return c

    def drain(t, c):
        _row_copy(pk_sc, 0, xs_ref, 0, sem).wait()
        _row_copy(pk_sc, 0, xs_ref, 0, sem).wait()
        return c

    lax.fori_loop(0, ROW_TILE, issue, 0)
    lax.fori_loop(0, ROW_TILE, drain, 0)


def _moe_scatter(x, pos0, pos1):
    smem_tile = pl.BlockSpec((ROW_TILE,), lambda i: (i,), memory_space=pltpu.SMEM)
    return pl.pallas_call(
        _scatter_kernel,
        grid=(SEQ // ROW_TILE,),
        in_specs=[smem_tile, smem_tile,
                  pl.BlockSpec((ROW_TILE, D_MODEL), lambda i: (i, 0)),
                  pl.BlockSpec(memory_space=pl.ANY)],
        out_specs=pl.BlockSpec(memory_space=pl.ANY),
        out_shape=jax.ShapeDtypeStruct((MOE_SLOTS, PACK_W), jnp.uint32),
        scratch_shapes=[pltpu.VMEM((ROW_TILE, PACK_W), jnp.uint32), pltpu.SemaphoreType.DMA],
        input_output_aliases={3: 0},
        compiler_params=pltpu.CompilerParams(dimension_semantics=("arbitrary",), vmem_limit_bytes=VMEM_LIMIT),
        name="moe_scatter",
    )(pos0, pos1, x, jnp.zeros((MOE_SLOTS, PACK_W), jnp.uint32))


def _moe_kernel(items_ref, xs_ref, wg_ref, wu_ref, wd_ref, y_ref, xb_sc):
    w = pl.program_id(0)
    j = pl.program_id(1)
    lo = items_ref[ITEM_LO, w]
    hi = items_ref[ITEM_HI, w]

    @pl.when((j == 0) & (hi > lo))
    def _():
        xlo, xhi = _unpack_bf16_pairs(xs_ref[...])
        xb_sc[:, :PACK_W] = xlo
        xb_sc[:, PACK_W:] = xhi

    @pl.when((j == 0) & (items_ref[ITEM_FIRST, w] == 1))
    def _():
        y_ref[...] = jnp.zeros_like(y_ref)

    for sb in range(MOE_SUBS):
        @pl.when((lo <= sb) & (sb < hi))
        def _(sb=sb):
            rows = slice(sb * MOE_SUB, (sb + 1) * MOE_SUB)
            y_ref[rows, :] += _swiglu_partial(xb_sc[rows, :], wg_ref[0], wu_ref[0], wd_ref[0])


def _moe_grouped(items, xs, wg, wu, wd):
    tf = FFN_COL_TILE
    nf = D_FF // tf
    jf = lambda w, j, it: jnp.where(it[ITEM_HI, w] > it[ITEM_LO, w], j, nf - 1)
    grid_spec = pltpu.PrefetchScalarGridSpec(
        num_scalar_prefetch=1,
        grid=(MOE_ITEMS, nf),
        in_specs=[pl.BlockSpec((MOE_ROWS, PACK_W), lambda w, j, it: (it[ITEM_TILE, w], 0)),
                  pl.BlockSpec((1, D_MODEL, tf), lambda w, j, it: (it[ITEM_EXPERT, w], 0, jf(w, j, it))),
                  pl.BlockSpec((1, D_MODEL, tf), lambda w, j, it: (it[ITEM_EXPERT, w], 0, jf(w, j, it))),
                  pl.BlockSpec((1, tf, D_MODEL), lambda w, j, it: (it[ITEM_EXPERT, w], jf(w, j, it), 0))],
        out_specs=pl.BlockSpec((MOE_ROWS, D_MODEL), lambda w, j, it: (it[ITEM_TILE, w], 0)),
        scratch_shapes=[pltpu.VMEM((MOE_ROWS, D_MODEL), BF16)],
    )
    return pl.pallas_call(
        _moe_kernel,
        grid_spec=grid_spec,
        out_shape=jax.ShapeDtypeStruct((MOE_SLOTS, D_MODEL), F32),
        compiler_params=pltpu.CompilerParams(dimension_semantics=("arbitrary", "arbitrary"),
                                             vmem_limit_bytes=VMEM_LIMIT),
        name="moe_grouped",
    )(items, xs, wg, wu, wd)


def _combine_kernel(pos0_ref, pos1_ref, meta_ref, h_ref, gf_ref, y_ref, o_ref, y0_sc, y1_sc, sem):
    def issue(t, c):
        _row_copy(y_ref, pos0_ref[t], y0_sc, t, sem).start()
        _row_copy(y_ref, pos1_ref[t], y1_sc, t, sem).start()
        return c

    def drain(t, c):
        _row_copy(y_ref, 0, y0_sc, 0, sem).wait()
        _row_copy(y_ref, 0, y1_sc, 0, sem).wait()
        return c

    lax.fori_loop(0, ROW_TILE, issue, 0)
    lax.fori_loop(0, ROW_TILE, drain, 0)
    meta = meta_ref[...]
    g0 = meta[:, META_G0:META_G0 + 1]
    g1 = meta[:, META_G1:META_G1 + 1]
    o_ref[...] = _rms(h_ref[...] + (g0 * y0_sc[...] + g1 * y1_sc[...]), gf_ref[...])


def _moe_combine(pos0, pos1, meta, h, g_final, y):
    smem_tile = pl.BlockSpec((ROW_TILE,), lambda i: (i,), memory_space=pltpu.SMEM)
    return pl.pallas_call(
        _combine_kernel,
        grid=(SEQ // ROW_TILE,),
        in_specs=[smem_tile, smem_tile,
                  pl.BlockSpec((ROW_TILE, LANES), lambda i: (i, 0)),
                  pl.BlockSpec((ROW_TILE, D_MODEL), lambda i: (i, 0)),
                  pl.BlockSpec((1, D_MODEL), lambda i: (0, 0)),
                  pl.BlockSpec(memory_space=pl.ANY)],
        out_specs=pl.BlockSpec((ROW_TILE, D_MODEL), lambda i: (i, 0)),
        out_shape=jax.ShapeDtypeStruct((SEQ, D_MODEL), F32),
        scratch_shapes=[pltpu.VMEM((ROW_TILE, D_MODEL), F32), pltpu.VMEM((ROW_TILE, D_MODEL), F32),
                        pltpu.SemaphoreType.DMA],
        compiler_params=pltpu.CompilerParams(dimension_semantics=("arbitrary",), vmem_limit_bytes=VMEM_LIMIT),
        name="moe_combine",
    )(pos0, pos1, meta, h, g_final.reshape(1, D_MODEL), y)


def _moe_plan(meta, counts):
    i32 = jnp.int32
    cnt = counts[0, :N_EXPERTS].astype(i32)
    padded = (cnt + MOE_SUB - 1) // MOE_SUB * MOE_SUB
    start = jnp.cumsum(padded) - padded
    e0, e1 = meta[:, META_E0].astype(i32), meta[:, META_E1].astype(i32)
    pos0 = start[e0] + meta[:, META_RANK0].astype(i32)
    pos1 = start[e1] + meta[:, META_RANK1].astype(i32)

    first_sb, end_sb = start // MOE_SUB, (start + padded) // MOE_SUB
    t0 = jnp.arange(MOE_TILES, dtype=i32)[:, None] * MOE_SUBS
    lo = jnp.clip(first_sb[None, :] - t0, 0, MOE_SUBS)
    hi = jnp.clip(end_sb[None, :] - t0, 0, MOE_SUBS)
    active = hi > lo
    none = ~jnp.any(active, axis=1, keepdims=True)
    active = jnp.concatenate([active, none], axis=1)
    lo = jnp.concatenate([lo, jnp.zeros_like(t0)], axis=1)
    hi = jnp.concatenate([hi, jnp.zeros_like(t0)], axis=1)
    first = active & (jnp.cumsum(active.astype(i32), axis=1) == 1)
    ncol = N_EXPERTS + 1
    flat = jnp.arange(MOE_TILES * ncol, dtype=i32)
    key = jnp.where(active.reshape(-1), flat, MOE_TILES * ncol)
    order = jnp.argsort(key)[:MOE_ITEMS]
    live = key[order] < MOE_TILES * ncol
    tile = jnp.where(live, order // ncol, MOE_TILES - 1)
    lo_i = jnp.where(live, lo.reshape(-1)[order], 0)
    hi_i = jnp.where(live, hi.reshape(-1)[order], 0)
    first_i = jnp.where(live, first.reshape(-1)[order], False).astype(i32)
    idx = jnp.arange(MOE_ITEMS, dtype=i32)
    src = jnp.maximum(lax.cummax(jnp.where(hi_i > lo_i, idx, -1), axis=0), 0)
    expert = jnp.minimum(order % ncol, N_EXPERTS - 1)[src]
    items = jnp.stack([tile, expert, lo_i, hi_i, first_i]).astype(i32)
    return pos0, pos1, items


def kernel(x, mem, rel_bias_table, mem_norm, norm_mix, norm_ffn, final_norm, swa_w_in, swa_sinks, swa_w_mem_kv,
           swa_w_out, dil_w_in, dil_w_mem_kv, dil_w_out, ffn_gate, ffn_up, ffn_down, router, moe_gate, moe_up,
           moe_down):
    assert x.shape == (1, SEQ, D_MODEL) and mem.shape == (1, N_MEM, D_MODEL)
    assert norm_mix.shape == (2, D_MODEL) and swa_w_in.shape == (1, D_MODEL, SWA_IN)
    assert dil_w_in.shape == (1, D_MODEL, DIL_IN) and moe_gate.shape == (1, N_EXPERTS, D_MODEL, D_FF)
    bf = lambda a: a.astype(BF16)
    h0 = x.reshape(SEQ, D_MODEL)

    kvm = _memkv(mem[0], mem_norm, bf(jnp.concatenate([swa_w_mem_kv[0], dil_w_mem_kv[0]], axis=1)))

    p0 = _proj(h0, norm_mix[0], bf(swa_w_in[0]), "swa_proj")
    h1, hn1 = _swa_layer(p0, kvm, rel_bias_table, swa_sinks[0], bf(swa_w_out[0]), h0, norm_ffn[0])
    h2 = _ffn(hn1, h1, bf(ffn_gate[0]), bf(ffn_up[0]), bf(ffn_down[0]))

    p1 = _proj(h2, norm_mix[1], bf(dil_w_in[0]), "dil_proj")
    groups = [_dil_group(p1, rel_bias_table, gi) for gi in range(len(DIL_GROUPS))]
    router_p = jnp.pad(bf(router[0]), ((0, 0), (0, LANES - N_EXPERTS)))
    h3, hn3, meta, counts = _dilout([g[0] for g in groups], [g[1] for g in groups], p1, kvm, bf(dil_w_out[0]), h2,
                                    norm_ffn[1], router_p)
    pos0, pos1, items = _moe_plan(meta, counts)
    xs = _moe_scatter(hn3, pos0, pos1)
    y = _moe_grouped(items, xs, bf(moe_gate[0]), bf(moe_up[0]), bf(moe_down[0]))
    out = _moe_combine(pos0, pos1, meta, h3, final_norm, y)
    return out.reshape(1, SEQ, D_MODEL)
```

```python
import functools
import math

import jax
import jax.numpy as jnp
import numpy as np
from jax import lax
from jax.experimental import pallas as pl
from jax.experimental.pallas import tpu as pltpu

F32 = jnp.float32
BF16 = jnp.bfloat16

D_MODEL = 1024
SEQ = 16384
HEAD_DIM = 64
N_MIX_HEADS = 12
SWA_KV_HEADS = 3
SWA_GROUP = N_MIX_HEADS // SWA_KV_HEADS
SWA_WINDOW = 128
DIL_GROUPS = ((128, 1), (512, 4), (2048, 16))
DIL_HEADS = 4
N_MEM = 256
MEM_HEADS = 4
BLOCK = 128
N_BUCKETS = 32
MAX_DISTANCE = 2048
D_FF = 3584
N_EXPERTS = 8
EPS = 1e-5
NEG = -1e30
Q_SCALE = HEAD_DIM ** -0.5

SWA_Q = N_MIX_HEADS * HEAD_DIM
SWA_KV = SWA_KV_HEADS * HEAD_DIM
SWA_IN = SWA_Q + 2 * SWA_KV + MEM_HEADS * HEAD_DIM
DIL_W = DIL_HEADS * HEAD_DIM
DIL_IN = len(DIL_GROUPS) * 3 * DIL_W + MEM_HEADS * HEAD_DIM
MEM_W = MEM_HEADS * HEAD_DIM

LANES = 128
SUBLANES = 8
VMEM_LIMIT = 56 * 1024 * 1024

ROW_TILE = 512
BLOCKS_PER_TILE = ROW_TILE // BLOCK
FFN_ROW_TILE = 1024
FFN_COL_TILE = 512


def _bucket_map(dil):
    qi = np.arange(BLOCK)[:, None]
    kj = np.arange(2 * BLOCK)[None, :]
    d = np.maximum((qi + BLOCK - kj) * dil, 0)
    max_exact = N_BUCKETS // 2
    ratio = np.maximum(d, 1).astype(np.float32) / np.float32(max_exact)
    large = max_exact + (np.log(ratio) / np.float32(math.log(MAX_DISTANCE / max_exact))
                         * np.float32(N_BUCKETS - max_exact)).astype(np.int32)
    return np.where(d < max_exact, d, np.minimum(large, N_BUCKETS - 1)).astype(np.int32)


def _rms(x, g):
    ms = jnp.mean(x * x, axis=-1, keepdims=True)
    return x * lax.rsqrt(ms + EPS) * g


def _dot_nt(a, b):
    return lax.dot_general(a, b, (((1,), (1,)), ((), ())), preferred_element_type=F32)


def _fill_bias(bias_sc, tab_ref, bucket, head0, n_heads):
    for h in range(n_heads):
        def body(k, b, h=h):
            return jnp.where(bucket == k, tab_ref[k, head0 + h], b)
        bias_sc[h] = lax.fori_loop(0, N_BUCKETS, body, jnp.zeros(bucket.shape, F32))


def _band_mask(max_dist):
    qi = lax.broadcasted_iota(jnp.int32, (BLOCK, 2 * BLOCK), 0)
    kj = lax.broadcasted_iota(jnp.int32, (BLOCK, 2 * BLOCK), 1)
    dist = qi + BLOCK - kj
    return (dist >= 0) & (dist <= max_dist), kj >= BLOCK


def _band_head(q, kk, vv, bias, valid, sink):
    s = _dot_nt(q, kk)
    s = jnp.where(valid, s + bias, NEG)
    m = jnp.max(s, axis=-1, keepdims=True)
    if sink is not None:
        m = jnp.maximum(m, sink)
    e = jnp.exp(s - m)
    den = jnp.sum(e, axis=-1, keepdims=True)
    if sink is not None:
        den = den + jnp.exp(sink - m)
    p = (e * (1.0 / den)).astype(BF16)
    return jnp.dot(p, vv, preferred_element_type=F32), m + jnp.log(den)


def _mem_attention(qm, kvm):
    outs = []
    for h in range(MEM_HEADS):
        q = qm[:, h * HEAD_DIM:(h + 1) * HEAD_DIM] * Q_SCALE
        km = kvm[:, h * HEAD_DIM:(h + 1) * HEAD_DIM]
        vm = kvm[:, MEM_W + h * HEAD_DIM:MEM_W + (h + 1) * HEAD_DIM]
        s = _dot_nt(q, km)
        e = jnp.exp(s - jnp.max(s, axis=-1, keepdims=True))
        p = (e * (1.0 / jnp.sum(e, axis=-1, keepdims=True))).astype(BF16)
        outs.append(jnp.dot(p, vm, preferred_element_type=F32).astype(BF16))
    return jnp.concatenate(outs, axis=-1)


def _memkv_kernel(mem_ref, g_ref, w_ref, o_ref):
    mn = _rms(mem_ref[...], g_ref[...]).astype(BF16)
    o_ref[...] = jnp.dot(mn, w_ref[...], preferred_element_type=F32).astype(BF16)


def _memkv(mem, g, w):
    return pl.pallas_call(
        _memkv_kernel,
        out_shape=jax.ShapeDtypeStruct((N_MEM, w.shape[1]), BF16),
        name="mem_kv",
    )(mem, g.reshape(1, D_MODEL), w)


def _proj_kernel(h_ref, g_ref, w_ref, o_ref):
    hn = _rms(h_ref[...], g_ref[...]).astype(BF16)
    o_ref[...] = jnp.dot(hn, w_ref[...], preferred_element_type=F32).astype(BF16)


def _proj(h, g, w, name):
    n = w.shape[1]
    return pl.pallas_call(
        _proj_kernel,
        grid=(SEQ // ROW_TILE,),
        in_specs=[pl.BlockSpec((ROW_TILE, D_MODEL), lambda i: (i, 0)),
                  pl.BlockSpec((1, D_MODEL), lambda i: (0, 0)),
                  pl.BlockSpec((D_MODEL, n), lambda i: (0, 0))],
        out_specs=pl.BlockSpec((ROW_TILE, n), lambda i: (i, 0)),
        out_shape=jax.ShapeDtypeStruct((SEQ, n), BF16),
        compiler_params=pltpu.CompilerParams(dimension_semantics=("arbitrary",), vmem_limit_bytes=VMEM_LIMIT),
        name=name,
    )(h, g.reshape(1, D_MODEL), w)


def _swa_kernel(tab_ref, sink_ref, bucket_ref, p_ref, pprev_ref, kvm_ref, wout_ref, h_ref, g_ref,
                h1_ref, hn1_ref, bias_sc, kv_sc, cat_sc):
    i = pl.program_id(0)

    @pl.when(i == 0)
    def _():
        _fill_bias(bias_sc, tab_ref, bucket_ref[...], 0, N_MIX_HEADS)

    kv_sc[0:BLOCK, :] = pprev_ref[...]
    kv_sc[BLOCK:, :] = p_ref[:, SWA_Q:SWA_Q + 2 * SWA_KV]
    band, own = _band_mask(SWA_WINDOW - 1)

    def block_body(b, carry):
        r0 = pl.multiple_of(b * BLOCK, BLOCK)
        valid = band & (own | (i * BLOCKS_PER_TILE + b > 0))
        qb = p_ref[pl.ds(r0, BLOCK), 0:SWA_Q]
        kvb = kv_sc[pl.ds(r0, 2 * BLOCK), :]
        outs = []
        for kvh in range(SWA_KV_HEADS):
            kk = kvb[:, kvh * HEAD_DIM:(kvh + 1) * HEAD_DIM]
            vv = kvb[:, SWA_KV + kvh * HEAD_DIM:SWA_KV + (kvh + 1) * HEAD_DIM]
            for g in range(SWA_GROUP):
                h = kvh * SWA_GROUP + g
                q = qb[:, h * HEAD_DIM:(h + 1) * HEAD_DIM] * Q_SCALE
                o, _ = _band_head(q, kk, vv, bias_sc[h], valid, sink_ref[h])
                outs.append(o.astype(BF16))
        cat_sc[pl.ds(r0, BLOCK), 0:SWA_Q] = jnp.concatenate(outs, axis=-1)
        return carry

    lax.fori_loop(0, BLOCKS_PER_TILE, block_body, 0)

    cat_sc[:, SWA_Q:] = _mem_attention(p_ref[:, SWA_Q + 2 * SWA_KV:], kvm_ref[...])
    out = h_ref[...] + jnp.dot(cat_sc[...], wout_ref[...], preferred_element_type=F32)
    h1_ref[...] = out
    hn1_ref[...] = _rms(out, g_ref[...]).astype(BF16)


def _swa_layer(p, kvm, table, sinks, wout, h, g_ffn):
    kv_cols = 2 * SWA_KV
    return pl.pallas_call(
        _swa_kernel,
        grid=(SEQ // ROW_TILE,),
        in_specs=[
            pl.BlockSpec(memory_space=pltpu.SMEM),
            pl.BlockSpec(memory_space=pltpu.SMEM),
            pl.BlockSpec((BLOCK, 2 * BLOCK), lambda i: (0, 0)),
            pl.BlockSpec((ROW_TILE, SWA_IN), lambda i: (i, 0)),
            pl.BlockSpec((BLOCK, kv_cols), lambda i: (jnp.maximum(i * BLOCKS_PER_TILE - 1, 0), SWA_Q // kv_cols)),
            pl.BlockSpec((N_MEM, 2 * MEM_W), lambda i: (0, 0)),
            pl.BlockSpec((D_MODEL, D_MODEL), lambda i: (0, 0)),
            pl.BlockSpec((ROW_TILE, D_MODEL), lambda i: (i, 0)),
            pl.BlockSpec((1, D_MODEL), lambda i: (0, 0)),
        ],
        out_specs=[pl.BlockSpec((ROW_TILE, D_MODEL), lambda i: (i, 0)),
                   pl.BlockSpec((ROW_TILE, D_MODEL), lambda i: (i, 0))],
        out_shape=[jax.ShapeDtypeStruct((SEQ, D_MODEL), F32), jax.ShapeDtypeStruct((SEQ, D_MODEL), BF16)],
        scratch_shapes=[pltpu.VMEM((N_MIX_HEADS, BLOCK, 2 * BLOCK), F32),
                        pltpu.VMEM((ROW_TILE + BLOCK, kv_cols), BF16),
                        pltpu.VMEM((ROW_TILE, D_MODEL), BF16)],
        compiler_params=pltpu.CompilerParams(dimension_semantics=("arbitrary",), vmem_limit_bytes=VMEM_LIMIT),
        name="swa_mixer",
    )(table, sinks, jnp.asarray(_bucket_map(1)), p, p, kvm, wout, h, g_ffn.reshape(1, D_MODEL))


DIL_SLAB = 3 * DIL_W
TOK_W = DIL_SLAB + MEM_W
SLABS_PER_GROUP = DIL_SLAB // LANES


def _dilproj_kernel(h_ref, g_ref, w_ref, tok_ref, g1_ref, g2_ref, slab_sc):
    hn = _rms(h_ref[...], g_ref[...]).astype(BF16)
    res = jnp.dot(hn, w_ref[...], preferred_element_type=F32)
    tok_ref[...] = res[:, :TOK_W].astype(BF16)
    for s in range(2 * SLABS_PER_GROUP):
        slab_sc[s] = res[:, TOK_W + s * LANES:TOK_W + (s + 1) * LANES]
    for gi, out_ref in ((1, g1_ref), (2, g2_ref)):
        d = DIL_GROUPS[gi][1]
        for s in range(SLABS_PER_GROUP):
            for r in range(d):
                rows = slab_sc[(gi - 1) * SLABS_PER_GROUP + s, pl.ds(r, ROW_TILE // d, stride=d), :]
                out_ref[r, :, s * LANES:(s + 1) * LANES] = rows.astype(BF16)


def _dilproj(h, g, w):
    d1, d2 = DIL_GROUPS[1][1], DIL_GROUPS[2][1]
    return pl.pallas_call(
        _dilproj_kernel,
        grid=(SEQ // ROW_TILE,),
        in_specs=[pl.BlockSpec((ROW_TILE, D_MODEL), lambda i: (i, 0)),
                  pl.BlockSpec((1, D_MODEL), lambda i: (0, 0)),
                  pl.BlockSpec((D_MODEL, DIL_IN), lambda i: (0, 0))],
        out_specs=[pl.BlockSpec((ROW_TILE, TOK_W), lambda i: (i, 0)),
                   pl.BlockSpec((d1, ROW_TILE // d1, DIL_SLAB), lambda i: (0, i, 0)),
                   pl.BlockSpec((d2, ROW_TILE // d2, DIL_SLAB), lambda i: (0, i, 0))],
        out_shape=[jax.ShapeDtypeStruct((SEQ, TOK_W), BF16),
                   jax.ShapeDtypeStruct((d1, SEQ // d1, DIL_SLAB), BF16),
                   jax.ShapeDtypeStruct((d2, SEQ // d2, DIL_SLAB), BF16)],
        scratch_shapes=[pltpu.VMEM((2 * SLABS_PER_GROUP, ROW_TILE, LANES), F32)],
        compiler_params=pltpu.CompilerParams(dimension_semantics=("arbitrary",), vmem_limit_bytes=VMEM_LIMIT),
        name="dil_proj",
    )(h, g.reshape(1, D_MODEL), w)


def _dil_kernel(tab_ref, bucket_ref, q_ref, k_ref, v_ref, kp_ref, vp_ref, o_ref, l_ref,
                bias_sc, k_sc, v_sc, *, head0, max_dist):
    r = pl.program_id(0)
    n = pl.program_id(1)

    @pl.when((r == 0) & (n == 0))
    def _():
        _fill_bias(bias_sc, tab_ref, bucket_ref[...], head0, DIL_HEADS)

    k_sc[0:BLOCK, :] = kp_ref[...]
    k_sc[BLOCK:, :] = k_ref[...]
    v_sc[0:BLOCK, :] = vp_ref[...]
    v_sc[BLOCK:, :] = v_ref[...]
    band, own = _band_mask(max_dist)

    def block_body(b, carry):
        r0 = pl.multiple_of(b * BLOCK, BLOCK)
        valid = band & (own | (n * BLOCKS_PER_TILE + b > 0))
        qb = q_ref[pl.ds(r0, BLOCK), :]
        kb = k_sc[pl.ds(r0, 2 * BLOCK), :]
        vb = v_sc[pl.ds(r0, 2 * BLOCK), :]
        outs, lses = [], []
        for h in range(DIL_HEADS):
            sl = slice(h * HEAD_DIM, (h + 1) * HEAD_DIM)
            o, lse = _band_head(qb[:, sl] * Q_SCALE, kb[:, sl], vb[:, sl], bias_sc[h], valid, None)
            outs.append(o)
            lses.append(jnp.broadcast_to(lse, (BLOCK, HEAD_DIM)))
        o_ref[pl.ds(r0, BLOCK), :] = jnp.concatenate(outs, axis=-1)
        l_ref[pl.ds(r0, BLOCK), :] = jnp.concatenate(lses, axis=-1)
        return carry

    lax.fori_loop(0, BLOCKS_PER_TILE, block_body, 0)


def _dil_group(qkv, table, gi):
    window, d = DIL_GROUPS[gi]
    rows = SEQ // d
    assert qkv.shape[:2] == (d, rows)
    prev = lambda n: jnp.maximum(n * BLOCKS_PER_TILE - 1, 0)
    tile = lambda c: pl.BlockSpec((None, ROW_TILE, DIL_W), lambda r, n: (r, n, c))
    prev_block = lambda c: pl.BlockSpec((None, BLOCK, DIL_W), lambda r, n: (r, prev(n), c))
    return pl.pallas_call(
        functools.partial(_dil_kernel, head0=gi * DIL_HEADS, max_dist=window // d),
        grid=(d, rows // ROW_TILE),
        in_specs=[
            pl.BlockSpec(memory_space=pltpu.SMEM),
            pl.BlockSpec((BLOCK, 2 * BLOCK), lambda r, n: (0, 0)),
            tile(0), tile(1), tile(2), prev_block(1), prev_block(2),
        ],
        out_specs=[tile(0), tile(0)],
        out_shape=[jax.ShapeDtypeStruct((d, rows, DIL_W), F32), jax.ShapeDtypeStruct((d, rows, DIL_W), F32)],
        scratch_shapes=[pltpu.VMEM((DIL_HEADS, BLOCK, 2 * BLOCK), F32),
                        pltpu.VMEM((ROW_TILE + BLOCK, DIL_W), BF16),
                        pltpu.VMEM((ROW_TILE + BLOCK, DIL_W), BF16)],
        compiler_params=pltpu.CompilerParams(dimension_semantics=("arbitrary", "arbitrary"),
                                             vmem_limit_bytes=VMEM_LIMIT),
        name=f"dil_attn_{gi}",
    )(table, jnp.asarray(_bucket_map(d)), qkv, qkv, qkv, qkv, qkv)


def _dilout_kernel(o0_ref, o1_ref, o2_ref, l0_ref, l1_ref, l2_ref, qm_ref, kvm_ref, wout_ref, h_ref, g_ref,
                   router_ref, h2_ref, hn2_ref, meta_ref, cnt_ref, carry_sc, tok_sc):
    @pl.when(pl.program_id(0) == 0)
    def _():
        carry_sc[...] = jnp.zeros_like(carry_sc)

    for k, src_ref in enumerate((o1_ref, l1_ref, o2_ref, l2_ref)):
        d = src_ref.shape[0]
        for s in range(DIL_W // LANES):
            for r in range(d):
                tok_sc[k, s, pl.ds(r, ROW_TILE // d, stride=d), :] = src_ref[r, :, s * LANES:(s + 1) * LANES]

    mixed = []
    for s in range(DIL_W // LANES):
        cols = slice(s * LANES, (s + 1) * LANES)
        o0, o1, o2 = o0_ref[:, cols], tok_sc[0, s], tok_sc[2, s]
        l0, l1, l2 = l0_ref[:, cols], tok_sc[1, s], tok_sc[3, s]
        mx = jnp.maximum(jnp.maximum(l0, l1), l2)
        e0, e1, e2 = jnp.exp(l0 - mx), jnp.exp(l1 - mx), jnp.exp(l2 - mx)
        inv = 1.0 / (e0 + e1 + e2)
        mixed.append(((e0 * inv) * o0 + (e1 * inv) * o1 + (e2 * inv) * o2).astype(BF16))
    cat = jnp.concatenate(mixed + [_mem_attention(qm_ref[...], kvm_ref[...])], axis=-1)
    out = h_ref[...] + jnp.dot(cat, wout_ref[...], preferred_element_type=F32)
    h2_ref[...] = out
    hn = _rms(out, g_ref[...]).astype(BF16)
    hn2_ref[...] = hn

    logits = jnp.dot(hn, router_ref[...], preferred_element_type=F32)
    lane = lax.broadcasted_iota(jnp.int32, logits.shape, 1)
    masked = jnp.where(lane < N_EXPERTS, logits, -jnp.inf)
    v0 = jnp.max(masked, axis=-1, keepdims=True)
    i0 = jnp.min(jnp.where(masked == v0, lane, LANES), axis=-1, keepdims=True)
    rest = jnp.where(lane == i0, -jnp.inf, masked)
    v1 = jnp.max(rest, axis=-1, keepdims=True)
    i1 = jnp.min(jnp.where(rest == v1, lane, LANES), axis=-1, keepdims=True)
    ex = jnp.exp(v1 - v0)
    inv2 = 1.0 / (1.0 + ex)

    oh0, oh1 = lane == i0, lane == i1
    sel = (oh0 | oh1).astype(F32)
    tok_r = lax.broadcasted_iota(jnp.int32, (ROW_TILE, ROW_TILE), 0)
    tok_c = lax.broadcasted_iota(jnp.int32, (ROW_TILE, ROW_TILE), 1)
    earlier = (tok_c < tok_r).astype(F32).astype(BF16)
    before = jnp.dot(earlier, sel.astype(BF16), preferred_element_type=F32) + carry_sc[...]
    rank0 = jnp.sum(jnp.where(oh0, before, 0.0), axis=-1, keepdims=True)
    rank1 = jnp.sum(jnp.where(oh1, before, 0.0), axis=-1, keepdims=True)
    count = carry_sc[...] + jnp.sum(sel, axis=0, keepdims=True)
    carry_sc[...] = count
    cnt_ref[...] = jnp.broadcast_to(count, cnt_ref.shape)

    fields = (i0.astype(F32), i1.astype(F32), inv2, ex * inv2, rank0, rank1)
    meta = jnp.zeros(logits.shape, F32)
    for k, f in enumerate(fields):
        meta = jnp.where(lane == k, f, meta)
    meta_ref[...] = meta


META_E0, META_E1, META_G0, META_G1, META_RANK0, META_RANK1 = range(6)


def _dilout(os, ls, tok, kvm, wout, h, g_ffn, router):
    row = lambda i: (i, 0)
    const = lambda i: (0, 0)

    def group_spec(a):
        d = a.shape[0]
        if d == 1:
            return pl.BlockSpec((None, ROW_TILE, DIL_W), lambda i: (0, i, 0))
        return pl.BlockSpec((d, ROW_TILE // d, DIL_W), lambda i: (0, i, 0))

    return pl.pallas_call(
        _dilout_kernel,
        grid=(SEQ // ROW_TILE,),
        in_specs=[group_spec(a) for a in (*os, *ls)] + [
                  pl.BlockSpec((ROW_TILE, MEM_W), lambda i: (i, DIL_SLAB // MEM_W)),
                  pl.BlockSpec((N_MEM, 2 * MEM_W), lambda i: (0, 1)),
                  pl.BlockSpec((DIL_W + MEM_W, D_MODEL), const),
                  pl.BlockSpec((ROW_TILE, D_MODEL), row),
                  pl.BlockSpec((1, D_MODEL), const),
                  pl.BlockSpec((D_MODEL, LANES), const)],
        out_specs=[pl.BlockSpec((ROW_TILE, D_MODEL), row),
                   pl.BlockSpec((ROW_TILE, D_MODEL), row),
                   pl.BlockSpec((ROW_TILE, LANES), row),
                   pl.BlockSpec((SUBLANES, LANES), const)],
        out_shape=[jax.ShapeDtypeStruct((SEQ, D_MODEL), F32), jax.ShapeDtypeStruct((SEQ, D_MODEL), BF16),
                   jax.ShapeDtypeStruct((SEQ, LANES), F32), jax.ShapeDtypeStruct((SUBLANES, LANES), F32)],
        scratch_shapes=[pltpu.VMEM((1, LANES), F32),
                        pltpu.VMEM((4, DIL_W // LANES, ROW_TILE, LANES), F32)],
        compiler_params=pltpu.CompilerParams(dimension_semantics=("arbitrary",), vmem_limit_bytes=VMEM_LIMIT),
        name="dil_out",
    )(*os, *ls, tok, kvm, wout, h, g_ffn.reshape(1, D_MODEL), router)


def _swiglu_partial(x, wg, wu, wd):
    a = jnp.dot(x, wg, preferred_element_type=F32)
    b = jnp.dot(x, wu, preferred_element_type=F32)
    hm = (a * jax.nn.sigmoid(a) * b).astype(BF16)
    return jnp.dot(hm, wd, preferred_element_type=F32)


def _ffn_kernel(x_ref, wg_ref, wu_ref, wd_ref, h_ref, o_ref, acc_sc):
    j = pl.program_id(1)

    @pl.when(j == 0)
    def _():
        acc_sc[...] = jnp.zeros_like(acc_sc)

    acc_sc[...] += _swiglu_partial(x_ref[...], wg_ref[...], wu_ref[...], wd_ref[...])

    @pl.when(j == pl.num_programs(1) - 1)
    def _():
        o_ref[...] = h_ref[...] + acc_sc[...]


def _ffn(x, h, wg, wu, wd):
    tm, tf = FFN_ROW_TILE, FFN_COL_TILE
    return pl.pallas_call(
        _ffn_kernel,
        grid=(SEQ // tm, D_FF // tf),
        in_specs=[pl.BlockSpec((tm, D_MODEL), lambda i, j: (i, 0)),
                  pl.BlockSpec((D_MODEL, tf), lambda i, j: (0, j)),
                  pl.BlockSpec((D_MODEL, tf), lambda i, j: (0, j)),
                  pl.BlockSpec((tf, D_MODEL), lambda i, j: (j, 0)),
                  pl.BlockSpec((tm, D_MODEL), lambda i, j: (i, 0))],
        out_specs=pl.BlockSpec((tm, D_MODEL), lambda i, j: (i, 0)),
        out_shape=jax.ShapeDtypeStruct((SEQ, D_MODEL), F32),
        scratch_shapes=[pltpu.VMEM((tm, D_MODEL), F32)],
        compiler_params=pltpu.CompilerParams(dimension_semantics=("arbitrary", "arbitrary"),
                                             vmem_limit_bytes=VMEM_LIMIT),
        name="ffn",
    )(x, wg, wu, wd, h)


MOE_SUB = 512
MOE_ROWS = 2048
MOE_SUBS = MOE_ROWS // MOE_SUB
MOE_SLOTS = 2 * SEQ + N_EXPERTS * MOE_SUB
MOE_TILES = MOE_SLOTS // MOE_ROWS
MOE_ITEMS = MOE_TILES + N_EXPERTS - 1
PACK_W = D_MODEL // 2
ITEM_TILE, ITEM_EXPERT, ITEM_LO, ITEM_HI, ITEM_FIRST = range(5)
DMA_UNROLL = 8


def _pack_bf16_pairs(x):
    lo = lax.bitcast_convert_type(x[:, :PACK_W].astype(F32), jnp.uint32) >> 16
    hi = lax.bitcast_convert_type(x[:, PACK_W:].astype(F32), jnp.uint32) & jnp.uint32(0xFFFF0000)
    return hi | lo


def _unpack_bf16_pairs(pk):
    lo = lax.bitcast_convert_type(pk << 16, F32).astype(BF16)
    hi = lax.bitcast_convert_type(pk & jnp.uint32(0xFFFF0000), F32).astype(BF16)
    return lo, hi


def _row_copy(src, src_row, dst, dst_row, sem):
    return pltpu.make_async_copy(src.at[pl.ds(src_row, 1), :], dst.at[pl.ds(dst_row, 1), :], sem)


def _scatter_kernel(pos0_ref, pos1_ref, x_ref, xs_in_ref, xs_ref, pk_sc, sem):
    del xs_in_ref
    pk_sc[...] = _pack_bf16_pairs(x_ref[...])

    def issue(g, c):
        for u in range(DMA_UNROLL):
            t = g * DMA_UNROLL + u
            _row_copy(pk_sc, t, xs_ref, pos0_ref[t], sem).start(priority=0)
            _row_copy(pk_sc, t, xs_ref, pos1_ref[t], sem).start(priority=1)
        return c

    def drain(g, c):
        for _ in range(2 * DMA_UNROLL):
            _row_copy(pk_sc, 0, xs_ref, 0, sem).wait()
        return c

    lax.fori_loop(0, ROW_TILE // DMA_UNROLL, issue, 0)
    lax.fori_loop(0, ROW_TILE // DMA_UNROLL, drain, 0)


def _moe_scatter(x, pos0, pos1):
    smem_tile = pl.BlockSpec((ROW_TILE,), lambda i: (i,), memory_space=pltpu.SMEM)
    return pl.pallas_call(
        _scatter_kernel,
        grid=(SEQ // ROW_TILE,),
        in_specs=[smem_tile, smem_tile,
                  pl.BlockSpec((ROW_TILE, D_MODEL), lambda i: (i, 0)),
                  pl.BlockSpec(memory_space=pl.ANY)],
        out_specs=pl.BlockSpec(memory_space=pl.ANY),
        out_shape=jax.ShapeDtypeStruct((MOE_SLOTS, PACK_W), jnp.uint32),
        scratch_shapes=[pltpu.VMEM((ROW_TILE, PACK_W), jnp.uint32), pltpu.SemaphoreType.DMA],
        input_output_aliases={3: 0},
        compiler_params=pltpu.CompilerParams(dimension_semantics=("arbitrary",), vmem_limit_bytes=VMEM_LIMIT),
        name="moe_scatter",
    )(pos0, pos1, x, jnp.zeros((MOE_SLOTS, PACK_W), jnp.uint32))


def _moe_kernel(items_ref, xs_ref, wg_ref, wu_ref, wd_ref, y_ref, xb_sc):
    w = pl.program_id(0)
    j = pl.program_id(1)
    lo = items_ref[ITEM_LO, w]
    hi = items_ref[ITEM_HI, w]

    @pl.when((j == 0) & (hi > lo))
    def _():
        xlo, xhi = _unpack_bf16_pairs(xs_ref[...])
        xb_sc[:, :PACK_W] = xlo
        xb_sc[:, PACK_W:] = xhi

    @pl.when((j == 0) & (items_ref[ITEM_FIRST, w] == 1))
    def _():
        y_ref[...] = jnp.zeros_like(y_ref)

    for sb in range(MOE_SUBS):
        @pl.when((lo <= sb) & (sb < hi))
        def _(sb=sb):
            rows = slice(sb * MOE_SUB, (sb + 1) * MOE_SUB)
            y_ref[rows, :] += _swiglu_partial(xb_sc[rows, :], wg_ref[0], wu_ref[0], wd_ref[0])


def _moe_grouped(items, xs, wg, wu, wd):
    tf = FFN_COL_TILE
    nf = D_FF // tf
    jf = lambda w, j, it: jnp.where(it[ITEM_HI, w] > it[ITEM_LO, w], j, nf - 1)
    grid_spec = pltpu.PrefetchScalarGridSpec(
        num_scalar_prefetch=1,
        grid=(MOE_ITEMS, nf),
        in_specs=[pl.BlockSpec((MOE_ROWS, PACK_W), lambda w, j, it: (it[ITEM_TILE, w], 0)),
                  pl.BlockSpec((1, D_MODEL, tf), lambda w, j, it: (it[ITEM_EXPERT, w], 0, jf(w, j, it))),
                  pl.BlockSpec((1, D_MODEL, tf), lambda w, j, it: (it[ITEM_EXPERT, w], 0, jf(w, j, it))),
                  pl.BlockSpec((1, tf, D_MODEL), lambda w, j, it: (it[ITEM_EXPERT, w], jf(w, j, it), 0))],
        out_specs=pl.BlockSpec((MOE_ROWS, D_MODEL), lambda w, j, it: (it[ITEM_TILE, w], 0)),
        scratch_shapes=[pltpu.VMEM((MOE_ROWS, D_MODEL), BF16)],
    )
    return pl.pallas_call(
        _moe_kernel,
        grid_spec=grid_spec,
        out_shape=jax.ShapeDtypeStruct((MOE_SLOTS, D_MODEL), F32),
        compiler_params=pltpu.CompilerParams(dimension_semantics=("arbitrary", "arbitrary"),
                                             vmem_limit_bytes=VMEM_LIMIT),
        name="moe_grouped",
    )(items, xs, wg, wu, wd)


def _combine_kernel(pos0_ref, pos1_ref, meta_ref, h_ref, gf_ref, y_ref, o_ref, y0_sc, y1_sc, sem):
    def issue(g, c):
        for u in range(DMA_UNROLL):
            t = g * DMA_UNROLL + u
            _row_copy(y_ref, pos0_ref[t], y0_sc, t, sem).start(priority=0)
            _row_copy(y_ref, pos1_ref[t], y1_sc, t, sem).start(priority=1)
        return c

    def drain(g, c):
        for _ in range(DMA_UNROLL):
            _row_copy(y_ref, 0, y0_sc, 0, sem).wait()
            _row_copy(y_ref, 0, y1_sc, 0, sem).wait()
        return c

    lax.fori_loop(0, ROW_TILE // DMA_UNROLL, issue, 0)
    lax.fori_loop(0, ROW_TILE // DMA_UNROLL, drain, 0)
    meta = meta_ref[...]
    g0 = meta[:, META_G0:META_G0 + 1]
    g1 = meta[:, META_G1:META_G1 + 1]
    o_ref[...] = _rms(h_ref[...] + (g0 * y0_sc[...] + g1 * y1_sc[...]), gf_ref[...])


def _moe_combine(pos0, pos1, meta, h, g_final, y):
    smem_tile = pl.BlockSpec((ROW_TILE,), lambda i: (i,), memory_space=pltpu.SMEM)
    return pl.pallas_call(
        _combine_kernel,
        grid=(SEQ // ROW_TILE,),
        in_specs=[smem_tile, smem_tile,
                  pl.BlockSpec((ROW_TILE, LANES), lambda i: (i, 0)),
                  pl.BlockSpec((ROW_TILE, D_MODEL), lambda i: (i, 0)),
                  pl.BlockSpec((1, D_MODEL), lambda i: (0, 0)),
                  pl.BlockSpec(memory_space=pl.ANY)],
        out_specs=pl.BlockSpec((ROW_TILE, D_MODEL), lambda i: (i, 0)),
        out_shape=jax.ShapeDtypeStruct((SEQ, D_MODEL), F32),
        scratch_shapes=[pltpu.VMEM((ROW_TILE, D_MODEL), F32), pltpu.VMEM((ROW_TILE, D_MODEL), F32),
                        pltpu.SemaphoreType.DMA],
        compiler_params=pltpu.CompilerParams(dimension_semantics=("arbitrary",), vmem_limit_bytes=VMEM_LIMIT),
        name="moe_combine",
    )(pos0, pos1, meta, h, g_final.reshape(1, D_MODEL), y)


def _moe_plan(meta, counts):
    i32 = jnp.int32
    cnt = counts[0, :N_EXPERTS].astype(i32)
    padded = (cnt + MOE_SUB - 1) // MOE_SUB * MOE_SUB
    start = jnp.cumsum(padded) - padded
    e0, e1 = meta[:, META_E0].astype(i32), meta[:, META_E1].astype(i32)
    pos0 = start[e0] + meta[:, META_RANK0].astype(i32)
    pos1 = start[e1] + meta[:, META_RANK1].astype(i32)

    first_sb, end_sb = start // MOE_SUB, (start + padded) // MOE_SUB
    t0 = jnp.arange(MOE_TILES, dtype=i32)[:, None] * MOE_SUBS
    lo = jnp.clip(first_sb[None, :] - t0, 0, MOE_SUBS)
    hi = jnp.clip(end_sb[None, :] - t0, 0, MOE_SUBS)
    active = hi > lo
    none = ~jnp.any(active, axis=1, keepdims=True)
    active = jnp.concatenate([active, none], axis=1)
    lo = jnp.concatenate([lo, jnp.zeros_like(t0)], axis=1)
    hi = jnp.concatenate([hi, jnp.zeros_like(t0)], axis=1)
    first = active & (jnp.cumsum(active.astype(i32), axis=1) == 1)
    ncol = N_EXPERTS + 1
    flat = jnp.arange(MOE_TILES * ncol, dtype=i32)
    key = jnp.where(active.reshape(-1), flat, MOE_TILES * ncol)
    order = jnp.argsort(key)[:MOE_ITEMS]
    live = key[order] < MOE_TILES * ncol
    tile = jnp.where(live, order // ncol, MOE_TILES - 1)
    lo_i = jnp.where(live, lo.reshape(-1)[order], 0)
    hi_i = jnp.where(live, hi.reshape(-1)[order], 0)
    first_i = jnp.where(live, first.reshape(-1)[order], False).astype(i32)
    idx = jnp.arange(MOE_ITEMS, dtype=i32)
    src = jnp.maximum(lax.cummax(jnp.where(hi_i > lo_i, idx, -1), axis=0), 0)
    expert = jnp.minimum(order % ncol, N_EXPERTS - 1)[src]
    items = jnp.stack([tile, expert, lo_i, hi_i, first_i]).astype(i32)
    return pos0, pos1, items


def kernel(x, mem, rel_bias_table, mem_norm, norm_mix, norm_ffn, final_norm, swa_w_in, swa_sinks, swa_w_mem_kv,
           swa_w_out, dil_w_in, dil_w_mem_kv, dil_w_out, ffn_gate, ffn_up, ffn_down, router, moe_gate, moe_up,
           moe_down):
    assert x.shape == (1, SEQ, D_MODEL) and mem.shape == (1, N_MEM, D_MODEL)
    assert norm_mix.shape == (2, D_MODEL) and swa_w_in.shape == (1, D_MODEL, SWA_IN)
    assert dil_w_in.shape == (1, D_MODEL, DIL_IN) and moe_gate.shape == (1, N_EXPERTS, D_MODEL, D_FF)
    bf = lambda a: a.astype(BF16)
    h0 = x.reshape(SEQ, D_MODEL)

    kvm = _memkv(mem[0], mem_norm, bf(jnp.concatenate([swa_w_mem_kv[0], dil_w_mem_kv[0]], axis=1)))

    p0 = _proj(h0, norm_mix[0], bf(swa_w_in[0]), "swa_proj")
    h1, hn1 = _swa_layer(p0, kvm, rel_bias_table, swa_sinks[0], bf(swa_w_out[0]), h0, norm_ffn[0])
    h2 = _ffn(hn1, h1, bf(ffn_gate[0]), bf(ffn_up[0]), bf(ffn_down[0]))

    w1 = dil_w_in[0]
    n_grp = len(DIL_GROUPS) * DIL_SLAB
    w1 = jnp.concatenate([w1[:, :DIL_SLAB], w1[:, n_grp:], w1[:, DIL_SLAB:n_grp]], axis=1)
    tok, g1, g2 = _dilproj(h2, norm_mix[1], bf(w1))
    groups = [_dil_group(qkv, rel_bias_table, gi) for gi, qkv in enumerate((tok[None], g1, g2))]
    router_p = jnp.pad(bf(router[0]), ((0, 0), (0, LANES - N_EXPERTS)))
    h3, hn3, meta, counts = _dilout([g[0] for g in groups], [g[1] for g in groups], tok, kvm, bf(dil_w_out[0]), h2,
                                    norm_ffn[1], router_p)
    pos0, pos1, items = _moe_plan(meta, counts)
    xs = _moe_scatter(hn3, pos0, pos1)
    y = _moe_grouped(items, xs, bf(moe_gate[0]), bf(moe_up[0]), bf(moe_down[0]))
    out = _moe_combine(pos0, pos1, meta, h3, final_norm, y)
    return out.reshape(1, SEQ, D_MODEL)
```

```python
import functools
import math

import jax
import jax.numpy as jnp
import numpy as np
from jax import lax
from jax.experimental import pallas as pl
from jax.experimental.pallas import tpu as pltpu

F32 = jnp.float32
BF16 = jnp.bfloat16

D_MODEL = 1024
SEQ = 16384
HEAD_DIM = 64
N_MIX_HEADS = 12
SWA_KV_HEADS = 3
SWA_GROUP = N_MIX_HEADS // SWA_KV_HEADS
SWA_WINDOW = 128
DIL_GROUPS = ((128, 1), (512, 4), (2048, 16))
DIL_HEADS = 4
N_MEM = 256
MEM_HEADS = 4
BLOCK = 128
N_BUCKETS = 32
MAX_DISTANCE = 2048
D_FF = 3584
N_EXPERTS = 8
EPS = 1e-5
NEG = -1e30
Q_SCALE = HEAD_DIM ** -0.5

SWA_Q = N_MIX_HEADS * HEAD_DIM
SWA_KV = SWA_KV_HEADS * HEAD_DIM
SWA_IN = SWA_Q + 2 * SWA_KV + MEM_HEADS * HEAD_DIM
DIL_W = DIL_HEADS * HEAD_DIM
DIL_IN = len(DIL_GROUPS) * 3 * DIL_W + MEM_HEADS * HEAD_DIM
MEM_W = MEM_HEADS * HEAD_DIM

LANES = 128
SUBLANES = 8
VMEM_LIMIT = 56 * 1024 * 1024

ROW_TILE = 512
BLOCKS_PER_TILE = ROW_TILE // BLOCK
FFN_ROW_TILE = 1024
FFN_COL_TILE = 512


def _bucket_map(dil):
    qi = np.arange(BLOCK)[:, None]
    kj = np.arange(2 * BLOCK)[None, :]
    d = np.maximum((qi + BLOCK - kj) * dil, 0)
    max_exact = N_BUCKETS // 2
    ratio = np.maximum(d, 1).astype(np.float32) / np.float32(max_exact)
    large = max_exact + (np.log(ratio) / np.float32(math.log(MAX_DISTANCE / max_exact))
                         * np.float32(N_BUCKETS - max_exact)).astype(np.int32)
    return np.where(d < max_exact, d, np.minimum(large, N_BUCKETS - 1)).astype(np.int32)


def _rms(x, g):
    ms = jnp.mean(x * x, axis=-1, keepdims=True)
    return x * lax.rsqrt(ms + EPS) * g


def _dot_nt(a, b):
    return lax.dot_general(a, b, (((1,), (1,)), ((), ())), preferred_element_type=F32)


def _dot_tn(a, b):
    return lax.dot_general(a, b, (((0,), (0,)), ((), ())), preferred_element_type=F32)


def _fill_bias(bias_sc, tab_ref, bucket, heads):
    for j, h in enumerate(heads):
        def body(k, b, h=h):
            return jnp.where(bucket == k, tab_ref[k, h], b)
        bias_sc[:, j * BLOCK:(j + 1) * BLOCK] = lax.fori_loop(0, N_BUCKETS, body, jnp.zeros(bucket.shape, F32))


def _fill_row(row_sc, values):
    blk = lax.broadcasted_iota(jnp.int32, row_sc.shape, 1) // BLOCK
    row = jnp.zeros(row_sc.shape, F32)
    for j, v in enumerate(values):
        row = jnp.where(blk == j, v, row)
    row_sc[...] = row


def _folded_bucket_map(dil):
    full = _bucket_map(dil)
    qi = np.arange(BLOCK)[:, None]
    c = np.arange(BLOCK)[None, :]
    return np.ascontiguousarray(np.where(c <= qi, full[:, BLOCK:], full[:, :BLOCK]).T.astype(np.int32))


def _diag_bucket(dil):
    return int(_bucket_map(dil)[0, 0])


def _fold_masks(n):
    c = lax.broadcasted_iota(jnp.int32, (BLOCK, n), 0)
    qi = lax.broadcasted_iota(jnp.int32, (BLOCK, n), 1) % BLOCK
    return c <= qi, c == qi


def _pair_scores(k_slab, q_rows):
    lane = lax.broadcasted_iota(jnp.int32, q_rows.shape, 1)
    zero = jnp.zeros_like(q_rows)
    qa = jnp.where(lane < HEAD_DIM, q_rows, zero)
    qb = jnp.where(lane < HEAD_DIM, zero, q_rows)
    return _dot_nt(k_slab, qa), _dot_nt(k_slab, qb)


def _pair_values(v_slab, p_a, p_b):
    oa = _dot_tn(v_slab, p_a)
    ob = _dot_tn(v_slab, p_b)
    row = lax.broadcasted_iota(jnp.int32, oa.shape, 0)
    return jnp.where(row < HEAD_DIM, oa, ob)


def _band_softmax(st, bias, own, eye, has_prev, sink, diag_bias):
    s_prev, s_own = st[:BLOCK], st[BLOCK:]
    t = jnp.where(own | has_prev, jnp.where(own, s_own, s_prev) + bias, NEG)
    m = jnp.max(t, axis=0, keepdims=True)
    if diag_bias is not None:
        s_d = jnp.sum(jnp.where(eye, s_prev, 0.0), axis=0, keepdims=True) + diag_bias
        s_d = jnp.where(has_prev, s_d, NEG)
        m = jnp.maximum(m, s_d)
    if sink is not None:
        m = jnp.maximum(m, sink)
    e = jnp.exp(t - m)
    den = jnp.sum(e, axis=0, keepdims=True)
    if diag_bias is not None:
        e_d = jnp.exp(s_d - m)
        den = den + e_d
    if sink is not None:
        den = den + jnp.exp(sink - m)
    inv = 1.0 / den
    p = e * inv
    p_own = jnp.where(own, p, 0.0)
    p_prev = jnp.where(own, 0.0, p)
    if diag_bias is not None:
        p_prev = jnp.where(eye, e_d * inv, p_prev)
    return jnp.concatenate([p_prev, p_own], axis=0).astype(BF16), m + jnp.log(den)


def _mem_attention(qm, kvm):
    outs = []
    for s in range(MEM_W // LANES):
        sa, sb = _pair_scores(kvm[:, s * LANES:(s + 1) * LANES], qm[:, s * LANES:(s + 1) * LANES])
        ps = []
        for st in (sa, sb):
            e = jnp.exp(st - jnp.max(st, axis=0, keepdims=True))
            ps.append((e * (1.0 / jnp.sum(e, axis=0, keepdims=True))).astype(BF16))
        outs.append(_pair_values(kvm[:, MEM_W + s * LANES:MEM_W + (s + 1) * LANES], ps[0], ps[1]))
    return outs


def _to_rows(slabs_t):
    return jnp.concatenate([jnp.transpose(x).astype(BF16) for x in slabs_t], axis=-1)


def _memkv_kernel(mem_ref, g_ref, w_ref, o_ref):
    mn = _rms(mem_ref[...], g_ref[...]).astype(BF16)
    o_ref[...] = jnp.dot(mn, w_ref[...], preferred_element_type=F32).astype(BF16)


def _memkv(mem, g, w):
    return pl.pallas_call(
        _memkv_kernel,
        out_shape=jax.ShapeDtypeStruct((N_MEM, w.shape[1]), BF16),
        name="mem_kv",
    )(mem, g.reshape(1, D_MODEL), w)


def _proj_kernel(h_ref, g_ref, w_ref, o_ref):
    hn = _rms(h_ref[...], g_ref[...]).astype(BF16)
    o_ref[...] = jnp.dot(hn, w_ref[...], preferred_element_type=F32).astype(BF16)


def _proj(h, g, w, name):
    n = w.shape[1]
    return pl.pallas_call(
        _proj_kernel,
        grid=(SEQ // ROW_TILE,),
        in_specs=[pl.BlockSpec((ROW_TILE, D_MODEL), lambda i: (i, 0)),
                  pl.BlockSpec((1, D_MODEL), lambda i: (0, 0)),
                  pl.BlockSpec((D_MODEL, n), lambda i: (0, 0))],
        out_specs=pl.BlockSpec((ROW_TILE, n), lambda i: (i, 0)),
        out_shape=jax.ShapeDtypeStruct((SEQ, n), BF16),
        compiler_params=pltpu.CompilerParams(dimension_semantics=("arbitrary",), vmem_limit_bytes=VMEM_LIMIT),
        name=name,
    )(h, g.reshape(1, D_MODEL), w)


SWA_PAIR_SLABS = [(g, SWA_GROUP + g) for g in range(SWA_GROUP)] + [(8, 9), (10, 11)]
SWA_K0 = SWA_Q
SWA_V0 = SWA_K0 + 2 * LANES
SWA_M0 = SWA_V0 + 2 * LANES
SWA_PW = SWA_M0 + MEM_W
SWA_SCORE_HEADS = [0, 1, 2, 3, 4, 5, 6, 7, 8, 10, 9, 11]
SWA_NQ = N_MIX_HEADS * BLOCK


def _swa_kernel(tab_ref, sink_ref, bucket_ref, p_ref, kprev_ref, vprev_ref, kvm_ref, wout_ref, h_ref, g_ref,
                h1_ref, hn1_ref, bias_sc, sink_sc, k_sc, v_sc, cat_sc):
    i = pl.program_id(0)

    @pl.when(i == 0)
    def _():
        _fill_bias(bias_sc, tab_ref, bucket_ref[...], SWA_SCORE_HEADS)
        _fill_row(sink_sc, [sink_ref[h] for h in SWA_SCORE_HEADS])

    k_sc[0:BLOCK, :] = kprev_ref[...]
    k_sc[BLOCK:, :] = p_ref[:, SWA_K0:SWA_V0]
    v_sc[0:BLOCK, :] = vprev_ref[...]
    v_sc[BLOCK:, :] = p_ref[:, SWA_V0:SWA_M0]
    own, eye = _fold_masks(SWA_NQ)

    def block_body(b, carry):
        r0 = pl.multiple_of(b * BLOCK, BLOCK)
        has_prev = i * BLOCKS_PER_TILE + b > 0
        qb = p_ref[pl.ds(r0, BLOCK), 0:SWA_Q]
        slab = lambda x, s: x[:, s * LANES:(s + 1) * LANES]
        qa = jnp.concatenate([slab(qb, s) for s in range(4)], axis=0)
        qc = jnp.concatenate([slab(qb, s) for s in (4, 5)], axis=0)
        kb = k_sc[pl.ds(r0, 2 * BLOCK), :]
        vb = v_sc[pl.ds(r0, 2 * BLOCK), :]
        s0, s1 = _pair_scores(slab(kb, 0), qa)
        s2, s3 = _pair_scores(slab(kb, 1), qc)
        st = jnp.concatenate([s0, s1, s2, s3], axis=1)
        pt, _ = _band_softmax(st, bias_sc[...], own, eye, has_prev, sink_sc[...], None)
        na, nc = 4 * BLOCK, 2 * BLOCK
        oa = _pair_values(slab(vb, 0), pt[:, 0:na], pt[:, na:2 * na])
        oc = _pair_values(slab(vb, 1), pt[:, 2 * na:2 * na + nc], pt[:, 2 * na + nc:])
        outs = [slab(oa, s) for s in range(4)] + [slab(oc, s) for s in range(2)]
        outs += _mem_attention(p_ref[pl.ds(r0, BLOCK), SWA_M0:], kvm_ref[...])
        cat_sc[pl.ds(r0, BLOCK), :] = _to_rows(outs)
        return carry

    lax.fori_loop(0, BLOCKS_PER_TILE, block_body, 0)

    out = h_ref[...] + jnp.dot(cat_sc[...], wout_ref[...], preferred_element_type=F32)
    h1_ref[...] = out
    hn1_ref[...] = _rms(out, g_ref[...]).astype(BF16)


def _swa_layer(p, kvm, table, sinks, wout, h, g_ffn):
    assert SWA_WINDOW == BLOCK
    kv_w = 2 * LANES
    prev = lambda i: jnp.maximum(i * BLOCKS_PER_TILE - 1, 0)
    return pl.pallas_call(
        _swa_kernel,
        grid=(SEQ // ROW_TILE,),
        in_specs=[
            pl.BlockSpec(memory_space=pltpu.SMEM),
            pl.BlockSpec(memory_space=pltpu.SMEM),
            pl.BlockSpec((BLOCK, BLOCK), lambda i: (0, 0)),
            pl.BlockSpec((ROW_TILE, SWA_PW), lambda i: (i, 0)),
            pl.BlockSpec((BLOCK, kv_w), lambda i: (prev(i), SWA_K0 // kv_w)),
            pl.BlockSpec((BLOCK, kv_w), lambda i: (prev(i), SWA_V0 // kv_w)),
            pl.BlockSpec((N_MEM, 2 * MEM_W), lambda i: (0, 0)),
            pl.BlockSpec((D_MODEL, D_MODEL), lambda i: (0, 0)),
            pl.BlockSpec((ROW_TILE, D_MODEL), lambda i: (i, 0)),
            pl.BlockSpec((1, D_MODEL), lambda i: (0, 0)),
        ],
        out_specs=[pl.BlockSpec((ROW_TILE, D_MODEL), lambda i: (i, 0)),
                   pl.BlockSpec((ROW_TILE, D_MODEL), lambda i: (i, 0))],
        out_shape=[jax.ShapeDtypeStruct((SEQ, D_MODEL), F32), jax.ShapeDtypeStruct((SEQ, D_MODEL), BF16)],
        scratch_shapes=[pltpu.VMEM((BLOCK, SWA_NQ), F32),
                        pltpu.VMEM((1, SWA_NQ), F32),
                        pltpu.VMEM((ROW_TILE + BLOCK, kv_w), BF16),
                        pltpu.VMEM((ROW_TILE + BLOCK, kv_w), BF16),
                        pltpu.VMEM((ROW_TILE, D_MODEL), BF16)],
        compiler_params=pltpu.CompilerParams(dimension_semantics=("arbitrary",), vmem_limit_bytes=VMEM_LIMIT),
        name="swa_mixer",
    )(table, sinks, jnp.asarray(_folded_bucket_map(1)), p, p, p, kvm, wout, h, g_ffn.reshape(1, D_MODEL))


DIL_SLAB = 3 * DIL_W
TOK_W = DIL_SLAB + MEM_W
SLABS_PER_GROUP = DIL_SLAB // LANES


def _dilproj_kernel(h_ref, g_ref, w_ref, tok_ref, g1_ref, g2_ref, slab_sc):
    hn = _rms(h_ref[...], g_ref[...]).astype(BF16)
    res = jnp.dot(hn, w_ref[...], preferred_element_type=F32)
    tok_ref[...] = res[:, :TOK_W].astype(BF16)
    for s in range(2 * SLABS_PER_GROUP):
        slab_sc[s] = res[:, TOK_W + s * LANES:TOK_W + (s + 1) * LANES]
    for gi, out_ref in ((1, g1_ref), (2, g2_ref)):
        d = DIL_GROUPS[gi][1]
        for s in range(SLABS_PER_GROUP):
            for r in range(d):
                rows = slab_sc[(gi - 1) * SLABS_PER_GROUP + s, pl.ds(r, ROW_TILE // d, stride=d), :]
                out_ref[r, :, s * LANES:(s + 1) * LANES] = rows.astype(BF16)


def _dilproj(h, g, w):
    d1, d2 = DIL_GROUPS[1][1], DIL_GROUPS[2][1]
    return pl.pallas_call(
        _dilproj_kernel,
        grid=(SEQ // ROW_TILE,),
        in_specs=[pl.BlockSpec((ROW_TILE, D_MODEL), lambda i: (i, 0)),
                  pl.BlockSpec((1, D_MODEL), lambda i: (0, 0)),
                  pl.BlockSpec((D_MODEL, DIL_IN), lambda i: (0, 0))],
        out_specs=[pl.BlockSpec((ROW_TILE, TOK_W), lambda i: (i, 0)),
                   pl.BlockSpec((d1, ROW_TILE // d1, DIL_SLAB), lambda i: (0, i, 0)),
                   pl.BlockSpec((d2, ROW_TILE // d2, DIL_SLAB), lambda i: (0, i, 0))],
        out_shape=[jax.ShapeDtypeStruct((SEQ, TOK_W), BF16),
                   jax.ShapeDtypeStruct((d1, SEQ // d1, DIL_SLAB), BF16),
                   jax.ShapeDtypeStruct((d2, SEQ // d2, DIL_SLAB), BF16)],
        scratch_shapes=[pltpu.VMEM((2 * SLABS_PER_GROUP, ROW_TILE, LANES), F32)],
        compiler_params=pltpu.CompilerParams(dimension_semantics=("arbitrary",), vmem_limit_bytes=VMEM_LIMIT),
        name="dil_proj",
    )(h, g.reshape(1, D_MODEL), w)


def _dil_kernel(tab_ref, bucket_ref, q_ref, k_ref, v_ref, kp_ref, vp_ref, o_ref, l_ref,
                bias_sc, diag_sc, k_sc, v_sc, *, head0, diag_bucket):
    r = pl.program_id(0)
    n = pl.program_id(1)
    heads = [head0 + h for h in range(DIL_HEADS)]
    nq = DIL_HEADS * BLOCK

    @pl.when((r == 0) & (n == 0))
    def _():
        _fill_bias(bias_sc, tab_ref, bucket_ref[...], heads)
        _fill_row(diag_sc, [tab_ref[diag_bucket, h] for h in heads])

    k_sc[0:BLOCK, :] = kp_ref[...]
    k_sc[BLOCK:, :] = k_ref[...]
    v_sc[0:BLOCK, :] = vp_ref[...]
    v_sc[BLOCK:, :] = v_ref[...]
    own, eye = _fold_masks(nq)
    upper_rows = lax.broadcasted_iota(jnp.int32, (BLOCK, BLOCK), 0) < HEAD_DIM

    def block_body(b, carry):
        r0 = pl.multiple_of(b * BLOCK, BLOCK)
        has_prev = n * BLOCKS_PER_TILE + b > 0
        qb = q_ref[pl.ds(r0, BLOCK), :]
        kb = k_sc[pl.ds(r0, 2 * BLOCK), :]
        vb = v_sc[pl.ds(r0, 2 * BLOCK), :]
        slab = lambda x, s: x[:, s * LANES:(s + 1) * LANES]
        scores = []
        for s in range(DIL_W // LANES):
            scores += _pair_scores(slab(kb, s), slab(qb, s))
        pt, lse = _band_softmax(jnp.concatenate(scores, axis=1), bias_sc[...], own, eye, has_prev, None,
                                diag_sc[...])
        outs, lses = [], []
        for s in range(DIL_W // LANES):
            ca, cb = 2 * s * BLOCK, (2 * s + 1) * BLOCK
            outs.append(jnp.transpose(_pair_values(slab(vb, s), pt[:, ca:ca + BLOCK], pt[:, cb:cb + BLOCK])))
            lse_t = jnp.where(upper_rows, jnp.broadcast_to(lse[:, ca:ca + BLOCK], (BLOCK, BLOCK)),
                              jnp.broadcast_to(lse[:, cb:cb + BLOCK], (BLOCK, BLOCK)))
            lses.append(jnp.transpose(lse_t))
        o_ref[pl.ds(r0, BLOCK), :] = jnp.concatenate(outs, axis=-1)
        l_ref[pl.ds(r0, BLOCK), :] = jnp.concatenate(lses, axis=-1)
        return carry

    lax.fori_loop(0, BLOCKS_PER_TILE, block_body, 0)


def _dil_group(qkv, table, gi):
    window, d = DIL_GROUPS[gi]
    rows = SEQ // d
    assert qkv.shape[:2] == (d, rows) and window // d == BLOCK
    prev = lambda n: jnp.maximum(n * BLOCKS_PER_TILE - 1, 0)
    tile = lambda c: pl.BlockSpec((None, ROW_TILE, DIL_W), lambda r, n: (r, n, c))
    prev_block = lambda c: pl.BlockSpec((None, BLOCK, DIL_W), lambda r, n: (r, prev(n), c))
    return pl.pallas_call(
        functools.partial(_dil_kernel, head0=gi * DIL_HEADS, diag_bucket=_diag_bucket(d)),
        grid=(d, rows // ROW_TILE),
        in_specs=[
            pl.BlockSpec(memory_space=pltpu.SMEM),
            pl.BlockSpec((BLOCK, BLOCK), lambda r, n: (0, 0)),
            tile(0), tile(1), tile(2), prev_block(1), prev_block(2),
        ],
        out_specs=[tile(0), tile(0)],
        out_shape=[jax.ShapeDtypeStruct((d, rows, DIL_W), F32), jax.ShapeDtypeStruct((d, rows, DIL_W), F32)],
        scratch_shapes=[pltpu.VMEM((BLOCK, DIL_HEADS * BLOCK), F32),
                        pltpu.VMEM((1, DIL_HEADS * BLOCK), F32),
                        pltpu.VMEM((ROW_TILE + BLOCK, DIL_W), BF16),
                        pltpu.VMEM((ROW_TILE + BLOCK, DIL_W), BF16)],
        compiler_params=pltpu.CompilerParams(dimension_semantics=("arbitrary", "arbitrary"),
                                             vmem_limit_bytes=VMEM_LIMIT),
        name=f"dil_attn_{gi}",
    )(table, jnp.asarray(_folded_bucket_map(d)), qkv, qkv, qkv, qkv, qkv)


def _dilout_kernel(o0_ref, o1_ref, o2_ref, l0_ref, l1_ref, l2_ref, qm_ref, kvm_ref, wout_ref, h_ref, g_ref,
                   router_ref, h2_ref, hn2_ref, meta_ref, cnt_ref, carry_sc, tok_sc):
    @pl.when(pl.program_id(0) == 0)
    def _():
        carry_sc[...] = jnp.zeros_like(carry_sc)

    for k, src_ref in enumerate((o1_ref, l1_ref, o2_ref, l2_ref)):
        d = src_ref.shape[0]
        for s in range(DIL_W // LANES):
            for r in range(d):
                tok_sc[k, s, pl.ds(r, ROW_TILE // d, stride=d), :] = src_ref[r, :, s * LANES:(s + 1) * LANES]

    mixed = []
    for s in range(DIL_W // LANES):
        cols = slice(s * LANES, (s + 1) * LANES)
        o0, o1, o2 = o0_ref[:, cols], tok_sc[0, s], tok_sc[2, s]
        l0, l1, l2 = l0_ref[:, cols], tok_sc[1, s], tok_sc[3, s]
        mx = jnp.maximum(jnp.maximum(l0, l1), l2)
        e0, e1, e2 = jnp.exp(l0 - mx), jnp.exp(l1 - mx), jnp.exp(l2 - mx)
        inv = 1.0 / (e0 + e1 + e2)
        mixed.append(((e0 * inv) * o0 + (e1 * inv) * o1 + (e2 * inv) * o2).astype(BF16))
    cat = jnp.concatenate(mixed + [_to_rows(_mem_attention(qm_ref[...], kvm_ref[...]))], axis=-1)
    out = h_ref[...] + jnp.dot(cat, wout_ref[...], preferred_element_type=F32)
    h2_ref[...] = out
    hn = _rms(out, g_ref[...]).astype(BF16)
    hn2_ref[...] = hn

    logits = jnp.dot(hn, router_ref[...], preferred_element_type=F32)
    lane = lax.broadcasted_iota(jnp.int32, logits.shape, 1)
    masked = jnp.where(lane < N_EXPERTS, logits, -jnp.inf)
    v0 = jnp.max(masked, axis=-1, keepdims=True)
    i0 = jnp.min(jnp.where(masked == v0, lane, LANES), axis=-1, keepdims=True)
    rest = jnp.where(lane == i0, -jnp.inf, masked)
    v1 = jnp.max(rest, axis=-1, keepdims=True)
    i1 = jnp.min(jnp.where(rest == v1, lane, LANES), axis=-1, keepdims=True)
    ex = jnp.exp(v1 - v0)
    inv2 = 1.0 / (1.0 + ex)

    oh0, oh1 = lane == i0, lane == i1
    sel = (oh0 | oh1).astype(F32)
    tok_r = lax.broadcasted_iota(jnp.int32, (ROW_TILE, ROW_TILE), 0)
    tok_c = lax.broadcasted_iota(jnp.int32, (ROW_TILE, ROW_TILE), 1)
    earlier = (tok_c < tok_r).astype(F32).astype(BF16)
    before = jnp.dot(earlier, sel.astype(BF16), preferred_element_type=F32) + carry_sc[...]
    rank0 = jnp.sum(jnp.where(oh0, before, 0.0), axis=-1, keepdims=True)
    rank1 = jnp.sum(jnp.where(oh1, before, 0.0), axis=-1, keepdims=True)
    count = carry_sc[...] + jnp.sum(sel, axis=0, keepdims=True)
    carry_sc[...] = count
    cnt_ref[...] = jnp.broadcast_to(count, cnt_ref.shape)

    fields = (i0.astype(F32), i1.astype(F32), inv2, ex * inv2, rank0, rank1)
    meta = jnp.zeros(logits.shape, F32)
    for k, f in enumerate(fields):
        meta = jnp.where(lane == k, f, meta)
    meta_ref[...] = meta


META_E0, META_E1, META_G0, META_G1, META_RANK0, META_RANK1 = range(6)


def _dilout(os, ls, tok, kvm, wout, h, g_ffn, router):
    row = lambda i: (i, 0)
    const = lambda i: (0, 0)

    def group_spec(a):
        d = a.shape[0]
        if d == 1:
            return pl.BlockSpec((None, ROW_TILE, DIL_W), lambda i: (0, i, 0))
        return pl.BlockSpec((d, ROW_TILE // d, DIL_W), lambda i: (0, i, 0))

    return pl.pallas_call(
        _dilout_kernel,
        grid=(SEQ // ROW_TILE,),
        in_specs=[group_spec(a) for a in (*os, *ls)] + [
                  pl.BlockSpec((ROW_TILE, MEM_W), lambda i: (i, DIL_SLAB // MEM_W)),
                  pl.BlockSpec((N_MEM, 2 * MEM_W), lambda i: (0, 1)),
                  pl.BlockSpec((DIL_W + MEM_W, D_MODEL), const),
                  pl.BlockSpec((ROW_TILE, D_MODEL), row),
                  pl.BlockSpec((1, D_MODEL), const),
                  pl.BlockSpec((D_MODEL, LANES), const)],
        out_specs=[pl.BlockSpec((ROW_TILE, D_MODEL), row),
                   pl.BlockSpec((ROW_TILE, D_MODEL), row),
                   pl.BlockSpec((ROW_TILE, LANES), row),
                   pl.BlockSpec((SUBLANES, LANES), const)],
        out_shape=[jax.ShapeDtypeStruct((SEQ, D_MODEL), F32), jax.ShapeDtypeStruct((SEQ, D_MODEL), BF16),
                   jax.ShapeDtypeStruct((SEQ, LANES), F32), jax.ShapeDtypeStruct((SUBLANES, LANES), F32)],
        scratch_shapes=[pltpu.VMEM((1, LANES), F32),
                        pltpu.VMEM((4, DIL_W // LANES, ROW_TILE, LANES), F32)],
        compiler_params=pltpu.CompilerParams(dimension_semantics=("arbitrary",), vmem_limit_bytes=VMEM_LIMIT),
        name="dil_out",
    )(*os, *ls, tok, kvm, wout, h, g_ffn.reshape(1, D_MODEL), router)


def _swiglu_partial(x, wg, wu, wd):
    a = jnp.dot(x, wg, preferred_element_type=F32)
    b = jnp.dot(x, wu, preferred_element_type=F32)
    hm = (a * jax.nn.sigmoid(a) * b).astype(BF16)
    return jnp.dot(hm, wd, preferred_element_type=F32)


def _ffn_kernel(x_ref, wg_ref, wu_ref, wd_ref, h_ref, o_ref, acc_sc):
    j = pl.program_id(1)

    @pl.when(j == 0)
    def _():
        acc_sc[...] = jnp.zeros_like(acc_sc)

    acc_sc[...] += _swiglu_partial(x_ref[...], wg_ref[...], wu_ref[...], wd_ref[...])

    @pl.when(j == pl.num_programs(1) - 1)
    def _():
        o_ref[...] = h_ref[...] + acc_sc[...]


def _ffn(x, h, wg, wu, wd):
    tm, tf = FFN_ROW_TILE, FFN_COL_TILE
    return pl.pallas_call(
        _ffn_kernel,
        grid=(SEQ // tm, D_FF // tf),
        in_specs=[pl.BlockSpec((tm, D_MODEL), lambda i, j: (i, 0)),
                  pl.BlockSpec((D_MODEL, tf), lambda i, j: (0, j)),
                  pl.BlockSpec((D_MODEL, tf), lambda i, j: (0, j)),
                  pl.BlockSpec((tf, D_MODEL), lambda i, j: (j, 0)),
                  pl.BlockSpec((tm, D_MODEL), lambda i, j: (i, 0))],
        out_specs=pl.BlockSpec((tm, D_MODEL), lambda i, j: (i, 0)),
        out_shape=jax.ShapeDtypeStruct((SEQ, D_MODEL), F32),
        scratch_shapes=[pltpu.VMEM((tm, D_MODEL), F32)],
        compiler_params=pltpu.CompilerParams(dimension_semantics=("arbitrary", "arbitrary"),
                                             vmem_limit_bytes=VMEM_LIMIT),
        name="ffn",
    )(x, wg, wu, wd, h)


MOE_SUB = 512
MOE_ROWS = 2048
MOE_SUBS = MOE_ROWS // MOE_SUB
MOE_SLOTS = 2 * SEQ + N_EXPERTS * MOE_SUB
MOE_TILES = MOE_SLOTS // MOE_ROWS
MOE_ITEMS = MOE_TILES + N_EXPERTS - 1
PACK_W = D_MODEL // 2
ITEM_TILE, ITEM_EXPERT, ITEM_LO, ITEM_HI, ITEM_FIRST = range(5)
DMA_UNROLL = 8


def _pack_bf16_pairs(x):
    lo = lax.bitcast_convert_type(x[:, :PACK_W].astype(F32), jnp.uint32) >> 16
    hi = lax.bitcast_convert_type(x[:, PACK_W:].astype(F32), jnp.uint32) & jnp.uint32(0xFFFF0000)
    return hi | lo


def _unpack_bf16_pairs(pk):
    lo = lax.bitcast_convert_type(pk << 16, F32).astype(BF16)
    hi = lax.bitcast_convert_type(pk & jnp.uint32(0xFFFF0000), F32).astype(BF16)
    return lo, hi


def _row_copy(src, src_row, dst, dst_row, sem):
    return pltpu.make_async_copy(src.at[pl.ds(src_row, 1), :], dst.at[pl.ds(dst_row, 1), :], sem)


def _scatter_kernel(pos0_ref, pos1_ref, x_ref, xs_in_ref, xs_ref, pk_sc, sem):
    del xs_in_ref
    pk_sc[...] = _pack_bf16_pairs(x_ref[...])

    def issue(g, c):
        for u in range(DMA_UNROLL):
            t = g * DMA_UNROLL + u
            _row_copy(pk_sc, t, xs_ref, pos0_ref[t], sem).start(priority=0)
            _row_copy(pk_sc, t, xs_ref, pos1_ref[t], sem).start(priority=1)
        return c

    def drain(g, c):
        for _ in range(2 * DMA_UNROLL):
            _row_copy(pk_sc, 0, xs_ref, 0, sem).wait()
        return c

    lax.fori_loop(0, ROW_TILE // DMA_UNROLL, issue, 0)
    lax.fori_loop(0, ROW_TILE // DMA_UNROLL, drain, 0)


def _moe_scatter(x, pos0, pos1):
    smem_tile = pl.BlockSpec((ROW_TILE,), lambda i: (i,), memory_space=pltpu.SMEM)
    return pl.pallas_call(
        _scatter_kernel,
        grid=(SEQ // ROW_TILE,),
        in_specs=[smem_tile, smem_tile,
                  pl.BlockSpec((ROW_TILE, D_MODEL), lambda i: (i, 0)),
                  pl.BlockSpec(memory_space=pl.ANY)],
        out_specs=pl.BlockSpec(memory_space=pl.ANY),
        out_shape=jax.ShapeDtypeStruct((MOE_SLOTS, PACK_W), jnp.uint32),
        scratch_shapes=[pltpu.VMEM((ROW_TILE, PACK_W), jnp.uint32), pltpu.SemaphoreType.DMA],
        input_output_aliases={3: 0},
        compiler_params=pltpu.CompilerParams(dimension_semantics=("arbitrary",), vmem_limit_bytes=VMEM_LIMIT),
        name="moe_scatter",
    )(pos0, pos1, x, jnp.zeros((MOE_SLOTS, PACK_W), jnp.uint32))


def _moe_kernel(items_ref, xs_ref, wg_ref, wu_ref, wd_ref, y_ref, xb_sc):
    w = pl.program_id(0)
    j = pl.program_id(1)
    lo = items_ref[ITEM_LO, w]
    hi = items_ref[ITEM_HI, w]

    @pl.when((j == 0) & (hi > lo))
    def _():
        xlo, xhi = _unpack_bf16_pairs(xs_ref[...])
        xb_sc[:, :PACK_W] = xlo
        xb_sc[:, PACK_W:] = xhi

    @pl.when((j == 0) & (items_ref[ITEM_FIRST, w] == 1))
    def _():
        y_ref[...] = jnp.zeros_like(y_ref)

    for sb in range(MOE_SUBS):
        @pl.when((lo <= sb) & (sb < hi))
        def _(sb=sb):
            rows = slice(sb * MOE_SUB, (sb + 1) * MOE_SUB)
            y_ref[rows, :] += _swiglu_partial(xb_sc[rows, :], wg_ref[0], wu_ref[0], wd_ref[0])


def _moe_grouped(items, xs, wg, wu, wd):
    tf = FFN_COL_TILE
    nf = D_FF // tf
    jf = lambda w, j, it: jnp.where(it[ITEM_HI, w] > it[ITEM_LO, w], j, nf - 1)
    grid_spec = pltpu.PrefetchScalarGridSpec(
        num_scalar_prefetch=1,
        grid=(MOE_ITEMS, nf),
        in_specs=[pl.BlockSpec((MOE_ROWS, PACK_W), lambda w, j, it: (it[ITEM_TILE, w], 0)),
                  pl.BlockSpec((1, D_MODEL, tf), lambda w, j, it: (it[ITEM_EXPERT, w], 0, jf(w, j, it))),
                  pl.BlockSpec((1, D_MODEL, tf), lambda w, j, it: (it[ITEM_EXPERT, w], 0, jf(w, j, it))),
                  pl.BlockSpec((1, tf, D_MODEL), lambda w, j, it: (it[ITEM_EXPERT, w], jf(w, j, it), 0))],
        out_specs=pl.BlockSpec((MOE_ROWS, D_MODEL), lambda w, j, it: (it[ITEM_TILE, w], 0)),
        scratch_shapes=[pltpu.VMEM((MOE_ROWS, D_MODEL), BF16)],
    )
    return pl.pallas_call(
        _moe_kernel,
        grid_spec=grid_spec,
        out_shape=jax.ShapeDtypeStruct((MOE_SLOTS, D_MODEL), F32),
        compiler_params=pltpu.CompilerParams(dimension_semantics=("arbitrary", "arbitrary"),
                                             vmem_limit_bytes=VMEM_LIMIT),
        name="moe_grouped",
    )(items, xs, wg, wu, wd)


def _combine_kernel(pos0_ref, pos1_ref, meta_ref, h_ref, gf_ref, y_ref, o_ref, y0_sc, y1_sc, sem):
    def issue(g, c):
        for u in range(DMA_UNROLL):
            t = g * DMA_UNROLL + u
            _row_copy(y_ref, pos0_ref[t], y0_sc, t, sem).start(priority=0)
            _row_copy(y_ref, pos1_ref[t], y1_sc, t, sem).start(priority=1)
        return c

    def drain(g, c):
        for _ in range(DMA_UNROLL):
            _row_copy(y_ref, 0, y0_sc, 0, sem).wait()
            _row_copy(y_ref, 0, y1_sc, 0, sem).wait()
        return c

    lax.fori_loop(0, ROW_TILE // DMA_UNROLL, issue, 0)
    lax.fori_loop(0, ROW_TILE // DMA_UNROLL, drain, 0)
    meta = meta_ref[...]
    g0 = meta[:, META_G0:META_G0 + 1]
    g1 = meta[:, META_G1:META_G1 + 1]
    o_ref[...] = _rms(h_ref[...] + (g0 * y0_sc[...] + g1 * y1_sc[...]), gf_ref[...])


def _moe_combine(pos0, pos1, meta, h, g_final, y):
    smem_tile = pl.BlockSpec((ROW_TILE,), lambda i: (i,), memory_space=pltpu.SMEM)
    return pl.pallas_call(
        _combine_kernel,
        grid=(SEQ // ROW_TILE,),
        in_specs=[smem_tile, smem_tile,
                  pl.BlockSpec((ROW_TILE, LANES), lambda i: (i, 0)),
                  pl.BlockSpec((ROW_TILE, D_MODEL), lambda i: (i, 0)),
                  pl.BlockSpec((1, D_MODEL), lambda i: (0, 0)),
                  pl.BlockSpec(memory_space=pl.ANY)],
        out_specs=pl.BlockSpec((ROW_TILE, D_MODEL), lambda i: (i, 0)),
        out_shape=jax.ShapeDtypeStruct((SEQ, D_MODEL), F32),
        scratch_shapes=[pltpu.VMEM((ROW_TILE, D_MODEL), F32), pltpu.VMEM((ROW_TILE, D_MODEL), F32),
                        pltpu.SemaphoreType.DMA],
        compiler_params=pltpu.CompilerParams(dimension_semantics=("arbitrary",), vmem_limit_bytes=VMEM_LIMIT),
        name="moe_combine",
    )(pos0, pos1, meta, h, g_final.reshape(1, D_MODEL), y)


def _moe_plan(meta, counts):
    i32 = jnp.int32
    cnt = counts[0, :N_EXPERTS].astype(i32)
    padded = (cnt + MOE_SUB - 1) // MOE_SUB * MOE_SUB
    start = jnp.cumsum(padded) - padded
    e0, e1 = meta[:, META_E0].astype(i32), meta[:, META_E1].astype(i32)
    pos0 = start[e0] + meta[:, META_RANK0].astype(i32)
    pos1 = start[e1] + meta[:, META_RANK1].astype(i32)

    first_sb, end_sb = start // MOE_SUB, (start + padded) // MOE_SUB
    t0 = jnp.arange(MOE_TILES, dtype=i32)[:, None] * MOE_SUBS
    lo = jnp.clip(first_sb[None, :] - t0, 0, MOE_SUBS)
    hi = jnp.clip(end_sb[None, :] - t0, 0, MOE_SUBS)
    active = hi > lo
    none = ~jnp.any(active, axis=1, keepdims=True)
    active = jnp.concatenate([active, none], axis=1)
    lo = jnp.concatenate([lo, jnp.zeros_like(t0)], axis=1)
    hi = jnp.concatenate([hi, jnp.zeros_like(t0)], axis=1)
    first = active & (jnp.cumsum(active.astype(i32), axis=1) == 1)
    ncol = N_EXPERTS + 1
    flat = jnp.arange(MOE_TILES * ncol, dtype=i32)
    key = jnp.where(active.reshape(-1), flat, MOE_TILES * ncol)
    order = jnp.argsort(key)[:MOE_ITEMS]
    live = key[order] < MOE_TILES * ncol
    tile = jnp.where(live, order // ncol, MOE_TILES - 1)
    lo_i = jnp.where(live, lo.reshape(-1)[order], 0)
    hi_i = jnp.where(live, hi.reshape(-1)[order], 0)
    first_i = jnp.where(live, first.reshape(-1)[order], False).astype(i32)
    idx = jnp.arange(MOE_ITEMS, dtype=i32)
    src = jnp.maximum(lax.cummax(jnp.where(hi_i > lo_i, idx, -1), axis=0), 0)
    expert = jnp.minimum(order % ncol, N_EXPERTS - 1)[src]
    items = jnp.stack([tile, expert, lo_i, hi_i, first_i]).astype(i32)
    return pos0, pos1, items


def _take_runs(w, idx, axis):
    idx = list(idx)
    runs, start = [], 0
    for i in range(1, len(idx) + 1):
        if i == len(idx) or idx[i] != idx[i - 1] + 1:
            runs.append(lax.slice_in_dim(w, idx[start], idx[i - 1] + 1, axis=axis))
            start = i
    return jnp.concatenate(runs, axis=axis)


def _swa_weight_layout():
    head = lambda base, j: list(range(base + j * HEAD_DIM, base + (j + 1) * HEAD_DIM))
    k0, v0, m0 = SWA_Q, SWA_Q + SWA_KV, SWA_Q + 2 * SWA_KV
    cols, rows = [], []
    for a, b in SWA_PAIR_SLABS:
        cols += head(0, a) + head(0, b)
        rows += head(0, a) + head(0, b)
    for base in (k0, v0):
        cols += head(base, 0) + head(base, 1) + head(base, 2) + head(base, 2)
    cols += list(range(m0, m0 + MEM_W))
    rows += list(range(SWA_Q, SWA_Q + MEM_W))
    scale = np.ones((len(cols),), np.float32)
    scale[:SWA_Q] = Q_SCALE
    scale[SWA_M0:] = Q_SCALE
    assert len(cols) == SWA_PW and len(rows) == D_MODEL
    return np.asarray(cols), scale, np.asarray(rows)


def _dil_weight_layout():
    n_grp = len(DIL_GROUPS) * DIL_SLAB
    cols = list(range(DIL_SLAB)) + list(range(n_grp, n_grp + MEM_W)) + list(range(DIL_SLAB, n_grp))
    scale = np.ones((len(cols),), np.float32)
    for c0 in (0, DIL_SLAB, TOK_W, TOK_W + DIL_SLAB):
        scale[c0:c0 + DIL_W] = Q_SCALE
    return np.asarray(cols), scale


def kernel(x, mem, rel_bias_table, mem_norm, norm_mix, norm_ffn, final_norm, swa_w_in, swa_sinks, swa_w_mem_kv,
           swa_w_out, dil_w_in, dil_w_mem_kv, dil_w_out, ffn_gate, ffn_up, ffn_down, router, moe_gate, moe_up,
           moe_down):
    assert x.shape == (1, SEQ, D_MODEL) and mem.shape == (1, N_MEM, D_MODEL)
    assert norm_mix.shape == (2, D_MODEL) and swa_w_in.shape == (1, D_MODEL, SWA_IN)
    assert dil_w_in.shape == (1, D_MODEL, DIL_IN) and moe_gate.shape == (1, N_EXPERTS, D_MODEL, D_FF)
    bf = lambda a: a.astype(BF16)
    h0 = x.reshape(SEQ, D_MODEL)

    kvm = _memkv(mem[0], mem_norm, bf(jnp.concatenate([swa_w_mem_kv[0], dil_w_mem_kv[0]], axis=1)))

    cols0, scale0, rows0 = _swa_weight_layout()
    p0 = _proj(h0, norm_mix[0], bf(_take_runs(swa_w_in[0], cols0, 1) * scale0), "swa_proj")
    h1, hn1 = _swa_layer(p0, kvm, rel_bias_table, swa_sinks[0], bf(_take_runs(swa_w_out[0], rows0, 0)), h0,
                         norm_ffn[0])
    h2 = _ffn(hn1, h1, bf(ffn_gate[0]), bf(ffn_up[0]), bf(ffn_down[0]))

    cols1, scale1 = _dil_weight_layout()
    tok, g1, g2 = _dilproj(h2, norm_mix[1], bf(_take_runs(dil_w_in[0], cols1, 1) * scale1))
    groups = [_dil_group(qkv, rel_bias_table, gi) for gi, qkv in enumerate((tok[None], g1, g2))]
    router_p = jnp.pad(bf(router[0]), ((0, 0), (0, LANES - N_EXPERTS)))
    h3, hn3, meta, counts = _dilout([g[0] for g in groups], [g[1] for g in groups], tok, kvm, bf(dil_w_out[0]), h2,
                                    norm_ffn[1], router_p)
    pos0, pos1, items = _moe_plan(meta, counts)
    xs = _moe_scatter(hn3, pos0, pos1)
    y = _moe_grouped(items, xs, bf(moe_gate[0]), bf(moe_up[0]), bf(moe_down[0]))
    out = _moe_combine(pos0, pos1, meta, h3, final_norm, y)
    return out.reshape(1, SEQ, D_MODEL)
```

```python
import functools
import math

import jax
import jax.numpy as jnp
import numpy as np
from jax import lax
from jax.experimental import pallas as pl
from jax.experimental.pallas import tpu as pltpu

F32 = jnp.float32
BF16 = jnp.bfloat16

D_MODEL = 1024
SEQ = 16384
HEAD_DIM = 64
N_MIX_HEADS = 12
SWA_KV_HEADS = 3
SWA_GROUP = N_MIX_HEADS // SWA_KV_HEADS
SWA_WINDOW = 128
DIL_GROUPS = ((128, 1), (512, 4), (2048, 16))
DIL_HEADS = 4
N_MEM = 256
MEM_HEADS = 4
BLOCK = 128
N_BUCKETS = 32
MAX_DISTANCE = 2048
D_FF = 3584
N_EXPERTS = 8
EPS = 1e-5
NEG = -1e30
Q_SCALE = HEAD_DIM ** -0.5

SWA_Q = N_MIX_HEADS * HEAD_DIM
SWA_KV = SWA_KV_HEADS * HEAD_DIM
SWA_IN = SWA_Q + 2 * SWA_KV + MEM_HEADS * HEAD_DIM
DIL_W = DIL_HEADS * HEAD_DIM
DIL_IN = len(DIL_GROUPS) * 3 * DIL_W + MEM_HEADS * HEAD_DIM
MEM_W = MEM_HEADS * HEAD_DIM

LANES = 128
SUBLANES = 8
VMEM_LIMIT = 56 * 1024 * 1024

ROW_TILE = 512
BLOCKS_PER_TILE = ROW_TILE // BLOCK
FFN_ROW_TILE = 1024
FFN_COL_TILE = 512


def _bucket_map(dil):
    qi = np.arange(BLOCK)[:, None]
    kj = np.arange(2 * BLOCK)[None, :]
    d = np.maximum((qi + BLOCK - kj) * dil, 0)
    max_exact = N_BUCKETS // 2
    ratio = np.maximum(d, 1).astype(np.float32) / np.float32(max_exact)
    large = max_exact + (np.log(ratio) / np.float32(math.log(MAX_DISTANCE / max_exact))
                         * np.float32(N_BUCKETS - max_exact)).astype(np.int32)
    return np.where(d < max_exact, d, np.minimum(large, N_BUCKETS - 1)).astype(np.int32)


def _rms(x, g):
    ms = jnp.mean(x * x, axis=-1, keepdims=True)
    return x * lax.rsqrt(ms + EPS) * g


def _dot_nt(a, b):
    return lax.dot_general(a, b, (((1,), (1,)), ((), ())), preferred_element_type=F32)


def _dot_tn(a, b):
    return lax.dot_general(a, b, (((0,), (0,)), ((), ())), preferred_element_type=F32)


def _fill_bias(bias_sc, tab_ref, bucket, heads):
    for j, h in enumerate(heads):
        def body(k, b, h=h):
            return jnp.where(bucket == k, tab_ref[k, h], b)
        bias_sc[:, j * BLOCK:(j + 1) * BLOCK] = lax.fori_loop(0, N_BUCKETS, body, jnp.zeros(bucket.shape, F32))


def _fill_row(row_sc, values):
    blk = lax.broadcasted_iota(jnp.int32, row_sc.shape, 1) // BLOCK
    row = jnp.zeros(row_sc.shape, F32)
    for j, v in enumerate(values):
        row = jnp.where(blk == j, v, row)
    row_sc[...] = row


def _folded_bucket_map(dil):
    full = _bucket_map(dil)
    qi = np.arange(BLOCK)[:, None]
    c = np.arange(BLOCK)[None, :]
    return np.ascontiguousarray(np.where(c <= qi, full[:, BLOCK:], full[:, :BLOCK]).T.astype(np.int32))


def _diag_bucket(dil):
    return int(_bucket_map(dil)[0, 0])


def _fold_masks(n):
    c = lax.broadcasted_iota(jnp.int32, (BLOCK, n), 0)
    qi = lax.broadcasted_iota(jnp.int32, (BLOCK, n), 1) % BLOCK
    return c <= qi, c == qi


def _pair_scores(k_slab, q_rows):
    lane = lax.broadcasted_iota(jnp.int32, q_rows.shape, 1)
    zero = jnp.zeros_like(q_rows)
    qa = jnp.where(lane < HEAD_DIM, q_rows, zero)
    qb = jnp.where(lane < HEAD_DIM, zero, q_rows)
    return _dot_nt(k_slab, qa), _dot_nt(k_slab, qb)


def _pair_values(v_slab, p_a, p_b):
    oa = _dot_tn(v_slab, p_a)
    ob = _dot_tn(v_slab, p_b)
    row = lax.broadcasted_iota(jnp.int32, oa.shape, 0)
    return jnp.where(row < HEAD_DIM, oa, ob)


def _band_softmax(st, bias, own, eye, has_prev, sink, diag_bias):
    s_prev, s_own = st[:BLOCK], st[BLOCK:]
    t = jnp.where(own | has_prev, jnp.where(own, s_own, s_prev) + bias, NEG)
    m = jnp.max(t, axis=0, keepdims=True)
    if diag_bias is not None:
        s_d = jnp.sum(jnp.where(eye, s_prev, 0.0), axis=0, keepdims=True) + diag_bias
        s_d = jnp.where(has_prev, s_d, NEG)
        m = jnp.maximum(m, s_d)
    if sink is not None:
        m = jnp.maximum(m, sink)
    e = jnp.exp(t - m)
    den = jnp.sum(e, axis=0, keepdims=True)
    if diag_bias is not None:
        e_d = jnp.exp(s_d - m)
        den = den + e_d
    if sink is not None:
        den = den + jnp.exp(sink - m)
    inv = 1.0 / den
    p = e * inv
    p_own = jnp.where(own, p, 0.0)
    p_prev = jnp.where(own, 0.0, p)
    if diag_bias is not None:
        p_prev = jnp.where(eye, e_d * inv, p_prev)
    return jnp.concatenate([p_prev, p_own], axis=0).astype(BF16), m + jnp.log(den)


def _mem_attention(qm, kvm):
    outs = []
    for s in range(MEM_W // LANES):
        sa, sb = _pair_scores(kvm[:, s * LANES:(s + 1) * LANES], qm[:, s * LANES:(s + 1) * LANES])
        ps = []
        for st in (sa, sb):
            e = jnp.exp(st - jnp.max(st, axis=0, keepdims=True))
            ps.append((e * (1.0 / jnp.sum(e, axis=0, keepdims=True))).astype(BF16))
        outs.append(_pair_values(kvm[:, MEM_W + s * LANES:MEM_W + (s + 1) * LANES], ps[0], ps[1]))
    return outs


def _to_rows(slabs_t):
    return jnp.concatenate([jnp.transpose(x).astype(BF16) for x in slabs_t], axis=-1)


def _memkv_kernel(mem_ref, g_ref, w_ref, o_ref):
    mn = _rms(mem_ref[...], g_ref[...]).astype(BF16)
    o_ref[...] = jnp.dot(mn, w_ref[...], preferred_element_type=F32).astype(BF16)


def _memkv(mem, g, w):
    return pl.pallas_call(
        _memkv_kernel,
        out_shape=jax.ShapeDtypeStruct((N_MEM, w.shape[1]), BF16),
        name="mem_kv",
    )(mem, g.reshape(1, D_MODEL), w)


def _proj_kernel(h_ref, g_ref, w_ref, o_ref):
    hn = _rms(h_ref[...], g_ref[...]).astype(BF16)
    o_ref[...] = jnp.dot(hn, w_ref[...], preferred_element_type=F32).astype(BF16)


def _proj(h, g, w, name):
    n = w.shape[1]
    return pl.pallas_call(
        _proj_kernel,
        grid=(SEQ // ROW_TILE,),
        in_specs=[pl.BlockSpec((ROW_TILE, D_MODEL), lambda i: (i, 0)),
                  pl.BlockSpec((1, D_MODEL), lambda i: (0, 0)),
                  pl.BlockSpec((D_MODEL, n), lambda i: (0, 0))],
        out_specs=pl.BlockSpec((ROW_TILE, n), lambda i: (i, 0)),
        out_shape=jax.ShapeDtypeStruct((SEQ, n), BF16),
        compiler_params=pltpu.CompilerParams(dimension_semantics=("arbitrary",), vmem_limit_bytes=VMEM_LIMIT),
        name=name,
    )(h, g.reshape(1, D_MODEL), w)


SWA_PAIR_SLABS = [(g, SWA_GROUP + g) for g in range(SWA_GROUP)] + [(8, 9), (10, 11)]
SWA_K0 = SWA_Q
SWA_V0 = SWA_K0 + 2 * LANES
SWA_M0 = SWA_V0 + 2 * LANES
SWA_PW = SWA_M0 + MEM_W
SWA_SCORE_HEADS = [0, 1, 2, 3, 4, 5, 6, 7, 8, 10, 9, 11]
SWA_NQ = N_MIX_HEADS * BLOCK


def _swa_kernel(tab_ref, sink_ref, bucket_ref, p_ref, kprev_ref, vprev_ref, kvm_ref, wout_ref, h_ref, g_ref,
                h1_ref, hn1_ref, bias_sc, sink_sc, k_sc, v_sc, cat_sc):
    i = pl.program_id(0)

    @pl.when(i == 0)
    def _():
        _fill_bias(bias_sc, tab_ref, bucket_ref[...], SWA_SCORE_HEADS)
        _fill_row(sink_sc, [sink_ref[h] for h in SWA_SCORE_HEADS])

    k_sc[0:BLOCK, :] = kprev_ref[...]
    k_sc[BLOCK:, :] = p_ref[:, SWA_K0:SWA_V0]
    v_sc[0:BLOCK, :] = vprev_ref[...]
    v_sc[BLOCK:, :] = p_ref[:, SWA_V0:SWA_M0]
    own, eye = _fold_masks(SWA_NQ)

    def block_body(b, carry):
        r0 = pl.multiple_of(b * BLOCK, BLOCK)
        has_prev = i * BLOCKS_PER_TILE + b > 0
        qb = p_ref[pl.ds(r0, BLOCK), 0:SWA_Q]
        slab = lambda x, s: x[:, s * LANES:(s + 1) * LANES]
        qa = jnp.concatenate([slab(qb, s) for s in range(4)], axis=0)
        qc = jnp.concatenate([slab(qb, s) for s in (4, 5)], axis=0)
        kb = k_sc[pl.ds(r0, 2 * BLOCK), :]
        vb = v_sc[pl.ds(r0, 2 * BLOCK), :]
        s0, s1 = _pair_scores(slab(kb, 0), qa)
        s2, s3 = _pair_scores(slab(kb, 1), qc)
        st = jnp.concatenate([s0, s1, s2, s3], axis=1)
        pt, _ = _band_softmax(st, bias_sc[...], own, eye, has_prev, sink_sc[...], None)
        na, nc = 4 * BLOCK, 2 * BLOCK
        oa = _pair_values(slab(vb, 0), pt[:, 0:na], pt[:, na:2 * na])
        oc = _pair_values(slab(vb, 1), pt[:, 2 * na:2 * na + nc], pt[:, 2 * na + nc:])
        outs = [slab(oa, s) for s in range(4)] + [slab(oc, s) for s in range(2)]
        outs += _mem_attention(p_ref[pl.ds(r0, BLOCK), SWA_M0:], kvm_ref[...])
        cat_sc[pl.ds(r0, BLOCK), :] = _to_rows(outs)
        return carry

    lax.fori_loop(0, BLOCKS_PER_TILE, block_body, 0)

    out = h_ref[...] + jnp.dot(cat_sc[...], wout_ref[...], preferred_element_type=F32)
    h1_ref[...] = out
    hn1_ref[...] = _rms(out, g_ref[...]).astype(BF16)


def _swa_layer(p, kvm, table, sinks, wout, h, g_ffn):
    assert SWA_WINDOW == BLOCK
    kv_w = 2 * LANES
    prev = lambda i: jnp.maximum(i * BLOCKS_PER_TILE - 1, 0)
    return pl.pallas_call(
        _swa_kernel,
        grid=(SEQ // ROW_TILE,),
        in_specs=[
            pl.BlockSpec(memory_space=pltpu.SMEM),
            pl.BlockSpec(memory_space=pltpu.SMEM),
            pl.BlockSpec((BLOCK, BLOCK), lambda i: (0, 0)),
            pl.BlockSpec((ROW_TILE, SWA_PW), lambda i: (i, 0)),
            pl.BlockSpec((BLOCK, kv_w), lambda i: (prev(i), SWA_K0 // kv_w)),
            pl.BlockSpec((BLOCK, kv_w), lambda i: (prev(i), SWA_V0 // kv_w)),
            pl.BlockSpec((N_MEM, 2 * MEM_W), lambda i: (0, 0)),
            pl.BlockSpec((D_MODEL, D_MODEL), lambda i: (0, 0)),
            pl.BlockSpec((ROW_TILE, D_MODEL), lambda i: (i, 0)),
            pl.BlockSpec((1, D_MODEL), lambda i: (0, 0)),
        ],
        out_specs=[pl.BlockSpec((ROW_TILE, D_MODEL), lambda i: (i, 0)),
                   pl.BlockSpec((ROW_TILE, D_MODEL), lambda i: (i, 0))],
        out_shape=[jax.ShapeDtypeStruct((SEQ, D_MODEL), F32), jax.ShapeDtypeStruct((SEQ, D_MODEL), BF16)],
        scratch_shapes=[pltpu.VMEM((BLOCK, SWA_NQ), F32),
                        pltpu.VMEM((1, SWA_NQ), F32),
                        pltpu.VMEM((ROW_TILE + BLOCK, kv_w), BF16),
                        pltpu.VMEM((ROW_TILE + BLOCK, kv_w), BF16),
                        pltpu.VMEM((ROW_TILE, D_MODEL), BF16)],
        compiler_params=pltpu.CompilerParams(dimension_semantics=("arbitrary",), vmem_limit_bytes=VMEM_LIMIT),
        name="swa_mixer",
    )(table, sinks, jnp.asarray(_folded_bucket_map(1)), p, p, p, kvm, wout, h, g_ffn.reshape(1, D_MODEL))


DIL_SLAB = 3 * DIL_W
TOK_W = DIL_SLAB + MEM_W
SLABS_PER_GROUP = DIL_SLAB // LANES


def _dilproj_kernel(h_ref, g_ref, w_ref, tok_ref, g1_ref, g2_ref, slab_sc):
    hn = _rms(h_ref[...], g_ref[...]).astype(BF16)
    res = jnp.dot(hn, w_ref[...], preferred_element_type=F32)
    tok_ref[...] = res[:, :TOK_W].astype(BF16)
    for s in range(2 * SLABS_PER_GROUP):
        slab_sc[s] = res[:, TOK_W + s * LANES:TOK_W + (s + 1) * LANES]
    for gi, out_ref in ((1, g1_ref), (2, g2_ref)):
        d = DIL_GROUPS[gi][1]
        for s in range(SLABS_PER_GROUP):
            for r in range(d):
                rows = slab_sc[(gi - 1) * SLABS_PER_GROUP + s, pl.ds(r, ROW_TILE // d, stride=d), :]
                out_ref[r, :, s * LANES:(s + 1) * LANES] = rows.astype(BF16)


def _dilproj(h, g, w):
    d1, d2 = DIL_GROUPS[1][1], DIL_GROUPS[2][1]
    return pl.pallas_call(
        _dilproj_kernel,
        grid=(SEQ // ROW_TILE,),
        in_specs=[pl.BlockSpec((ROW_TILE, D_MODEL), lambda i: (i, 0)),
                  pl.BlockSpec((1, D_MODEL), lambda i: (0, 0)),
                  pl.BlockSpec((D_MODEL, DIL_IN), lambda i: (0, 0))],
        out_specs=[pl.BlockSpec((ROW_TILE, TOK_W), lambda i: (i, 0)),
                   pl.BlockSpec((d1, ROW_TILE // d1, DIL_SLAB), lambda i: (0, i, 0)),
                   pl.BlockSpec((d2, ROW_TILE // d2, DIL_SLAB), lambda i: (0, i, 0))],
        out_shape=[jax.ShapeDtypeStruct((SEQ, TOK_W), BF16),
                   jax.ShapeDtypeStruct((d1, SEQ // d1, DIL_SLAB), BF16),
                   jax.ShapeDtypeStruct((d2, SEQ // d2, DIL_SLAB), BF16)],
        scratch_shapes=[pltpu.VMEM((2 * SLABS_PER_GROUP, ROW_TILE, LANES), F32)],
        compiler_params=pltpu.CompilerParams(dimension_semantics=("arbitrary",), vmem_limit_bytes=VMEM_LIMIT),
        name="dil_proj",
    )(h, g.reshape(1, D_MODEL), w)


def _dil_kernel(tab_ref, bucket_ref, q_ref, k_ref, v_ref, kp_ref, vp_ref, o_ref, l_ref,
                bias_sc, diag_sc, k_sc, v_sc, *, head0, diag_bucket):
    r = pl.program_id(0)
    n = pl.program_id(1)
    heads = [head0 + h for h in range(DIL_HEADS)]
    nq = DIL_HEADS * BLOCK

    @pl.when((r == 0) & (n == 0))
    def _():
        _fill_bias(bias_sc, tab_ref, bucket_ref[...], heads)
        _fill_row(diag_sc, [tab_ref[diag_bucket, h] for h in heads])

    k_sc[0:BLOCK, :] = kp_ref[...]
    k_sc[BLOCK:, :] = k_ref[...]
    v_sc[0:BLOCK, :] = vp_ref[...]
    v_sc[BLOCK:, :] = v_ref[...]
    own, eye = _fold_masks(nq)
    upper_rows = lax.broadcasted_iota(jnp.int32, (BLOCK, BLOCK), 0) < HEAD_DIM

    def block_body(b, carry):
        r0 = pl.multiple_of(b * BLOCK, BLOCK)
        has_prev = n * BLOCKS_PER_TILE + b > 0
        qb = q_ref[pl.ds(r0, BLOCK), :]
        kb = k_sc[pl.ds(r0, 2 * BLOCK), :]
        vb = v_sc[pl.ds(r0, 2 * BLOCK), :]
        slab = lambda x, s: x[:, s * LANES:(s + 1) * LANES]
        scores = []
        for s in range(DIL_W // LANES):
            scores += _pair_scores(slab(kb, s), slab(qb, s))
        pt, lse = _band_softmax(jnp.concatenate(scores, axis=1), bias_sc[...], own, eye, has_prev, None,
                                diag_sc[...])
        outs, lses = [], []
        for s in range(DIL_W // LANES):
            ca, cb = 2 * s * BLOCK, (2 * s + 1) * BLOCK
            outs.append(jnp.transpose(_pair_values(slab(vb, s), pt[:, ca:ca + BLOCK], pt[:, cb:cb + BLOCK])))
            lse_t = jnp.where(upper_rows, jnp.broadcast_to(lse[:, ca:ca + BLOCK], (BLOCK, BLOCK)),
                              jnp.broadcast_to(lse[:, cb:cb + BLOCK], (BLOCK, BLOCK)))
            lses.append(jnp.transpose(lse_t))
        o_ref[pl.ds(r0, BLOCK), :] = jnp.concatenate(outs, axis=-1)
        l_ref[pl.ds(r0, BLOCK), :] = jnp.concatenate(lses, axis=-1)
        return carry

    lax.fori_loop(0, BLOCKS_PER_TILE, block_body, 0)


def _dil_group(qkv, table, gi):
    window, d = DIL_GROUPS[gi]
    rows = SEQ // d
    assert qkv.shape[:2] == (d, rows) and window // d == BLOCK
    prev = lambda n: jnp.maximum(n * BLOCKS_PER_TILE - 1, 0)
    tile = lambda c: pl.BlockSpec((None, ROW_TILE, DIL_W), lambda r, n: (r, n, c))
    prev_block = lambda c: pl.BlockSpec((None, BLOCK, DIL_W), lambda r, n: (r, prev(n), c))
    return pl.pallas_call(
        functools.partial(_dil_kernel, head0=gi * DIL_HEADS, diag_bucket=_diag_bucket(d)),
        grid=(d, rows // ROW_TILE),
        in_specs=[
            pl.BlockSpec(memory_space=pltpu.SMEM),
            pl.BlockSpec((BLOCK, BLOCK), lambda r, n: (0, 0)),
            tile(0), tile(1), tile(2), prev_block(1), prev_block(2),
        ],
        out_specs=[tile(0), tile(0)],
        out_shape=[jax.ShapeDtypeStruct((d, rows, DIL_W), F32), jax.ShapeDtypeStruct((d, rows, DIL_W), F32)],
        scratch_shapes=[pltpu.VMEM((BLOCK, DIL_HEADS * BLOCK), F32),
                        pltpu.VMEM((1, DIL_HEADS * BLOCK), F32),
                        pltpu.VMEM((ROW_TILE + BLOCK, DIL_W), BF16),
                        pltpu.VMEM((ROW_TILE + BLOCK, DIL_W), BF16)],
        compiler_params=pltpu.CompilerParams(dimension_semantics=("arbitrary", "arbitrary"),
                                             vmem_limit_bytes=VMEM_LIMIT),
        name=f"dil_attn_{gi}",
    )(table, jnp.asarray(_folded_bucket_map(d)), qkv, qkv, qkv, qkv, qkv)


def _dilout_kernel(o0_ref, o1_ref, o2_ref, l0_ref, l1_ref, l2_ref, qm_ref, kvm_ref, wout_ref, h_ref, g_ref,
                   router_ref, h2_ref, hn2_ref, meta_ref, cnt_ref, carry_sc, tok_sc):
    @pl.when(pl.program_id(0) == 0)
    def _():
        carry_sc[...] = jnp.zeros_like(carry_sc)

    for k, src_ref in enumerate((o1_ref, l1_ref, o2_ref, l2_ref)):
        d = src_ref.shape[0]
        for s in range(DIL_W // LANES):
            for r in range(d):
                tok_sc[k, s, pl.ds(r, ROW_TILE // d, stride=d), :] = src_ref[r, :, s * LANES:(s + 1) * LANES]

    mixed = []
    for s in range(DIL_W // LANES):
        cols = slice(s * LANES, (s + 1) * LANES)
        o0, o1, o2 = o0_ref[:, cols], tok_sc[0, s], tok_sc[2, s]
        l0, l1, l2 = l0_ref[:, cols], tok_sc[1, s], tok_sc[3, s]
        mx = jnp.maximum(jnp.maximum(l0, l1), l2)
        e0, e1, e2 = jnp.exp(l0 - mx), jnp.exp(l1 - mx), jnp.exp(l2 - mx)
        inv = 1.0 / (e0 + e1 + e2)
        mixed.append(((e0 * inv) * o0 + (e1 * inv) * o1 + (e2 * inv) * o2).astype(BF16))
    cat = jnp.concatenate(mixed + [_to_rows(_mem_attention(qm_ref[...], kvm_ref[...]))], axis=-1)
    out = h_ref[...] + jnp.dot(cat, wout_ref[...], preferred_element_type=F32)
    h2_ref[...] = out
    hn = _rms(out, g_ref[...]).astype(BF16)
    hn2_ref[...] = hn

    logits = jnp.dot(hn, router_ref[...], preferred_element_type=F32)
    lane = lax.broadcasted_iota(jnp.int32, logits.shape, 1)
    masked = jnp.where(lane < N_EXPERTS, logits, -jnp.inf)
    v0 = jnp.max(masked, axis=-1, keepdims=True)
    i0 = jnp.min(jnp.where(masked == v0, lane, LANES), axis=-1, keepdims=True)
    rest = jnp.where(lane == i0, -jnp.inf, masked)
    v1 = jnp.max(rest, axis=-1, keepdims=True)
    i1 = jnp.min(jnp.where(rest == v1, lane, LANES), axis=-1, keepdims=True)
    ex = jnp.exp(v1 - v0)
    inv2 = 1.0 / (1.0 + ex)

    oh0, oh1 = lane == i0, lane == i1
    sel = (oh0 | oh1).astype(F32)
    tok_r = lax.broadcasted_iota(jnp.int32, (ROW_TILE, ROW_TILE), 0)
    tok_c = lax.broadcasted_iota(jnp.int32, (ROW_TILE, ROW_TILE), 1)
    earlier = (tok_c < tok_r).astype(F32).astype(BF16)
    before = jnp.dot(earlier, sel.astype(BF16), preferred_element_type=F32) + carry_sc[...]
    rank0 = jnp.sum(jnp.where(oh0, before, 0.0), axis=-1, keepdims=True)
    rank1 = jnp.sum(jnp.where(oh1, before, 0.0), axis=-1, keepdims=True)
    count = carry_sc[...] + jnp.sum(sel, axis=0, keepdims=True)
    carry_sc[...] = count
    cnt_ref[...] = jnp.broadcast_to(count, cnt_ref.shape)

    fields = (i0.astype(F32), i1.astype(F32), inv2, ex * inv2, rank0, rank1)
    meta = jnp.zeros(logits.shape, F32)
    for k, f in enumerate(fields):
        meta = jnp.where(lane == k, f, meta)
    meta_ref[...] = meta


META_E0, META_E1, META_G0, META_G1, META_RANK0, META_RANK1 = range(6)


def _dilout(os, ls, tok, kvm, wout, h, g_ffn, router):
    row = lambda i: (i, 0)
    const = lambda i: (0, 0)

    def group_spec(a):
        d = a.shape[0]
        if d == 1:
            return pl.BlockSpec((None, ROW_TILE, DIL_W), lambda i: (0, i, 0))
        return pl.BlockSpec((d, ROW_TILE // d, DIL_W), lambda i: (0, i, 0))

    return pl.pallas_call(
        _dilout_kernel,
        grid=(SEQ // ROW_TILE,),
        in_specs=[group_spec(a) for a in (*os, *ls)] + [
                  pl.BlockSpec((ROW_TILE, MEM_W), lambda i: (i, DIL_SLAB // MEM_W)),
                  pl.BlockSpec((N_MEM, 2 * MEM_W), lambda i: (0, 1)),
                  pl.BlockSpec((DIL_W + MEM_W, D_MODEL), const),
                  pl.BlockSpec((ROW_TILE, D_MODEL), row),
                  pl.BlockSpec((1, D_MODEL), const),
                  pl.BlockSpec((D_MODEL, LANES), const)],
        out_specs=[pl.BlockSpec((ROW_TILE, D_MODEL), row),
                   pl.BlockSpec((ROW_TILE, D_MODEL), row),
                   pl.BlockSpec((ROW_TILE, LANES), row),
                   pl.BlockSpec((SUBLANES, LANES), const)],
        out_shape=[jax.ShapeDtypeStruct((SEQ, D_MODEL), F32), jax.ShapeDtypeStruct((SEQ, D_MODEL), BF16),
                   jax.ShapeDtypeStruct((SEQ, LANES), F32), jax.ShapeDtypeStruct((SUBLANES, LANES), F32)],
        scratch_shapes=[pltpu.VMEM((1, LANES), F32),
                        pltpu.VMEM((4, DIL_W // LANES, ROW_TILE, LANES), F32)],
        compiler_params=pltpu.CompilerParams(dimension_semantics=("arbitrary",), vmem_limit_bytes=VMEM_LIMIT),
        name="dil_out",
    )(*os, *ls, tok, kvm, wout, h, g_ffn.reshape(1, D_MODEL), router)


def _swiglu_partial(x, wg, wu, wd):
    a = jnp.dot(x, wg, preferred_element_type=F32)
    b = jnp.dot(x, wu, preferred_element_type=F32)
    hm = (a * jax.nn.sigmoid(a) * b).astype(BF16)
    return jnp.dot(hm, wd, preferred_element_type=F32)


def _ffn_kernel(x_ref, wg_ref, wu_ref, wd_ref, h_ref, o_ref):
    x = x_ref[...]
    acc = h_ref[...]
    for c in range(D_FF // FFN_COL_TILE):
        cols = slice(c * FFN_COL_TILE, (c + 1) * FFN_COL_TILE)
        acc = acc + _swiglu_partial(x, wg_ref[:, cols], wu_ref[:, cols], wd_ref[cols, :])
    o_ref[...] = acc


def _ffn(x, h, wg, wu, wd):
    tm = ROW_TILE
    resident = lambda shape: pl.BlockSpec(shape, lambda i: (0, 0), pipeline_mode=pl.Buffered(1))
    return pl.pallas_call(
        _ffn_kernel,
        grid=(SEQ // tm,),
        in_specs=[pl.BlockSpec((tm, D_MODEL), lambda i: (i, 0)),
                  resident((D_MODEL, D_FF)), resident((D_MODEL, D_FF)), resident((D_FF, D_MODEL)),
                  pl.BlockSpec((tm, D_MODEL), lambda i: (i, 0))],
        out_specs=pl.BlockSpec((tm, D_MODEL), lambda i: (i, 0)),
        out_shape=jax.ShapeDtypeStruct((SEQ, D_MODEL), F32),
        compiler_params=pltpu.CompilerParams(dimension_semantics=("arbitrary",), vmem_limit_bytes=VMEM_LIMIT),
        name="ffn",
    )(x, wg, wu, wd, h)


MOE_SUB = 512
MOE_ROWS = 1024
MOE_COL_TILE = D_FF // 2
MOE_SUBS = MOE_ROWS // MOE_SUB
MOE_SLOTS = 2 * SEQ + N_EXPERTS * MOE_SUB
MOE_TILES = MOE_SLOTS // MOE_ROWS
MOE_ITEMS = MOE_TILES + N_EXPERTS - 1
PACK_W = D_MODEL // 2
ITEM_TILE, ITEM_EXPERT, ITEM_LO, ITEM_HI, ITEM_FIRST = range(5)
DMA_UNROLL = 8


def _pack_bf16_pairs(x):
    lo = lax.bitcast_convert_type(x[:, :PACK_W].astype(F32), jnp.uint32) >> 16
    hi = lax.bitcast_convert_type(x[:, PACK_W:].astype(F32), jnp.uint32) & jnp.uint32(0xFFFF0000)
    return hi | lo


def _unpack_bf16_pairs(pk):
    lo = lax.bitcast_convert_type(pk << 16, F32).astype(BF16)
    hi = lax.bitcast_convert_type(pk & jnp.uint32(0xFFFF0000), F32).astype(BF16)
    return lo, hi


def _row_copy(src, src_row, dst, dst_row, sem):
    return pltpu.make_async_copy(src.at[pl.ds(src_row, 1), :], dst.at[pl.ds(dst_row, 1), :], sem)


def _scatter_kernel(pos0_ref, pos1_ref, x_ref, xs_in_ref, xs_ref, pk_sc, sem):
    del xs_in_ref
    pk_sc[...] = _pack_bf16_pairs(x_ref[...])

    def issue(g, c):
        for u in range(DMA_UNROLL):
            t = g * DMA_UNROLL + u
            _row_copy(pk_sc, t, xs_ref, pos0_ref[t], sem).start(priority=0)
            _row_copy(pk_sc, t, xs_ref, pos1_ref[t], sem).start(priority=1)
        return c

    def drain(g, c):
        for _ in range(2 * DMA_UNROLL):
            _row_copy(pk_sc, 0, xs_ref, 0, sem).wait()
        return c

    lax.fori_loop(0, ROW_TILE // DMA_UNROLL, issue, 0)
    lax.fori_loop(0, ROW_TILE // DMA_UNROLL, drain, 0)


def _moe_scatter(x, pos0, pos1):
    smem_tile = pl.BlockSpec((ROW_TILE,), lambda i: (i,), memory_space=pltpu.SMEM)
    return pl.pallas_call(
        _scatter_kernel,
        grid=(SEQ // ROW_TILE,),
        in_specs=[smem_tile, smem_tile,
                  pl.BlockSpec((ROW_TILE, D_MODEL), lambda i: (i, 0)),
                  pl.BlockSpec(memory_space=pl.ANY)],
        out_specs=pl.BlockSpec(memory_space=pl.ANY),
        out_shape=jax.ShapeDtypeStruct((MOE_SLOTS, PACK_W), jnp.uint32),
        scratch_shapes=[pltpu.VMEM((ROW_TILE, PACK_W), jnp.uint32), pltpu.SemaphoreType.DMA],
        input_output_aliases={3: 0},
        compiler_params=pltpu.CompilerParams(dimension_semantics=("arbitrary",), vmem_limit_bytes=VMEM_LIMIT),
        name="moe_scatter",
    )(pos0, pos1, x, jnp.zeros((MOE_SLOTS, PACK_W), jnp.uint32))


def _moe_kernel(items_ref, xs_ref, wg_ref, wu_ref, wd_ref, y_ref, xb_sc):
    w = pl.program_id(0)
    j = pl.program_id(1)
    lo = items_ref[ITEM_LO, w]
    hi = items_ref[ITEM_HI, w]

    @pl.when((j == 0) & (hi > lo))
    def _():
        xlo, xhi = _unpack_bf16_pairs(xs_ref[...])
        xb_sc[:, :PACK_W] = xlo
        xb_sc[:, PACK_W:] = xhi

    @pl.when((j == 0) & (items_ref[ITEM_FIRST, w] == 1))
    def _():
        y_ref[...] = jnp.zeros_like(y_ref)

    for sb in range(MOE_SUBS):
        @pl.when((lo <= sb) & (sb < hi))
        def _(sb=sb):
            rows = slice(sb * MOE_SUB, (sb + 1) * MOE_SUB)
            x = xb_sc[rows, :]
            acc = y_ref[rows, :]
            for c0 in range(0, MOE_COL_TILE, FFN_COL_TILE):
                cols = slice(c0, min(c0 + FFN_COL_TILE, MOE_COL_TILE))
                acc = acc + _swiglu_partial(x, wg_ref[0, :, cols], wu_ref[0, :, cols], wd_ref[0, cols, :])
            y_ref[rows, :] = acc


def _moe_grouped(items, xs, wg, wu, wd):
    tf = MOE_COL_TILE
    nf = D_FF // tf
    jf = lambda w, j, it: jnp.where(it[ITEM_HI, w] > it[ITEM_LO, w], j, nf - 1)
    grid_spec = pltpu.PrefetchScalarGridSpec(
        num_scalar_prefetch=1,
        grid=(MOE_ITEMS, nf),
        in_specs=[pl.BlockSpec((MOE_ROWS, PACK_W), lambda w, j, it: (it[ITEM_TILE, w], 0)),
                  pl.BlockSpec((1, D_MODEL, tf), lambda w, j, it: (it[ITEM_EXPERT, w], 0, jf(w, j, it))),
                  pl.BlockSpec((1, D_MODEL, tf), lambda w, j, it: (it[ITEM_EXPERT, w], 0, jf(w, j, it))),
                  pl.BlockSpec((1, tf, D_MODEL), lambda w, j, it: (it[ITEM_EXPERT, w], jf(w, j, it), 0))],
        out_specs=pl.BlockSpec((MOE_ROWS, D_MODEL), lambda w, j, it: (it[ITEM_TILE, w], 0)),
        scratch_shapes=[pltpu.VMEM((MOE_ROWS, D_MODEL), BF16)],
    )
    return pl.pallas_call(
        _moe_kernel,
        grid_spec=grid_spec,
        out_shape=jax.ShapeDtypeStruct((MOE_SLOTS, D_MODEL), F32),
        compiler_params=pltpu.CompilerParams(dimension_semantics=("arbitrary", "arbitrary"),
                                             vmem_limit_bytes=VMEM_LIMIT),
        name="moe_grouped",
    )(items, xs, wg, wu, wd)


def _combine_kernel(pos0_ref, pos1_ref, meta_ref, h_ref, gf_ref, y_ref, o_ref, y0_sc, y1_sc, sem):
    def issue(g, c):
        for u in range(DMA_UNROLL):
            t = g * DMA_UNROLL + u
            _row_copy(y_ref, pos0_ref[t], y0_sc, t, sem).start(priority=0)
            _row_copy(y_ref, pos1_ref[t], y1_sc, t, sem).start(priority=1)
        return c

    def drain(g, c):
        for _ in range(DMA_UNROLL):
            _row_copy(y_ref, 0, y0_sc, 0, sem).wait()
            _row_copy(y_ref, 0, y1_sc, 0, sem).wait()
        return c

    lax.fori_loop(0, ROW_TILE // DMA_UNROLL, issue, 0)
    lax.fori_loop(0, ROW_TILE // DMA_UNROLL, drain, 0)
    meta = meta_ref[...]
    g0 = meta[:, META_G0:META_G0 + 1]
    g1 = meta[:, META_G1:META_G1 + 1]
    o_ref[...] = _rms(h_ref[...] + (g0 * y0_sc[...] + g1 * y1_sc[...]), gf_ref[...])


def _moe_combine(pos0, pos1, meta, h, g_final, y):
    smem_tile = pl.BlockSpec((ROW_TILE,), lambda i: (i,), memory_space=pltpu.SMEM)
    return pl.pallas_call(
        _combine_kernel,
        grid=(SEQ // ROW_TILE,),
        in_specs=[smem_tile, smem_tile,
                  pl.BlockSpec((ROW_TILE, LANES), lambda i: (i, 0)),
                  pl.BlockSpec((ROW_TILE, D_MODEL), lambda i: (i, 0)),
                  pl.BlockSpec((1, D_MODEL), lambda i: (0, 0)),
                  pl.BlockSpec(memory_space=pl.ANY)],
        out_specs=pl.BlockSpec((ROW_TILE, D_MODEL), lambda i: (i, 0)),
        out_shape=jax.ShapeDtypeStruct((SEQ, D_MODEL), F32),
        scratch_shapes=[pltpu.VMEM((ROW_TILE, D_MODEL), F32), pltpu.VMEM((ROW_TILE, D_MODEL), F32),
                        pltpu.SemaphoreType.DMA],
        compiler_params=pltpu.CompilerParams(dimension_semantics=("arbitrary",), vmem_limit_bytes=VMEM_LIMIT),
        name="moe_combine",
    )(pos0, pos1, meta, h, g_final.reshape(1, D_MODEL), y)


def _moe_plan(meta, counts):
    i32 = jnp.int32
    cnt = counts[0, :N_EXPERTS].astype(i32)
    padded = (cnt + MOE_SUB - 1) // MOE_SUB * MOE_SUB
    start = jnp.cumsum(padded) - padded
    e0, e1 = meta[:, META_E0].astype(i32), meta[:, META_E1].astype(i32)
    pos0 = start[e0] + meta[:, META_RANK0].astype(i32)
    pos1 = start[e1] + meta[:, META_RANK1].astype(i32)

    first_sb, end_sb = start // MOE_SUB, (start + padded) // MOE_SUB
    t0 = jnp.arange(MOE_TILES, dtype=i32)[:, None] * MOE_SUBS
    lo = jnp.clip(first_sb[None, :] - t0, 0, MOE_SUBS)
    hi = jnp.clip(end_sb[None, :] - t0, 0, MOE_SUBS)
    active = hi > lo
    none = ~jnp.any(active, axis=1, keepdims=True)
    active = jnp.concatenate([active, none], axis=1)
    lo = jnp.concatenate([lo, jnp.zeros_like(t0)], axis=1)
    hi = jnp.concatenate([hi, jnp.zeros_like(t0)], axis=1)
    first = active & (jnp.cumsum(active.astype(i32), axis=1) == 1)
    ncol = N_EXPERTS + 1
    flat = jnp.arange(MOE_TILES * ncol, dtype=i32)
    key = jnp.where(active.reshape(-1), flat, MOE_TILES * ncol)
    order = jnp.argsort(key)[:MOE_ITEMS]
    live = key[order] < MOE_TILES * ncol
    tile = jnp.where(live, order // ncol, MOE_TILES - 1)
    lo_i = jnp.where(live, lo.reshape(-1)[order], 0)
    hi_i = jnp.where(live, hi.reshape(-1)[order], 0)
    first_i = jnp.where(live, first.reshape(-1)[order], False).astype(i32)
    idx = jnp.arange(MOE_ITEMS, dtype=i32)
    src = jnp.maximum(lax.cummax(jnp.where(hi_i > lo_i, idx, -1), axis=0), 0)
    expert = jnp.minimum(order % ncol, N_EXPERTS - 1)[src]
    items = jnp.stack([tile, expert, lo_i, hi_i, first_i]).astype(i32)
    return pos0, pos1, items


def _take_runs(w, idx, axis):
    idx = list(idx)
    runs, start = [], 0
    for i in range(1, len(idx) + 1):
        if i == len(idx) or idx[i] != idx[i - 1] + 1:
            runs.append(lax.slice_in_dim(w, idx[start], idx[i - 1] + 1, axis=axis))
            start = i
    return jnp.concatenate(runs, axis=axis)


def _swa_weight_layout():
    head = lambda base, j: list(range(base + j * HEAD_DIM, base + (j + 1) * HEAD_DIM))
    k0, v0, m0 = SWA_Q, SWA_Q + SWA_KV, SWA_Q + 2 * SWA_KV
    cols, rows = [], []
    for a, b in SWA_PAIR_SLABS:
        cols += head(0, a) + head(0, b)
        rows += head(0, a) + head(0, b)
    for base in (k0, v0):
        cols += head(base, 0) + head(base, 1) + head(base, 2) + head(base, 2)
    cols += list(range(m0, m0 + MEM_W))
    rows += list(range(SWA_Q, SWA_Q + MEM_W))
    scale = np.ones((len(cols),), np.float32)
    scale[:SWA_Q] = Q_SCALE
    scale[SWA_M0:] = Q_SCALE
    assert len(cols) == SWA_PW and len(rows) == D_MODEL
    return np.asarray(cols), scale, np.asarray(rows)


def _dil_weight_layout():
    n_grp = len(DIL_GROUPS) * DIL_SLAB
    cols = list(range(DIL_SLAB)) + list(range(n_grp, n_grp + MEM_W)) + list(range(DIL_SLAB, n_grp))
    scale = np.ones((len(cols),), np.float32)
    for c0 in (0, DIL_SLAB, TOK_W, TOK_W + DIL_SLAB):
        scale[c0:c0 + DIL_W] = Q_SCALE
    return np.asarray(cols), scale


def kernel(x, mem, rel_bias_table, mem_norm, norm_mix, norm_ffn, final_norm, swa_w_in, swa_sinks, swa_w_mem_kv,
           swa_w_out, dil_w_in, dil_w_mem_kv, dil_w_out, ffn_gate, ffn_up, ffn_down, router, moe_gate, moe_up,
           moe_down):
    assert x.shape == (1, SEQ, D_MODEL) and mem.shape == (1, N_MEM, D_MODEL)
    assert norm_mix.shape == (2, D_MODEL) and swa_w_in.shape == (1, D_MODEL, SWA_IN)
    assert dil_w_in.shape == (1, D_MODEL, DIL_IN) and moe_gate.shape == (1, N_EXPERTS, D_MODEL, D_FF)
    bf = lambda a: a.astype(BF16)
    h0 = x.reshape(SEQ, D_MODEL)

    kvm = _memkv(mem[0], mem_norm, bf(jnp.concatenate([swa_w_mem_kv[0], dil_w_mem_kv[0]], axis=1)))

    cols0, scale0, rows0 = _swa_weight_layout()
    p0 = _proj(h0, norm_mix[0], bf(_take_runs(swa_w_in[0], cols0, 1) * scale0), "swa_proj")
    h1, hn1 = _swa_layer(p0, kvm, rel_bias_table, swa_sinks[0], bf(_take_runs(swa_w_out[0], rows0, 0)), h0,
                         norm_ffn[0])
    h2 = _ffn(hn1, h1, bf(ffn_gate[0]), bf(ffn_up[0]), bf(ffn_down[0]))

    cols1, scale1 = _dil_weight_layout()
    tok, g1, g2 = _dilproj(h2, norm_mix[1], bf(_take_runs(dil_w_in[0], cols1, 1) * scale1))
    groups = [_dil_group(qkv, rel_bias_table, gi) for gi, qkv in enumerate((tok[None], g1, g2))]
    router_p = jnp.pad(bf(router[0]), ((0, 0), (0, LANES - N_EXPERTS)))
    h3, hn3, meta, counts = _dilout([g[0] for g in groups], [g[1] for g in groups], tok, kvm, bf(dil_w_out[0]), h2,
                                    norm_ffn[1], router_p)
    pos0, pos1, items = _moe_plan(meta, counts)
    xs = _moe_scatter(hn3, pos0, pos1)
    y = _moe_grouped(items, xs, bf(moe_gate[0]), bf(moe_up[0]), bf(moe_down[0]))
    out = _moe_combine(pos0, pos1, meta, h3, final_norm, y)
    return out.reshape(1, SEQ, D_MODEL)
```

```python
import functools
import math

import jax
import jax.numpy as jnp
import numpy as np
from jax import lax
from jax.experimental import pallas as pl
from jax.experimental.pallas import tpu as pltpu

F32 = jnp.float32
BF16 = jnp.bfloat16

D_MODEL = 1024
SEQ = 16384
HEAD_DIM = 64
N_MIX_HEADS = 12
SWA_KV_HEADS = 3
SWA_GROUP = N_MIX_HEADS // SWA_KV_HEADS
SWA_WINDOW = 128
DIL_GROUPS = ((128, 1), (512, 4), (2048, 16))
DIL_HEADS = 4
N_MEM = 256
MEM_HEADS = 4
BLOCK = 128
N_BUCKETS = 32
MAX_DISTANCE = 2048
D_FF = 3584
N_EXPERTS = 8
EPS = 1e-5
NEG = -1e30
Q_SCALE = HEAD_DIM ** -0.5

SWA_Q = N_MIX_HEADS * HEAD_DIM
SWA_KV = SWA_KV_HEADS * HEAD_DIM
SWA_IN = SWA_Q + 2 * SWA_KV + MEM_HEADS * HEAD_DIM
DIL_W = DIL_HEADS * HEAD_DIM
DIL_IN = len(DIL_GROUPS) * 3 * DIL_W + MEM_HEADS * HEAD_DIM
MEM_W = MEM_HEADS * HEAD_DIM

LANES = 128
SUBLANES = 8
VMEM_LIMIT = 56 * 1024 * 1024

ROW_TILE = 512
BLOCKS_PER_TILE = ROW_TILE // BLOCK
FFN_ROW_TILE = 1024
FFN_COL_TILE = 512


def _bucket_map(dil):
    qi = np.arange(BLOCK)[:, None]
    kj = np.arange(2 * BLOCK)[None, :]
    d = np.maximum((qi + BLOCK - kj) * dil, 0)
    max_exact = N_BUCKETS // 2
    ratio = np.maximum(d, 1).astype(np.float32) / np.float32(max_exact)
    large = max_exact + (np.log(ratio) / np.float32(math.log(MAX_DISTANCE / max_exact))
                         * np.float32(N_BUCKETS - max_exact)).astype(np.int32)
    return np.where(d < max_exact, d, np.minimum(large, N_BUCKETS - 1)).astype(np.int32)


def _rms(x, g):
    ms = jnp.mean(x * x, axis=-1, keepdims=True)
    return x * lax.rsqrt(ms + EPS) * g


def _dot_nt(a, b):
    return lax.dot_general(a, b, (((1,), (1,)), ((), ())), preferred_element_type=F32)


def _dot_tn(a, b):
    return lax.dot_general(a, b, (((0,), (0,)), ((), ())), preferred_element_type=F32)


def _fill_bias(bias_sc, tab_ref, bucket, heads):
    for j, h in enumerate(heads):
        def body(k, b, h=h):
            return jnp.where(bucket == k, tab_ref[k, h], b)
        bias_sc[:, j * BLOCK:(j + 1) * BLOCK] = lax.fori_loop(0, N_BUCKETS, body, jnp.zeros(bucket.shape, F32))


def _fill_row(row_sc, values):
    blk = lax.broadcasted_iota(jnp.int32, row_sc.shape, 1) // BLOCK
    row = jnp.zeros(row_sc.shape, F32)
    for j, v in enumerate(values):
        row = jnp.where(blk == j, v, row)
    row_sc[...] = row


def _folded_bucket_map(dil):
    full = _bucket_map(dil)
    qi = np.arange(BLOCK)[:, None]
    c = np.arange(BLOCK)[None, :]
    return np.ascontiguousarray(np.where(c <= qi, full[:, BLOCK:], full[:, :BLOCK]).T.astype(np.int32))


def _diag_bucket(dil):
    return int(_bucket_map(dil)[0, 0])


def _fold_masks(n):
    c = lax.broadcasted_iota(jnp.int32, (BLOCK, n), 0)
    qi = lax.broadcasted_iota(jnp.int32, (BLOCK, n), 1) % BLOCK
    return c <= qi, c == qi


def _pair_scores(k_slab, q_rows):
    lane = lax.broadcasted_iota(jnp.int32, q_rows.shape, 1)
    zero = jnp.zeros_like(q_rows)
    qa = jnp.where(lane < HEAD_DIM, q_rows, zero)
    qb = jnp.where(lane < HEAD_DIM, zero, q_rows)
    return _dot_nt(k_slab, qa), _dot_nt(k_slab, qb)


def _pair_values(v_slab, p_a, p_b):
    oa = _dot_tn(v_slab, p_a)
    ob = _dot_tn(v_slab, p_b)
    row = lax.broadcasted_iota(jnp.int32, oa.shape, 0)
    return jnp.where(row < HEAD_DIM, oa, ob)


def _band_softmax(st, bias, own, eye, has_prev, sink, diag_bias):
    s_prev, s_own = st[:BLOCK], st[BLOCK:]
    t = jnp.where(own | has_prev, jnp.where(own, s_own, s_prev) + bias, NEG)
    m = jnp.max(t, axis=0, keepdims=True)
    if diag_bias is not None:
        s_d = jnp.sum(jnp.where(eye, s_prev, 0.0), axis=0, keepdims=True) + diag_bias
        s_d = jnp.where(has_prev, s_d, NEG)
        m = jnp.maximum(m, s_d)
    if sink is not None:
        m = jnp.maximum(m, sink)
    e = jnp.exp(t - m)
    den = jnp.sum(e, axis=0, keepdims=True)
    if diag_bias is not None:
        e_d = jnp.exp(s_d - m)
        den = den + e_d
    if sink is not None:
        den = den + jnp.exp(sink - m)
    inv = 1.0 / den
    p = e * inv
    p_own = jnp.where(own, p, 0.0)
    p_prev = jnp.where(own, 0.0, p)
    if diag_bias is not None:
        p_prev = jnp.where(eye, e_d * inv, p_prev)
    return jnp.concatenate([p_prev, p_own], axis=0).astype(BF16), m + jnp.log(den)


def _mem_attention(qm, kvm):
    outs = []
    for s in range(MEM_W // LANES):
        sa, sb = _pair_scores(kvm[:, s * LANES:(s + 1) * LANES], qm[:, s * LANES:(s + 1) * LANES])
        ps = []
        for st in (sa, sb):
            e = jnp.exp(st - jnp.max(st, axis=0, keepdims=True))
            ps.append((e * (1.0 / jnp.sum(e, axis=0, keepdims=True))).astype(BF16))
        outs.append(_pair_values(kvm[:, MEM_W + s * LANES:MEM_W + (s + 1) * LANES], ps[0], ps[1]))
    return outs


def _to_rows(slabs_t):
    return jnp.concatenate([jnp.transpose(x).astype(BF16) for x in slabs_t], axis=-1)


def _memkv_kernel(mem_ref, g_ref, w_ref, o_ref):
    mn = _rms(mem_ref[...], g_ref[...]).astype(BF16)
    o_ref[...] = jnp.dot(mn, w_ref[...], preferred_element_type=F32).astype(BF16)


def _memkv(mem, g, w):
    return pl.pallas_call(
        _memkv_kernel,
        out_shape=jax.ShapeDtypeStruct((N_MEM, w.shape[1]), BF16),
        name="mem_kv",
    )(mem, g.reshape(1, D_MODEL), w)


def _proj_kernel(h_ref, g_ref, w_ref, o_ref):
    hn = _rms(h_ref[...], g_ref[...]).astype(BF16)
    o_ref[...] = jnp.dot(hn, w_ref[...], preferred_element_type=F32).astype(BF16)


def _proj(h, g, w, name):
    n = w.shape[1]
    return pl.pallas_call(
        _proj_kernel,
        grid=(SEQ // ROW_TILE,),
        in_specs=[pl.BlockSpec((ROW_TILE, D_MODEL), lambda i: (i, 0)),
                  pl.BlockSpec((1, D_MODEL), lambda i: (0, 0)),
                  pl.BlockSpec((D_MODEL, n), lambda i: (0, 0))],
        out_specs=pl.BlockSpec((ROW_TILE, n), lambda i: (i, 0)),
        out_shape=jax.ShapeDtypeStruct((SEQ, n), BF16),
        compiler_params=pltpu.CompilerParams(dimension_semantics=("arbitrary",), vmem_limit_bytes=VMEM_LIMIT),
        name=name,
    )(h, g.reshape(1, D_MODEL), w)


SWA_PAIR_SLABS = [(g, SWA_GROUP + g) for g in range(SWA_GROUP)] + [(8, 9), (10, 11)]
SWA_K0 = SWA_Q
SWA_V0 = SWA_K0 + 2 * LANES
SWA_M0 = SWA_V0 + 2 * LANES
SWA_PW = SWA_M0 + MEM_W
SWA_SCORE_HEADS = [0, 1, 2, 3, 4, 5, 6, 7, 8, 10, 9, 11]
SWA_NQ = N_MIX_HEADS * BLOCK


def _cast_slice_spec(w, steps):
    e, r, c = w.shape
    per_e = steps // e
    assert steps % e == 0 and r % per_e == 0 and (r // per_e) % 16 == 0
    return pl.BlockSpec((1, r // per_e, c), lambda i: (i // per_e, i % per_e, 0))


def _swa_kernel(tab_ref, sink_ref, bucket_ref, p_ref, kprev_ref, vprev_ref, kvm_ref, wout_ref, h_ref, g_ref,
                wa_ref, wb_ref, h1_ref, hn1_ref, wa_out, wb_out, bias_sc, sink_sc, k_sc, v_sc, cat_sc):
    i = pl.program_id(0)
    wa_out[...] = wa_ref[...].astype(BF16)
    wb_out[...] = wb_ref[...].astype(BF16)

    @pl.when(i == 0)
    def _():
        _fill_bias(bias_sc, tab_ref, bucket_ref[...], SWA_SCORE_HEADS)
        _fill_row(sink_sc, [sink_ref[h] for h in SWA_SCORE_HEADS])

    k_sc[0:BLOCK, :] = kprev_ref[...]
    k_sc[BLOCK:, :] = p_ref[:, SWA_K0:SWA_V0]
    v_sc[0:BLOCK, :] = vprev_ref[...]
    v_sc[BLOCK:, :] = p_ref[:, SWA_V0:SWA_M0]
    own, eye = _fold_masks(SWA_NQ)

    def block_body(b, carry):
        r0 = pl.multiple_of(b * BLOCK, BLOCK)
        has_prev = i * BLOCKS_PER_TILE + b > 0
        qb = p_ref[pl.ds(r0, BLOCK), 0:SWA_Q]
        slab = lambda x, s: x[:, s * LANES:(s + 1) * LANES]
        qa = jnp.concatenate([slab(qb, s) for s in range(4)], axis=0)
        qc = jnp.concatenate([slab(qb, s) for s in (4, 5)], axis=0)
        kb = k_sc[pl.ds(r0, 2 * BLOCK), :]
        vb = v_sc[pl.ds(r0, 2 * BLOCK), :]
        s0, s1 = _pair_scores(slab(kb, 0), qa)
        s2, s3 = _pair_scores(slab(kb, 1), qc)
        st = jnp.concatenate([s0, s1, s2, s3], axis=1)
        pt, _ = _band_softmax(st, bias_sc[...], own, eye, has_prev, sink_sc[...], None)
        na, nc = 4 * BLOCK, 2 * BLOCK
        oa = _pair_values(slab(vb, 0), pt[:, 0:na], pt[:, na:2 * na])
        oc = _pair_values(slab(vb, 1), pt[:, 2 * na:2 * na + nc], pt[:, 2 * na + nc:])
        outs = [slab(oa, s) for s in range(4)] + [slab(oc, s) for s in range(2)]
        outs += _mem_attention(p_ref[pl.ds(r0, BLOCK), SWA_M0:], kvm_ref[...])
        cat_sc[pl.ds(r0, BLOCK), :] = _to_rows(outs)
        return carry

    lax.fori_loop(0, BLOCKS_PER_TILE, block_body, 0, unroll=True)

    out = h_ref[...] + jnp.dot(cat_sc[...], wout_ref[...], preferred_element_type=F32)
    h1_ref[...] = out
    hn1_ref[...] = _rms(out, g_ref[...]).astype(BF16)


def _swa_layer(p, kvm, table, sinks, wout, h, g_ffn, wa, wb):
    assert SWA_WINDOW == BLOCK
    kv_w = 2 * LANES
    prev = lambda i: jnp.maximum(i * BLOCKS_PER_TILE - 1, 0)
    steps = SEQ // ROW_TILE
    return pl.pallas_call(
        _swa_kernel,
        grid=(SEQ // ROW_TILE,),
        in_specs=[
            pl.BlockSpec(memory_space=pltpu.SMEM),
            pl.BlockSpec(memory_space=pltpu.SMEM),
            pl.BlockSpec((BLOCK, BLOCK), lambda i: (0, 0)),
            pl.BlockSpec((ROW_TILE, SWA_PW), lambda i: (i, 0)),
            pl.BlockSpec((BLOCK, kv_w), lambda i: (prev(i), SWA_K0 // kv_w)),
            pl.BlockSpec((BLOCK, kv_w), lambda i: (prev(i), SWA_V0 // kv_w)),
            pl.BlockSpec((N_MEM, 2 * MEM_W), lambda i: (0, 0)),
            pl.BlockSpec((D_MODEL, D_MODEL), lambda i: (0, 0)),
            pl.BlockSpec((ROW_TILE, D_MODEL), lambda i: (i, 0)),
            pl.BlockSpec((1, D_MODEL), lambda i: (0, 0)),
            _cast_slice_spec(wa, steps), _cast_slice_spec(wb, steps),
        ],
        out_specs=[pl.BlockSpec((ROW_TILE, D_MODEL), lambda i: (i, 0)),
                   pl.BlockSpec((ROW_TILE, D_MODEL), lambda i: (i, 0)),
                   _cast_slice_spec(wa, steps), _cast_slice_spec(wb, steps)],
        out_shape=[jax.ShapeDtypeStruct((SEQ, D_MODEL), F32), jax.ShapeDtypeStruct((SEQ, D_MODEL), BF16),
                   jax.ShapeDtypeStruct(wa.shape, BF16), jax.ShapeDtypeStruct(wb.shape, BF16)],
        scratch_shapes=[pltpu.VMEM((BLOCK, SWA_NQ), F32),
                        pltpu.VMEM((1, SWA_NQ), F32),
                        pltpu.VMEM((ROW_TILE + BLOCK, kv_w), BF16),
                        pltpu.VMEM((ROW_TILE + BLOCK, kv_w), BF16),
                        pltpu.VMEM((ROW_TILE, D_MODEL), BF16)],
        compiler_params=pltpu.CompilerParams(dimension_semantics=("arbitrary",), vmem_limit_bytes=VMEM_LIMIT),
        name="swa_mixer",
    )(table, sinks, jnp.asarray(_folded_bucket_map(1)), p, p, p, kvm, wout, h, g_ffn.reshape(1, D_MODEL), wa, wb)


DIL_SLAB = 3 * DIL_W
TOK_W = DIL_SLAB + MEM_W
SLABS_PER_GROUP = DIL_SLAB // LANES


def _dilproj_kernel(h_ref, g_ref, w_ref, tok_ref, g1_ref, g2_ref, slab_sc):
    hn = _rms(h_ref[...], g_ref[...]).astype(BF16)
    res = jnp.dot(hn, w_ref[...], preferred_element_type=F32)
    tok_ref[...] = res[:, :TOK_W].astype(BF16)
    for s in range(2 * SLABS_PER_GROUP):
        slab_sc[s] = res[:, TOK_W + s * LANES:TOK_W + (s + 1) * LANES]
    for gi, out_ref in ((1, g1_ref), (2, g2_ref)):
        d = DIL_GROUPS[gi][1]
        for s in range(SLABS_PER_GROUP):
            for r in range(d):
                rows = slab_sc[(gi - 1) * SLABS_PER_GROUP + s, pl.ds(r, ROW_TILE // d, stride=d), :]
                out_ref[r, :, s * LANES:(s + 1) * LANES] = rows.astype(BF16)


def _dilproj(h, g, w):
    d1, d2 = DIL_GROUPS[1][1], DIL_GROUPS[2][1]
    return pl.pallas_call(
        _dilproj_kernel,
        grid=(SEQ // ROW_TILE,),
        in_specs=[pl.BlockSpec((ROW_TILE, D_MODEL), lambda i: (i, 0)),
                  pl.BlockSpec((1, D_MODEL), lambda i: (0, 0)),
                  pl.BlockSpec((D_MODEL, DIL_IN), lambda i: (0, 0))],
        out_specs=[pl.BlockSpec((ROW_TILE, TOK_W), lambda i: (i, 0)),
                   pl.BlockSpec((d1, ROW_TILE // d1, DIL_SLAB), lambda i: (0, i, 0)),
                   pl.BlockSpec((d2, ROW_TILE // d2, DIL_SLAB), lambda i: (0, i, 0))],
        out_shape=[jax.ShapeDtypeStruct((SEQ, TOK_W), BF16),
                   jax.ShapeDtypeStruct((d1, SEQ // d1, DIL_SLAB), BF16),
                   jax.ShapeDtypeStruct((d2, SEQ // d2, DIL_SLAB), BF16)],
        scratch_shapes=[pltpu.VMEM((2 * SLABS_PER_GROUP, ROW_TILE, LANES), F32)],
        compiler_params=pltpu.CompilerParams(dimension_semantics=("arbitrary",), vmem_limit_bytes=VMEM_LIMIT),
        name="dil_proj",
    )(h, g.reshape(1, D_MODEL), w)


def _dil_kernel(tab_ref, bucket_ref, q_ref, k_ref, v_ref, kp_ref, vp_ref, o_ref, l_ref,
                bias_sc, diag_sc, k_sc, v_sc, *, head0, diag_bucket):
    r = pl.program_id(0)
    n = pl.program_id(1)
    heads = [head0 + h for h in range(DIL_HEADS)]
    nq = DIL_HEADS * BLOCK

    @pl.when((r == 0) & (n == 0))
    def _():
        _fill_bias(bias_sc, tab_ref, bucket_ref[...], heads)
        _fill_row(diag_sc, [tab_ref[diag_bucket, h] for h in heads])

    k_sc[0:BLOCK, :] = kp_ref[...]
    k_sc[BLOCK:, :] = k_ref[...]
    v_sc[0:BLOCK, :] = vp_ref[...]
    v_sc[BLOCK:, :] = v_ref[...]
    own, eye = _fold_masks(nq)
    upper_rows = lax.broadcasted_iota(jnp.int32, (BLOCK, BLOCK), 0) < HEAD_DIM

    def block_body(b, carry):
        r0 = pl.multiple_of(b * BLOCK, BLOCK)
        has_prev = n * BLOCKS_PER_TILE + b > 0
        qb = q_ref[pl.ds(r0, BLOCK), :]
        kb = k_sc[pl.ds(r0, 2 * BLOCK), :]
        vb = v_sc[pl.ds(r0, 2 * BLOCK), :]
        slab = lambda x, s: x[:, s * LANES:(s + 1) * LANES]
        scores = []
        for s in range(DIL_W // LANES):
            scores += _pair_scores(slab(kb, s), slab(qb, s))
        pt, lse = _band_softmax(jnp.concatenate(scores, axis=1), bias_sc[...], own, eye, has_prev, None,
                                diag_sc[...])
        outs, lses = [], []
        for s in range(DIL_W // LANES):
            ca, cb = 2 * s * BLOCK, (2 * s + 1) * BLOCK
            outs.append(jnp.transpose(_pair_values(slab(vb, s), pt[:, ca:ca + BLOCK], pt[:, cb:cb + BLOCK])))
            lse_t = jnp.where(upper_rows, jnp.broadcast_to(lse[:, ca:ca + BLOCK], (BLOCK, BLOCK)),
                              jnp.broadcast_to(lse[:, cb:cb + BLOCK], (BLOCK, BLOCK)))
            lses.append(jnp.transpose(lse_t))
        o_ref[pl.ds(r0, BLOCK), :] = jnp.concatenate(outs, axis=-1)
        l_ref[pl.ds(r0, BLOCK), :] = jnp.concatenate(lses, axis=-1)
        return carry

    lax.fori_loop(0, BLOCKS_PER_TILE, block_body, 0, unroll=True)


def _dil_group(qkv, table, gi):
    window, d = DIL_GROUPS[gi]
    rows = SEQ // d
    assert qkv.shape[:2] == (d, rows) and window // d == BLOCK
    prev = lambda n: jnp.maximum(n * BLOCKS_PER_TILE - 1, 0)
    tile = lambda c: pl.BlockSpec((None, ROW_TILE, DIL_W), lambda r, n: (r, n, c))
    prev_block = lambda c: pl.BlockSpec((None, BLOCK, DIL_W), lambda r, n: (r, prev(n), c))
    return pl.pallas_call(
        functools.partial(_dil_kernel, head0=gi * DIL_HEADS, diag_bucket=_diag_bucket(d)),
        grid=(d, rows // ROW_TILE),
        in_specs=[
            pl.BlockSpec(memory_space=pltpu.SMEM),
            pl.BlockSpec((BLOCK, BLOCK), lambda r, n: (0, 0)),
            tile(0), tile(1), tile(2), prev_block(1), prev_block(2),
        ],
        out_specs=[tile(0), tile(0)],
        out_shape=[jax.ShapeDtypeStruct((d, rows, DIL_W), F32), jax.ShapeDtypeStruct((d, rows, DIL_W), F32)],
        scratch_shapes=[pltpu.VMEM((BLOCK, DIL_HEADS * BLOCK), F32),
                        pltpu.VMEM((1, DIL_HEADS * BLOCK), F32),
                        pltpu.VMEM((ROW_TILE + BLOCK, DIL_W), BF16),
                        pltpu.VMEM((ROW_TILE + BLOCK, DIL_W), BF16)],
        compiler_params=pltpu.CompilerParams(dimension_semantics=("arbitrary", "arbitrary"),
                                             vmem_limit_bytes=VMEM_LIMIT),
        name=f"dil_attn_{gi}",
    )(table, jnp.asarray(_folded_bucket_map(d)), qkv, qkv, qkv, qkv, qkv)


def _dilout_kernel(o0_ref, o1_ref, o2_ref, l0_ref, l1_ref, l2_ref, qm_ref, kvm_ref, wout_ref, h_ref, g_ref,
                   router_ref, h2_ref, hn2_ref, meta_ref, cnt_ref, metat_ref, carry_sc, tok_sc):
    @pl.when(pl.program_id(0) == 0)
    def _():
        carry_sc[...] = jnp.zeros_like(carry_sc)

    for k, src_ref in enumerate((o1_ref, l1_ref, o2_ref, l2_ref)):
        d = src_ref.shape[0]
        for s in range(DIL_W // LANES):
            for r in range(d):
                tok_sc[k, s, pl.ds(r, ROW_TILE // d, stride=d), :] = src_ref[r, :, s * LANES:(s + 1) * LANES]

    mixed = []
    for s in range(DIL_W // LANES):
        cols = slice(s * LANES, (s + 1) * LANES)
        o0, o1, o2 = o0_ref[:, cols], tok_sc[0, s], tok_sc[2, s]
        l0, l1, l2 = l0_ref[:, cols], tok_sc[1, s], tok_sc[3, s]
        mx = jnp.maximum(jnp.maximum(l0, l1), l2)
        e0, e1, e2 = jnp.exp(l0 - mx), jnp.exp(l1 - mx), jnp.exp(l2 - mx)
        inv = 1.0 / (e0 + e1 + e2)
        mixed.append(((e0 * inv) * o0 + (e1 * inv) * o1 + (e2 * inv) * o2).astype(BF16))
    cat = jnp.concatenate(mixed + [_to_rows(_mem_attention(qm_ref[...], kvm_ref[...]))], axis=-1)
    out = h_ref[...] + jnp.dot(cat, wout_ref[...], preferred_element_type=F32)
    h2_ref[...] = out
    hn = _rms(out, g_ref[...]).astype(BF16)
    hn2_ref[...] = hn

    logits = jnp.dot(hn, router_ref[...], preferred_element_type=F32)
    lane = lax.broadcasted_iota(jnp.int32, logits.shape, 1)
    masked = jnp.where(lane < N_EXPERTS, logits, -jnp.inf)
    v0 = jnp.max(masked, axis=-1, keepdims=True)
    i0 = jnp.min(jnp.where(masked == v0, lane, LANES), axis=-1, keepdims=True)
    rest = jnp.where(lane == i0, -jnp.inf, masked)
    v1 = jnp.max(rest, axis=-1, keepdims=True)
    i1 = jnp.min(jnp.where(rest == v1, lane, LANES), axis=-1, keepdims=True)
    ex = jnp.exp(v1 - v0)
    inv2 = 1.0 / (1.0 + ex)

    oh0, oh1 = lane == i0, lane == i1
    sel = (oh0 | oh1).astype(F32)
    tok_r = lax.broadcasted_iota(jnp.int32, (ROW_TILE, ROW_TILE), 0)
    tok_c = lax.broadcasted_iota(jnp.int32, (ROW_TILE, ROW_TILE), 1)
    earlier = (tok_c < tok_r).astype(F32).astype(BF16)
    before = jnp.dot(earlier, sel.astype(BF16), preferred_element_type=F32) + carry_sc[...]
    rank0 = jnp.sum(jnp.where(oh0, before, 0.0), axis=-1, keepdims=True)
    rank1 = jnp.sum(jnp.where(oh1, before, 0.0), axis=-1, keepdims=True)
    count = carry_sc[...] + jnp.sum(sel, axis=0, keepdims=True)
    carry_sc[...] = count
    cnt_ref[...] = jnp.broadcast_to(count, cnt_ref.shape)

    fields = (i0.astype(F32), i1.astype(F32), inv2, ex * inv2, rank0, rank1)
    meta = jnp.zeros(logits.shape, F32)
    for k, f in enumerate(fields):
        meta = jnp.where(lane == k, f, meta)
    meta_ref[...] = meta
    metat_ref[...] = jnp.transpose(meta)[:SUBLANES]


META_E0, META_E1, META_G0, META_G1, META_RANK0, META_RANK1 = range(6)


def _dilout(os, ls, tok, kvm, wout, h, g_ffn, router):
    row = lambda i: (i, 0)
    const = lambda i: (0, 0)

    def group_spec(a):
        d = a.shape[0]
        if d == 1:
            return pl.BlockSpec((None, ROW_TILE, DIL_W), lambda i: (0, i, 0))
        return pl.BlockSpec((d, ROW_TILE // d, DIL_W), lambda i: (0, i, 0))

    return pl.pallas_call(
        _dilout_kernel,
        grid=(SEQ // ROW_TILE,),
        in_specs=[group_spec(a) for a in (*os, *ls)] + [
                  pl.BlockSpec((ROW_TILE, MEM_W), lambda i: (i, DIL_SLAB // MEM_W)),
                  pl.BlockSpec((N_MEM, 2 * MEM_W), lambda i: (0, 1)),
                  pl.BlockSpec((DIL_W + MEM_W, D_MODEL), const),
                  pl.BlockSpec((ROW_TILE, D_MODEL), row),
                  pl.BlockSpec((1, D_MODEL), const),
                  pl.BlockSpec((D_MODEL, LANES), const)],
        out_specs=[pl.BlockSpec((ROW_TILE, D_MODEL), row),
                   pl.BlockSpec((ROW_TILE, D_MODEL), row),
                   pl.BlockSpec((ROW_TILE, LANES), row),
                   pl.BlockSpec((SUBLANES, LANES), const),
                   pl.BlockSpec((SUBLANES, ROW_TILE), lambda i: (0, i))],
        out_shape=[jax.ShapeDtypeStruct((SEQ, D_MODEL), F32), jax.ShapeDtypeStruct((SEQ, D_MODEL), BF16),
                   jax.ShapeDtypeStruct((SEQ, LANES), F32), jax.ShapeDtypeStruct((SUBLANES, LANES), F32),
                   jax.ShapeDtypeStruct((SUBLANES, SEQ), F32)],
        scratch_shapes=[pltpu.VMEM((1, LANES), F32),
                        pltpu.VMEM((4, DIL_W // LANES, ROW_TILE, LANES), F32)],
        compiler_params=pltpu.CompilerParams(dimension_semantics=("arbitrary",), vmem_limit_bytes=VMEM_LIMIT),
        name="dil_out",
    )(*os, *ls, tok, kvm, wout, h, g_ffn.reshape(1, D_MODEL), router)


def _swiglu_partial(x, wg, wu, wd):
    a = jnp.dot(x, wg, preferred_element_type=F32)
    b = jnp.dot(x, wu, preferred_element_type=F32)
    hm = (a * jax.nn.sigmoid(a) * b).astype(BF16)
    return jnp.dot(hm, wd, preferred_element_type=F32)


def _ffn_kernel(x_ref, wg_ref, wu_ref, wd_ref, h_ref, wa_ref, o_ref, wa_out):
    wa_out[...] = wa_ref[...].astype(BF16)
    x = x_ref[...]
    acc = h_ref[...]
    for c in range(D_FF // FFN_COL_TILE):
        cols = slice(c * FFN_COL_TILE, (c + 1) * FFN_COL_TILE)
        acc = acc + _swiglu_partial(x, wg_ref[:, cols], wu_ref[:, cols], wd_ref[cols, :])
    o_ref[...] = acc


def _ffn(x, h, wg, wu, wd, wa):
    tm = ROW_TILE
    steps = SEQ // tm
    resident = lambda shape: pl.BlockSpec(shape, lambda i: (0, 0), pipeline_mode=pl.Buffered(1))
    return pl.pallas_call(
        _ffn_kernel,
        grid=(steps,),
        in_specs=[pl.BlockSpec((tm, D_MODEL), lambda i: (i, 0)),
                  resident((D_MODEL, D_FF)), resident((D_MODEL, D_FF)), resident((D_FF, D_MODEL)),
                  pl.BlockSpec((tm, D_MODEL), lambda i: (i, 0)),
                  _cast_slice_spec(wa, steps)],
        out_specs=[pl.BlockSpec((tm, D_MODEL), lambda i: (i, 0)), _cast_slice_spec(wa, steps)],
        out_shape=[jax.ShapeDtypeStruct((SEQ, D_MODEL), F32), jax.ShapeDtypeStruct(wa.shape, BF16)],
        compiler_params=pltpu.CompilerParams(dimension_semantics=("arbitrary",), vmem_limit_bytes=VMEM_LIMIT),
        name="ffn",
    )(x, wg, wu, wd, h, wa)


MOE_SUB = 512
MOE_ROWS = 1024
MOE_COL_TILE = D_FF // 2
MOE_SUBS = MOE_ROWS // MOE_SUB
MOE_SLOTS = 2 * SEQ + N_EXPERTS * MOE_SUB
MOE_TILES = MOE_SLOTS // MOE_ROWS
MOE_ITEMS = MOE_TILES + N_EXPERTS - 1
PACK_W = D_MODEL // 2
ITEM_TILE, ITEM_EXPERT, ITEM_LO, ITEM_HI, ITEM_FIRST = range(5)
DMA_UNROLL = 8


def _pack_bf16_pairs(x):
    lo = lax.bitcast_convert_type(x[:, :PACK_W].astype(F32), jnp.uint32) >> 16
    hi = lax.bitcast_convert_type(x[:, PACK_W:].astype(F32), jnp.uint32) & jnp.uint32(0xFFFF0000)
    return hi | lo


def _unpack_bf16_pairs(pk):
    lo = lax.bitcast_convert_type(pk << 16, F32).astype(BF16)
    hi = lax.bitcast_convert_type(pk & jnp.uint32(0xFFFF0000), F32).astype(BF16)
    return lo, hi


def _row_copy(src, src_row, dst, dst_row, sem):
    return pltpu.make_async_copy(src.at[pl.ds(src_row, 1), :], dst.at[pl.ds(dst_row, 1), :], sem)


def _scatter_kernel(pos0_ref, pos1_ref, x_ref, xs_in_ref, xs_ref, pk_sc, sem):
    del xs_in_ref
    pk_sc[...] = _pack_bf16_pairs(x_ref[...])

    def issue(g, c):
        for u in range(DMA_UNROLL):
            t = g * DMA_UNROLL + u
            _row_copy(pk_sc, t, xs_ref, pos0_ref[t], sem).start(priority=0)
            _row_copy(pk_sc, t, xs_ref, pos1_ref[t], sem).start(priority=1)
        return c

    def drain(g, c):
        for _ in range(2 * DMA_UNROLL):
            _row_copy(pk_sc, 0, xs_ref, 0, sem).wait()
        return c

    lax.fori_loop(0, ROW_TILE // DMA_UNROLL, issue, 0)
    lax.fori_loop(0, ROW_TILE // DMA_UNROLL, drain, 0)


def _moe_scatter(x, pos0, pos1):
    smem_tile = pl.BlockSpec((ROW_TILE,), lambda i: (i,), memory_space=pltpu.SMEM)
    return pl.pallas_call(
        _scatter_kernel,
        grid=(SEQ // ROW_TILE,),
        in_specs=[smem_tile, smem_tile,
                  pl.BlockSpec((ROW_TILE, D_MODEL), lambda i: (i, 0)),
                  pl.BlockSpec(memory_space=pl.ANY)],
        out_specs=pl.BlockSpec(memory_space=pl.ANY),
        out_shape=jax.ShapeDtypeStruct((MOE_SLOTS, PACK_W), jnp.uint32),
        scratch_shapes=[pltpu.VMEM((ROW_TILE, PACK_W), jnp.uint32), pltpu.SemaphoreType.DMA],
        input_output_aliases={3: 0},
        compiler_params=pltpu.CompilerParams(dimension_semantics=("arbitrary",), vmem_limit_bytes=VMEM_LIMIT),
        name="moe_scatter",
    )(pos0, pos1, x, jnp.zeros((MOE_SLOTS, PACK_W), jnp.uint32))


def _moe_kernel(items_ref, xs_ref, wg_ref, wu_ref, wd_ref, y_ref, xb_sc):
    w = pl.program_id(0)
    j = pl.program_id(1)
    lo = items_ref[ITEM_LO, w]
    hi = items_ref[ITEM_HI, w]

    @pl.when((j == 0) & (hi > lo))
    def _():
        xlo, xhi = _unpack_bf16_pairs(xs_ref[...])
        xb_sc[:, :PACK_W] = xlo
        xb_sc[:, PACK_W:] = xhi

    @pl.when((j == 0) & (items_ref[ITEM_FIRST, w] == 1))
    def _():
        y_ref[...] = jnp.zeros_like(y_ref)

    for sb in range(MOE_SUBS):
        @pl.when((lo <= sb) & (sb < hi))
        def _(sb=sb):
            rows = slice(sb * MOE_SUB, (sb + 1) * MOE_SUB)
            x = xb_sc[rows, :]
            acc = y_ref[rows, :]
            for c0 in range(0, MOE_COL_TILE, FFN_COL_TILE):
                cols = slice(c0, min(c0 + FFN_COL_TILE, MOE_COL_TILE))
                acc = acc + _swiglu_partial(x, wg_ref[0, :, cols], wu_ref[0, :, cols], wd_ref[0, cols, :])
            y_ref[rows, :] = acc


def _moe_grouped(items, xs, wg, wu, wd):
    tf = MOE_COL_TILE
    nf = D_FF // tf
    jf = lambda w, j, it: jnp.where(it[ITEM_HI, w] > it[ITEM_LO, w], j, nf - 1)
    grid_spec = pltpu.PrefetchScalarGridSpec(
        num_scalar_prefetch=1,
        grid=(MOE_ITEMS, nf),
        in_specs=[pl.BlockSpec((MOE_ROWS, PACK_W), lambda w, j, it: (it[ITEM_TILE, w], 0)),
                  pl.BlockSpec((1, D_MODEL, tf), lambda w, j, it: (it[ITEM_EXPERT, w], 0, jf(w, j, it))),
                  pl.BlockSpec((1, D_MODEL, tf), lambda w, j, it: (it[ITEM_EXPERT, w], 0, jf(w, j, it))),
                  pl.BlockSpec((1, tf, D_MODEL), lambda w, j, it: (it[ITEM_EXPERT, w], jf(w, j, it), 0))],
        out_specs=pl.BlockSpec((MOE_ROWS, D_MODEL), lambda w, j, it: (it[ITEM_TILE, w], 0)),
        scratch_shapes=[pltpu.VMEM((MOE_ROWS, D_MODEL), BF16)],
    )
    return pl.pallas_call(
        _moe_kernel,
        grid_spec=grid_spec,
        out_shape=jax.ShapeDtypeStruct((MOE_SLOTS, D_MODEL), F32),
        compiler_params=pltpu.CompilerParams(dimension_semantics=("arbitrary", "arbitrary"),
                                             vmem_limit_bytes=VMEM_LIMIT),
        name="moe_grouped",
    )(items, xs, wg, wu, wd)


def _combine_kernel(pos0_ref, pos1_ref, meta_ref, h_ref, gf_ref, y_ref, o_ref, y0_sc, y1_sc, sem):
    def issue(g, c):
        for u in range(DMA_UNROLL):
            t = g * DMA_UNROLL + u
            _row_copy(y_ref, pos0_ref[t], y0_sc, t, sem).start(priority=0)
            _row_copy(y_ref, pos1_ref[t], y1_sc, t, sem).start(priority=1)
        return c

    def drain(g, c):
        for _ in range(DMA_UNROLL):
            _row_copy(y_ref, 0, y0_sc, 0, sem).wait()
            _row_copy(y_ref, 0, y1_sc, 0, sem).wait()
        return c

    lax.fori_loop(0, ROW_TILE // DMA_UNROLL, issue, 0)
    lax.fori_loop(0, ROW_TILE // DMA_UNROLL, drain, 0)
    meta = meta_ref[...]
    g0 = meta[:, META_G0:META_G0 + 1]
    g1 = meta[:, META_G1:META_G1 + 1]
    o_ref[...] = _rms(h_ref[...] + (g0 * y0_sc[...] + g1 * y1_sc[...]), gf_ref[...])


def _moe_combine(pos0, pos1, meta, h, g_final, y):
    smem_tile = pl.BlockSpec((ROW_TILE,), lambda i: (i,), memory_space=pltpu.SMEM)
    return pl.pallas_call(
        _combine_kernel,
        grid=(SEQ // ROW_TILE,),
        in_specs=[smem_tile, smem_tile,
                  pl.BlockSpec((ROW_TILE, LANES), lambda i: (i, 0)),
                  pl.BlockSpec((ROW_TILE, D_MODEL), lambda i: (i, 0)),
                  pl.BlockSpec((1, D_MODEL), lambda i: (0, 0)),
                  pl.BlockSpec(memory_space=pl.ANY)],
        out_specs=pl.BlockSpec((ROW_TILE, D_MODEL), lambda i: (i, 0)),
        out_shape=jax.ShapeDtypeStruct((SEQ, D_MODEL), F32),
        scratch_shapes=[pltpu.VMEM((ROW_TILE, D_MODEL), F32), pltpu.VMEM((ROW_TILE, D_MODEL), F32),
                        pltpu.SemaphoreType.DMA],
        compiler_params=pltpu.CompilerParams(dimension_semantics=("arbitrary",), vmem_limit_bytes=VMEM_LIMIT),
        name="moe_combine",
    )(pos0, pos1, meta, h, g_final.reshape(1, D_MODEL), y)


def _moe_plan(meta_t, counts):
    i32 = jnp.int32
    cnt = counts[0, :N_EXPERTS].astype(i32)
    padded = (cnt + MOE_SUB - 1) // MOE_SUB * MOE_SUB
    start = jnp.cumsum(padded) - padded

    def slot(e_row, rank_row):
        e = meta_t[e_row].astype(i32)
        base = sum(jnp.where(e == k, start[k], 0) for k in range(N_EXPERTS))
        return base + meta_t[rank_row].astype(i32)

    pos0, pos1 = slot(META_E0, META_RANK0), slot(META_E1, META_RANK1)

    first_sb, end_sb = start // MOE_SUB, (start + padded) // MOE_SUB
    t0 = jnp.arange(MOE_TILES, dtype=i32)[:, None] * MOE_SUBS
    lo = jnp.clip(first_sb[None, :] - t0, 0, MOE_SUBS)
    hi = jnp.clip(end_sb[None, :] - t0, 0, MOE_SUBS)
    active = hi > lo
    none = ~jnp.any(active, axis=1, keepdims=True)
    active = jnp.concatenate([active, none], axis=1)
    lo = jnp.concatenate([lo, jnp.zeros_like(t0)], axis=1)
    hi = jnp.concatenate([hi, jnp.zeros_like(t0)], axis=1)
    first = active & (jnp.cumsum(active.astype(i32), axis=1) == 1)
    ncol = N_EXPERTS + 1
    flat = jnp.arange(MOE_TILES * ncol, dtype=i32)
    key = jnp.where(active.reshape(-1), flat, MOE_TILES * ncol)
    order = jnp.argsort(key)[:MOE_ITEMS]
    live = key[order] < MOE_TILES * ncol
    tile = jnp.where(live, order // ncol, MOE_TILES - 1)
    lo_i = jnp.where(live, lo.reshape(-1)[order], 0)
    hi_i = jnp.where(live, hi.reshape(-1)[order], 0)
    first_i = jnp.where(live, first.reshape(-1)[order], False).astype(i32)
    idx = jnp.arange(MOE_ITEMS, dtype=i32)
    src = jnp.maximum(lax.cummax(jnp.where(hi_i > lo_i, idx, -1), axis=0), 0)
    expert = jnp.minimum(order % ncol, N_EXPERTS - 1)[src]
    items = jnp.stack([tile, expert, lo_i, hi_i, first_i]).astype(i32)
    return pos0, pos1, items


def _take_runs(w, idx, axis):
    idx = list(idx)
    runs, start = [], 0
    for i in range(1, len(idx) + 1):
        if i == len(idx) or idx[i] != idx[i - 1] + 1:
            runs.append(lax.slice_in_dim(w, idx[start], idx[i - 1] + 1, axis=axis))
            start = i
    return jnp.concatenate(runs, axis=axis)


def _swa_weight_layout():
    head = lambda base, j: list(range(base + j * HEAD_DIM, base + (j + 1) * HEAD_DIM))
    k0, v0, m0 = SWA_Q, SWA_Q + SWA_KV, SWA_Q + 2 * SWA_KV
    cols, rows = [], []
    for a, b in SWA_PAIR_SLABS:
        cols += head(0, a) + head(0, b)
        rows += head(0, a) + head(0, b)
    for base in (k0, v0):
        cols += head(base, 0) + head(base, 1) + head(base, 2) + head(base, 2)
    cols += list(range(m0, m0 + MEM_W))
    rows += list(range(SWA_Q, SWA_Q + MEM_W))
    scale = np.ones((len(cols),), np.float32)
    scale[:SWA_Q] = Q_SCALE
    scale[SWA_M0:] = Q_SCALE
    assert len(cols) == SWA_PW and len(rows) == D_MODEL
    return np.asarray(cols), scale, np.asarray(rows)


def _dil_weight_layout():
    n_grp = len(DIL_GROUPS) * DIL_SLAB
    cols = list(range(DIL_SLAB)) + list(range(n_grp, n_grp + MEM_W)) + list(range(DIL_SLAB, n_grp))
    scale = np.ones((len(cols),), np.float32)
    for c0 in (0, DIL_SLAB, TOK_W, TOK_W + DIL_SLAB):
        scale[c0:c0 + DIL_W] = Q_SCALE
    return np.asarray(cols), scale


def kernel(x, mem, rel_bias_table, mem_norm, norm_mix, norm_ffn, final_norm, swa_w_in, swa_sinks, swa_w_mem_kv,
           swa_w_out, dil_w_in, dil_w_mem_kv, dil_w_out, ffn_gate, ffn_up, ffn_down, router, moe_gate, moe_up,
           moe_down):
    assert x.shape == (1, SEQ, D_MODEL) and mem.shape == (1, N_MEM, D_MODEL)
    assert norm_mix.shape == (2, D_MODEL) and swa_w_in.shape == (1, D_MODEL, SWA_IN)
    assert dil_w_in.shape == (1, D_MODEL, DIL_IN) and moe_gate.shape == (1, N_EXPERTS, D_MODEL, D_FF)
    bf = lambda a: a.astype(BF16)
    h0 = x.reshape(SEQ, D_MODEL)

    kvm = _memkv(mem[0], mem_norm, bf(jnp.concatenate([swa_w_mem_kv[0], dil_w_mem_kv[0]], axis=1)))

    cols0, scale0, rows0 = _swa_weight_layout()
    p0 = _proj(h0, norm_mix[0], bf(_take_runs(swa_w_in[0], cols0, 1) * scale0), "swa_proj")
    h1, hn1, moe_up_b, moe_down_b = _swa_layer(p0, kvm, rel_bias_table, swa_sinks[0],
                                               bf(_take_runs(swa_w_out[0], rows0, 0)), h0, norm_ffn[0],
                                               moe_up[0], moe_down[0])
    h2, moe_gate_b = _ffn(hn1, h1, bf(ffn_gate[0]), bf(ffn_up[0]), bf(ffn_down[0]), moe_gate[0])

    cols1, scale1 = _dil_weight_layout()
    tok, g1, g2 = _dilproj(h2, norm_mix[1], bf(_take_runs(dil_w_in[0], cols1, 1) * scale1))
    groups = [_dil_group(qkv, rel_bias_table, gi) for gi, qkv in enumerate((tok[None], g1, g2))]
    router_p = jnp.pad(bf(router[0]), ((0, 0), (0, LANES - N_EXPERTS)))
    h3, hn3, meta, counts, meta_t = _dilout([g[0] for g in groups], [g[1] for g in groups], tok, kvm,
                                            bf(dil_w_out[0]), h2, norm_ffn[1], router_p)
    pos0, pos1, items = _moe_plan(meta_t, counts)
    xs = _moe_scatter(hn3, pos0, pos1)
    y = _moe_grouped(items, xs, moe_gate_b, moe_up_b, moe_down_b)
    out = _moe_combine(pos0, pos1, meta, h3, final_norm, y)
    return out.reshape(1, SEQ, D_MODEL)
```

```python
import functools
import math

import jax
import jax.numpy as jnp
import numpy as np
from jax import lax
from jax.experimental import pallas as pl
from jax.experimental.pallas import tpu as pltpu

F32 = jnp.float32
BF16 = jnp.bfloat16

D_MODEL = 1024
SEQ = 16384
HEAD_DIM = 64
N_MIX_HEADS = 12
SWA_KV_HEADS = 3
SWA_GROUP = N_MIX_HEADS // SWA_KV_HEADS
SWA_WINDOW = 128
DIL_GROUPS = ((128, 1), (512, 4), (2048, 16))
DIL_HEADS = 4
N_MEM = 256
MEM_HEADS = 4
BLOCK = 128
N_BUCKETS = 32
MAX_DISTANCE = 2048
D_FF = 3584
N_EXPERTS = 8
EPS = 1e-5
NEG = -1e30
Q_SCALE = HEAD_DIM ** -0.5

SWA_Q = N_MIX_HEADS * HEAD_DIM
SWA_KV = SWA_KV_HEADS * HEAD_DIM
SWA_IN = SWA_Q + 2 * SWA_KV + MEM_HEADS * HEAD_DIM
DIL_W = DIL_HEADS * HEAD_DIM
DIL_IN = len(DIL_GROUPS) * 3 * DIL_W + MEM_HEADS * HEAD_DIM
MEM_W = MEM_HEADS * HEAD_DIM

LANES = 128
SUBLANES = 8
VMEM_LIMIT = 56 * 1024 * 1024

ROW_TILE = 512
BLOCKS_PER_TILE = ROW_TILE // BLOCK
FFN_ROW_TILE = 1024
FFN_COL_TILE = 512


def _bucket_map(dil):
    qi = np.arange(BLOCK)[:, None]
    kj = np.arange(2 * BLOCK)[None, :]
    d = np.maximum((qi + BLOCK - kj) * dil, 0)
    max_exact = N_BUCKETS // 2
    ratio = np.maximum(d, 1).astype(np.float32) / np.float32(max_exact)
    large = max_exact + (np.log(ratio) / np.float32(math.log(MAX_DISTANCE / max_exact))
                         * np.float32(N_BUCKETS - max_exact)).astype(np.int32)
    return np.where(d < max_exact, d, np.minimum(large, N_BUCKETS - 1)).astype(np.int32)


def _rms(x, g):
    ms = jnp.mean(x * x, axis=-1, keepdims=True)
    return x * lax.rsqrt(ms + EPS) * g


def _dot_nt(a, b):
    return lax.dot_general(a, b, (((1,), (1,)), ((), ())), preferred_element_type=F32)


def _dot_tn(a, b):
    return lax.dot_general(a, b, (((0,), (0,)), ((), ())), preferred_element_type=F32)


def _fill_bias(bias_sc, tab_ref, bucket, heads):
    for j, h in enumerate(heads):
        def body(k, b, h=h):
            return jnp.where(bucket == k, tab_ref[k, h], b)
        bias_sc[:, j * BLOCK:(j + 1) * BLOCK] = lax.fori_loop(0, N_BUCKETS, body, jnp.zeros(bucket.shape, F32))


def _fill_row(row_sc, values):
    blk = lax.broadcasted_iota(jnp.int32, row_sc.shape, 1) // BLOCK
    row = jnp.zeros(row_sc.shape, F32)
    for j, v in enumerate(values):
        row = jnp.where(blk == j, v, row)
    row_sc[...] = row


def _folded_bucket_map(dil):
    full = _bucket_map(dil)
    qi = np.arange(BLOCK)[:, None]
    c = np.arange(BLOCK)[None, :]
    return np.ascontiguousarray(np.where(c <= qi, full[:, BLOCK:], full[:, :BLOCK]).T.astype(np.int32))


def _diag_bucket(dil):
    return int(_bucket_map(dil)[0, 0])


def _fold_masks(n):
    c = lax.broadcasted_iota(jnp.int32, (BLOCK, n), 0)
    qi = lax.broadcasted_iota(jnp.int32, (BLOCK, n), 1) % BLOCK
    return c <= qi, c == qi


def _pair_scores(k_slab, q_rows):
    lane = lax.broadcasted_iota(jnp.int32, q_rows.shape, 1)
    zero = jnp.zeros_like(q_rows)
    qa = jnp.where(lane < HEAD_DIM, q_rows, zero)
    qb = jnp.where(lane < HEAD_DIM, zero, q_rows)
    return _dot_nt(k_slab, qa), _dot_nt(k_slab, qb)


def _pair_values(v_slab, p_a, p_b):
    n = p_a.shape[1]
    o = _dot_tn(v_slab, jnp.concatenate([p_a, p_b], axis=1))
    row = lax.broadcasted_iota(jnp.int32, (LANES, n), 0)
    return jnp.where(row < HEAD_DIM, o[:, :n], o[:, n:])


def _band_softmax(st, bias, own, eye, has_prev, sink, diag_bias):
    s_prev, s_own = st[:BLOCK], st[BLOCK:]
    t = jnp.where(own, s_own, s_prev) + bias
    if has_prev is not None:
        t = jnp.where(own | has_prev, t, NEG)
    m = jnp.max(t, axis=0, keepdims=True)
    if diag_bias is not None:
        s_d = jnp.sum(jnp.where(eye, s_prev, 0.0), axis=0, keepdims=True) + diag_bias
        if has_prev is not None:
            s_d = jnp.where(has_prev, s_d, NEG)
        m = jnp.maximum(m, s_d)
    if sink is not None:
        m = jnp.maximum(m, sink)
    e = jnp.exp(t - m)
    den = jnp.sum(e, axis=0, keepdims=True)
    if diag_bias is not None:
        e_d = jnp.exp(s_d - m)
        den = den + e_d
    if sink is not None:
        den = den + jnp.exp(sink - m)
    inv = 1.0 / den
    p = e * inv
    p_own = jnp.where(own, p, 0.0)
    p_prev = jnp.where(own, 0.0, p)
    if diag_bias is not None:
        p_prev = jnp.where(eye, e_d * inv, p_prev)
    return jnp.concatenate([p_prev, p_own], axis=0).astype(BF16), m + jnp.log(den)


def _mem_attention(qm, kvm):
    outs = []
    for s in range(MEM_W // LANES):
        sa, sb = _pair_scores(kvm[:, s * LANES:(s + 1) * LANES], qm[:, s * LANES:(s + 1) * LANES])
        ps = []
        for st in (sa, sb):
            e = jnp.exp(st - jnp.max(st, axis=0, keepdims=True))
            ps.append((e * (1.0 / jnp.sum(e, axis=0, keepdims=True))).astype(BF16))
        outs.append(_pair_values(kvm[:, MEM_W + s * LANES:MEM_W + (s + 1) * LANES], ps[0], ps[1]))
    return outs


def _to_rows(slabs_t):
    return jnp.concatenate([jnp.transpose(x).astype(BF16) for x in slabs_t], axis=-1)


def _memkv_kernel(mem_ref, g_ref, w_ref, o_ref):
    mn = _rms(mem_ref[...], g_ref[...]).astype(BF16)
    o_ref[...] = jnp.dot(mn, w_ref[...], preferred_element_type=F32).astype(BF16)


def _memkv(mem, g, w):
    return pl.pallas_call(
        _memkv_kernel,
        out_shape=jax.ShapeDtypeStruct((N_MEM, w.shape[1]), BF16),
        name="mem_kv",
    )(mem, g.reshape(1, D_MODEL), w)


def _proj_kernel(h_ref, g_ref, w_ref, o_ref):
    hn = _rms(h_ref[...], g_ref[...]).astype(BF16)
    o_ref[...] = jnp.dot(hn, w_ref[...], preferred_element_type=F32).astype(BF16)


def _proj(h, g, w, name):
    n = w.shape[1]
    return pl.pallas_call(
        _proj_kernel,
        grid=(SEQ // ROW_TILE,),
        in_specs=[pl.BlockSpec((ROW_TILE, D_MODEL), lambda i: (i, 0)),
                  pl.BlockSpec((1, D_MODEL), lambda i: (0, 0)),
                  pl.BlockSpec((D_MODEL, n), lambda i: (0, 0))],
        out_specs=pl.BlockSpec((ROW_TILE, n), lambda i: (i, 0)),
        out_shape=jax.ShapeDtypeStruct((SEQ, n), BF16),
        compiler_params=pltpu.CompilerParams(dimension_semantics=("arbitrary",), vmem_limit_bytes=VMEM_LIMIT),
        name=name,
    )(h, g.reshape(1, D_MODEL), w)


SWA_PAIR_SLABS = [(g, SWA_GROUP + g) for g in range(SWA_GROUP)] + [(8, 9), (10, 11)]
SWA_K0 = SWA_Q
SWA_V0 = SWA_K0 + 2 * LANES
SWA_M0 = SWA_V0 + 2 * LANES
SWA_PW = SWA_M0 + MEM_W
SWA_SCORE_HEADS = [0, 1, 2, 3, 4, 5, 6, 7, 8, 10, 9, 11]
SWA_NQ = N_MIX_HEADS * BLOCK


def _cast_slice_spec(w, steps):
    e, r, c = w.shape
    per_e = steps // e
    assert steps % e == 0 and r % per_e == 0 and (r // per_e) % 16 == 0
    return pl.BlockSpec((1, r // per_e, c), lambda i: (i // per_e, i % per_e, 0))


def _swa_kernel(tab_ref, sink_ref, bucket_ref, p_ref, kprev_ref, vprev_ref, kvm_ref, wout_ref, h_ref, g_ref,
                wa_ref, wb_ref, h1_ref, hn1_ref, wa_out, wb_out, bias_sc, sink_sc, k_sc, v_sc, cat_sc):
    i = pl.program_id(0)
    wa_out[...] = wa_ref[...].astype(BF16)
    wb_out[...] = wb_ref[...].astype(BF16)

    @pl.when(i == 0)
    def _():
        _fill_bias(bias_sc, tab_ref, bucket_ref[...], SWA_SCORE_HEADS)
        _fill_row(sink_sc, [sink_ref[h] for h in SWA_SCORE_HEADS])

    k_sc[0:BLOCK, :] = kprev_ref[...]
    k_sc[BLOCK:, :] = p_ref[:, SWA_K0:SWA_V0]
    v_sc[0:BLOCK, :] = vprev_ref[...]
    v_sc[BLOCK:, :] = p_ref[:, SWA_V0:SWA_M0]
    own, eye = _fold_masks(SWA_NQ)

    def block_body(b, carry):
        r0 = b * BLOCK
        has_prev = (i > 0) if b == 0 else None
        qb = p_ref[pl.ds(r0, BLOCK), 0:SWA_Q]
        slab = lambda x, s: x[:, s * LANES:(s + 1) * LANES]
        qa = jnp.concatenate([slab(qb, s) for s in range(4)], axis=0)
        qc = jnp.concatenate([slab(qb, s) for s in (4, 5)], axis=0)
        kb = k_sc[pl.ds(r0, 2 * BLOCK), :]
        vb = v_sc[pl.ds(r0, 2 * BLOCK), :]
        s0, s1 = _pair_scores(slab(kb, 0), qa)
        s2, s3 = _pair_scores(slab(kb, 1), qc)
        st = jnp.concatenate([s0, s1, s2, s3], axis=1)
        pt, _ = _band_softmax(st, bias_sc[...], own, eye, has_prev, sink_sc[...], None)
        na, nc = 4 * BLOCK, 2 * BLOCK
        oa = _pair_values(slab(vb, 0), pt[:, 0:na], pt[:, na:2 * na])
        oc = _pair_values(slab(vb, 1), pt[:, 2 * na:2 * na + nc], pt[:, 2 * na + nc:])
        outs = [slab(oa, s) for s in range(4)] + [slab(oc, s) for s in range(2)]
        outs += _mem_attention(p_ref[pl.ds(r0, BLOCK), SWA_M0:], kvm_ref[...])
        cat_sc[pl.ds(r0, BLOCK), :] = _to_rows(outs)
        return carry

    for b in range(BLOCKS_PER_TILE):
        block_body(b, None)

    out = h_ref[...] + jnp.dot(cat_sc[...], wout_ref[...], preferred_element_type=F32)
    h1_ref[...] = out
    hn1_ref[...] = _rms(out, g_ref[...]).astype(BF16)


def _swa_layer(p, kvm, table, sinks, wout, h, g_ffn, wa, wb):
    assert SWA_WINDOW == BLOCK
    kv_w = 2 * LANES
    prev = lambda i: jnp.maximum(i * BLOCKS_PER_TILE - 1, 0)
    steps = SEQ // ROW_TILE
    return pl.pallas_call(
        _swa_kernel,
        grid=(SEQ // ROW_TILE,),
        in_specs=[
            pl.BlockSpec(memory_space=pltpu.SMEM),
            pl.BlockSpec(memory_space=pltpu.SMEM),
            pl.BlockSpec((BLOCK, BLOCK), lambda i: (0, 0)),
            pl.BlockSpec((ROW_TILE, SWA_PW), lambda i: (i, 0)),
            pl.BlockSpec((BLOCK, kv_w), lambda i: (prev(i), SWA_K0 // kv_w)),
            pl.BlockSpec((BLOCK, kv_w), lambda i: (prev(i), SWA_V0 // kv_w)),
            pl.BlockSpec((N_MEM, 2 * MEM_W), lambda i: (0, 0)),
            pl.BlockSpec((D_MODEL, D_MODEL), lambda i: (0, 0)),
            pl.BlockSpec((ROW_TILE, D_MODEL), lambda i: (i, 0)),
            pl.BlockSpec((1, D_MODEL), lambda i: (0, 0)),
            _cast_slice_spec(wa, steps), _cast_slice_spec(wb, steps),
        ],
        out_specs=[pl.BlockSpec((ROW_TILE, D_MODEL), lambda i: (i, 0)),
                   pl.BlockSpec((ROW_TILE, D_MODEL), lambda i: (i, 0)),
                   _cast_slice_spec(wa, steps), _cast_slice_spec(wb, steps)],
        out_shape=[jax.ShapeDtypeStruct((SEQ, D_MODEL), F32), jax.ShapeDtypeStruct((SEQ, D_MODEL), BF16),
                   jax.ShapeDtypeStruct(wa.shape, BF16), jax.ShapeDtypeStruct(wb.shape, BF16)],
        scratch_shapes=[pltpu.VMEM((BLOCK, SWA_NQ), F32),
                        pltpu.VMEM((1, SWA_NQ), F32),
                        pltpu.VMEM((ROW_TILE + BLOCK, kv_w), BF16),
                        pltpu.VMEM((ROW_TILE + BLOCK, kv_w), BF16),
                        pltpu.VMEM((ROW_TILE, D_MODEL), BF16)],
        compiler_params=pltpu.CompilerParams(dimension_semantics=("arbitrary",), vmem_limit_bytes=VMEM_LIMIT),
        name="swa_mixer",
    )(table, sinks, jnp.asarray(_folded_bucket_map(1)), p, p, p, kvm, wout, h, g_ffn.reshape(1, D_MODEL), wa, wb)


DIL_SLAB = 3 * DIL_W
TOK_W = DIL_SLAB + MEM_W
SLABS_PER_GROUP = DIL_SLAB // LANES


def _dilproj_kernel(h_ref, g_ref, w_ref, tok_ref, g1_ref, g2_ref, slab_sc):
    hn = _rms(h_ref[...], g_ref[...]).astype(BF16)
    res = jnp.dot(hn, w_ref[...], preferred_element_type=F32)
    tok_ref[...] = res[:, :TOK_W].astype(BF16)
    for s in range(2 * SLABS_PER_GROUP):
        slab_sc[s] = res[:, TOK_W + s * LANES:TOK_W + (s + 1) * LANES]
    for gi, out_ref in ((1, g1_ref), (2, g2_ref)):
        d = DIL_GROUPS[gi][1]
        for s in range(SLABS_PER_GROUP):
            for r in range(d):
                rows = slab_sc[(gi - 1) * SLABS_PER_GROUP + s, pl.ds(r, ROW_TILE // d, stride=d), :]
                out_ref[r, :, s * LANES:(s + 1) * LANES] = rows.astype(BF16)


def _dilproj(h, g, w):
    d1, d2 = DIL_GROUPS[1][1], DIL_GROUPS[2][1]
    return pl.pallas_call(
        _dilproj_kernel,
        grid=(SEQ // ROW_TILE,),
        in_specs=[pl.BlockSpec((ROW_TILE, D_MODEL), lambda i: (i, 0)),
                  pl.BlockSpec((1, D_MODEL), lambda i: (0, 0)),
                  pl.BlockSpec((D_MODEL, DIL_IN), lambda i: (0, 0))],
        out_specs=[pl.BlockSpec((ROW_TILE, TOK_W), lambda i: (i, 0)),
                   pl.BlockSpec((d1, ROW_TILE // d1, DIL_SLAB), lambda i: (0, i, 0)),
                   pl.BlockSpec((d2, ROW_TILE // d2, DIL_SLAB), lambda i: (0, i, 0))],
        out_shape=[jax.ShapeDtypeStruct((SEQ, TOK_W), BF16),
                   jax.ShapeDtypeStruct((d1, SEQ // d1, DIL_SLAB), BF16),
                   jax.ShapeDtypeStruct((d2, SEQ // d2, DIL_SLAB), BF16)],
        scratch_shapes=[pltpu.VMEM((2 * SLABS_PER_GROUP, ROW_TILE, LANES), F32)],
        compiler_params=pltpu.CompilerParams(dimension_semantics=("arbitrary",), vmem_limit_bytes=VMEM_LIMIT),
        name="dil_proj",
    )(h, g.reshape(1, D_MODEL), w)


def _dil_kernel(tab_ref, bucket_ref, q_ref, k_ref, v_ref, kp_ref, vp_ref, o_ref, l_ref,
                bias_sc, diag_sc, k_sc, v_sc, *, head0, diag_bucket):
    r = pl.program_id(0)
    n = pl.program_id(1)
    heads = [head0 + h for h in range(DIL_HEADS)]
    nq = DIL_HEADS * BLOCK

    @pl.when((r == 0) & (n == 0))
    def _():
        _fill_bias(bias_sc, tab_ref, bucket_ref[...], heads)
        _fill_row(diag_sc, [tab_ref[diag_bucket, h] for h in heads])

    k_sc[0:BLOCK, :] = kp_ref[...]
    k_sc[BLOCK:, :] = k_ref[...]
    v_sc[0:BLOCK, :] = vp_ref[...]
    v_sc[BLOCK:, :] = v_ref[...]
    own, eye = _fold_masks(nq)
    upper_rows = lax.broadcasted_iota(jnp.int32, (BLOCK, BLOCK), 0) < HEAD_DIM

    def block_body(b, carry):
        r0 = b * BLOCK
        has_prev = (n > 0) if b == 0 else None
        qb = q_ref[pl.ds(r0, BLOCK), :]
        kb = k_sc[pl.ds(r0, 2 * BLOCK), :]
        vb = v_sc[pl.ds(r0, 2 * BLOCK), :]
        slab = lambda x, s: x[:, s * LANES:(s + 1) * LANES]
        scores = []
        for s in range(DIL_W // LANES):
            scores += _pair_scores(slab(kb, s), slab(qb, s))
        pt, lse = _band_softmax(jnp.concatenate(scores, axis=1), bias_sc[...], own, eye, has_prev, None,
                                diag_sc[...])
        outs, lses = [], []
        for s in range(DIL_W // LANES):
            ca, cb = 2 * s * BLOCK, (2 * s + 1) * BLOCK
            outs.append(jnp.transpose(_pair_values(slab(vb, s), pt[:, ca:ca + BLOCK], pt[:, cb:cb + BLOCK])))
            lse_t = jnp.where(upper_rows, jnp.broadcast_to(lse[:, ca:ca + BLOCK], (BLOCK, BLOCK)),
                              jnp.broadcast_to(lse[:, cb:cb + BLOCK], (BLOCK, BLOCK)))
            lses.append(jnp.transpose(lse_t))
        o_ref[pl.ds(r0, BLOCK), :] = jnp.concatenate(outs, axis=-1)
        l_ref[pl.ds(r0, BLOCK), :] = jnp.concatenate(lses, axis=-1)
        return carry

    for b in range(BLOCKS_PER_TILE):
        block_body(b, None)


def _dil_group(qkv, table, gi):
    window, d = DIL_GROUPS[gi]
    rows = SEQ // d
    assert qkv.shape[:2] == (d, rows) and window // d == BLOCK
    prev = lambda n: jnp.maximum(n * BLOCKS_PER_TILE - 1, 0)
    tile = lambda c: pl.BlockSpec((None, ROW_TILE, DIL_W), lambda r, n: (r, n, c))
    prev_block = lambda c: pl.BlockSpec((None, BLOCK, DIL_W), lambda r, n: (r, prev(n), c))
    return pl.pallas_call(
        functools.partial(_dil_kernel, head0=gi * DIL_HEADS, diag_bucket=_diag_bucket(d)),
        grid=(d, rows // ROW_TILE),
        in_specs=[
            pl.BlockSpec(memory_space=pltpu.SMEM),
            pl.BlockSpec((BLOCK, BLOCK), lambda r, n: (0, 0)),
            tile(0), tile(1), tile(2), prev_block(1), prev_block(2),
        ],
        out_specs=[tile(0), tile(0)],
        out_shape=[jax.ShapeDtypeStruct((d, rows, DIL_W), F32), jax.ShapeDtypeStruct((d, rows, DIL_W), F32)],
        scratch_shapes=[pltpu.VMEM((BLOCK, DIL_HEADS * BLOCK), F32),
                        pltpu.VMEM((1, DIL_HEADS * BLOCK), F32),
                        pltpu.VMEM((ROW_TILE + BLOCK, DIL_W), BF16),
                        pltpu.VMEM((ROW_TILE + BLOCK, DIL_W), BF16)],
        compiler_params=pltpu.CompilerParams(dimension_semantics=("arbitrary", "arbitrary"),
                                             vmem_limit_bytes=VMEM_LIMIT),
        name=f"dil_attn_{gi}",
    )(table, jnp.asarray(_folded_bucket_map(d)), qkv, qkv, qkv, qkv, qkv)


def _dilout_kernel(o0_ref, o1_ref, o2_ref, l0_ref, l1_ref, l2_ref, qm_ref, kvm_ref, wout_ref, h_ref, g_ref,
                   router_ref, h2_ref, hn2_ref, meta_ref, cnt_ref, metat_ref, carry_sc, tok_sc):
    @pl.when(pl.program_id(0) == 0)
    def _():
        carry_sc[...] = jnp.zeros_like(carry_sc)

    for k, src_ref in enumerate((o1_ref, l1_ref, o2_ref, l2_ref)):
        d = src_ref.shape[0]
        for s in range(DIL_W // LANES):
            for r in range(d):
                tok_sc[k, s, pl.ds(r, ROW_TILE // d, stride=d), :] = src_ref[r, :, s * LANES:(s + 1) * LANES]

    mixed = []
    for s in range(DIL_W // LANES):
        cols = slice(s * LANES, (s + 1) * LANES)
        o0, o1, o2 = o0_ref[:, cols], tok_sc[0, s], tok_sc[2, s]
        l0, l1, l2 = l0_ref[:, cols], tok_sc[1, s], tok_sc[3, s]
        mx = jnp.maximum(jnp.maximum(l0, l1), l2)
        e0, e1, e2 = jnp.exp(l0 - mx), jnp.exp(l1 - mx), jnp.exp(l2 - mx)
        inv = 1.0 / (e0 + e1 + e2)
        mixed.append(((e0 * inv) * o0 + (e1 * inv) * o1 + (e2 * inv) * o2).astype(BF16))
    cat = jnp.concatenate(mixed + [_to_rows(_mem_attention(qm_ref[...], kvm_ref[...]))], axis=-1)
    out = h_ref[...] + jnp.dot(cat, wout_ref[...], preferred_element_type=F32)
    h2_ref[...] = out
    hn = _rms(out, g_ref[...]).astype(BF16)
    hn2_ref[...] = hn

    logits = _dot_nt(router_ref[...], hn)
    row = lax.broadcasted_iota(jnp.int32, logits.shape, 0)
    masked = jnp.where(row < N_EXPERTS, logits, -jnp.inf)
    v0 = jnp.max(masked, axis=0, keepdims=True)
    i0 = jnp.min(jnp.where(masked == v0, row, ROUTER_ROWS), axis=0, keepdims=True)
    rest = jnp.where(row == i0, -jnp.inf, masked)
    v1 = jnp.max(rest, axis=0, keepdims=True)
    i1 = jnp.min(jnp.where(rest == v1, row, ROUTER_ROWS), axis=0, keepdims=True)
    ex = jnp.exp(v1 - v0)
    inv2 = 1.0 / (1.0 + ex)

    oh0, oh1 = row == i0, row == i1
    sel = (oh0 | oh1).astype(F32)
    tok_r = lax.broadcasted_iota(jnp.int32, (ROW_TILE, ROW_TILE), 0)
    tok_c = lax.broadcasted_iota(jnp.int32, (ROW_TILE, ROW_TILE), 1)
    earlier = (tok_r < tok_c).astype(F32).astype(BF16)
    carry = carry_sc[:, 0:1]
    before = jnp.dot(sel.astype(BF16), earlier, preferred_element_type=F32) + carry
    rank0 = jnp.sum(jnp.where(oh0, before, 0.0), axis=0, keepdims=True)
    rank1 = jnp.sum(jnp.where(oh1, before, 0.0), axis=0, keepdims=True)
    count = jnp.broadcast_to(carry + jnp.sum(sel, axis=1, keepdims=True), carry_sc.shape)
    carry_sc[...] = count
    cnt_ref[...] = count[:SUBLANES]

    fields = (i0.astype(F32), i1.astype(F32), inv2, ex * inv2, rank0, rank1)
    meta_t = jnp.zeros((LANES, ROW_TILE), F32)
    frow = lax.broadcasted_iota(jnp.int32, meta_t.shape, 0)
    for k, f in enumerate(fields):
        meta_t = jnp.where(frow == k, f, meta_t)
    metat_ref[...] = meta_t[:SUBLANES]
    meta_ref[...] = jnp.transpose(meta_t)


META_E0, META_E1, META_G0, META_G1, META_RANK0, META_RANK1 = range(6)
ROUTER_ROWS = 16


def _dilout(os, ls, tok, kvm, wout, h, g_ffn, router):
    row = lambda i: (i, 0)
    const = lambda i: (0, 0)

    def group_spec(a):
        d = a.shape[0]
        if d == 1:
            return pl.BlockSpec((None, ROW_TILE, DIL_W), lambda i: (0, i, 0))
        return pl.BlockSpec((d, ROW_TILE // d, DIL_W), lambda i: (0, i, 0))

    return pl.pallas_call(
        _dilout_kernel,
        grid=(SEQ // ROW_TILE,),
        in_specs=[group_spec(a) for a in (*os, *ls)] + [
                  pl.BlockSpec((ROW_TILE, MEM_W), lambda i: (i, DIL_SLAB // MEM_W)),
                  pl.BlockSpec((N_MEM, 2 * MEM_W), lambda i: (0, 1)),
                  pl.BlockSpec((DIL_W + MEM_W, D_MODEL), const),
                  pl.BlockSpec((ROW_TILE, D_MODEL), row),
                  pl.BlockSpec((1, D_MODEL), const),
                  pl.BlockSpec((ROUTER_ROWS, D_MODEL), const)],
        out_specs=[pl.BlockSpec((ROW_TILE, D_MODEL), row),
                   pl.BlockSpec((ROW_TILE, D_MODEL), row),
                   pl.BlockSpec((ROW_TILE, LANES), row),
                   pl.BlockSpec((SUBLANES, LANES), const),
                   pl.BlockSpec((SUBLANES, ROW_TILE), lambda i: (0, i))],
        out_shape=[jax.ShapeDtypeStruct((SEQ, D_MODEL), F32), jax.ShapeDtypeStruct((SEQ, D_MODEL), BF16),
                   jax.ShapeDtypeStruct((SEQ, LANES), F32), jax.ShapeDtypeStruct((SUBLANES, LANES), F32),
                   jax.ShapeDtypeStruct((SUBLANES, SEQ), F32)],
        scratch_shapes=[pltpu.VMEM((ROUTER_ROWS, LANES), F32),
                        pltpu.VMEM((4, DIL_W // LANES, ROW_TILE, LANES), F32)],
        compiler_params=pltpu.CompilerParams(dimension_semantics=("arbitrary",), vmem_limit_bytes=VMEM_LIMIT),
        name="dil_out",
    )(*os, *ls, tok, kvm, wout, h, g_ffn.reshape(1, D_MODEL), router)


def _swiglu_partial(x, wg, wu, wd):
    a = jnp.dot(x, wg, preferred_element_type=F32)
    b = jnp.dot(x, wu, preferred_element_type=F32)
    hm = (a * jax.nn.sigmoid(a) * b).astype(BF16)
    return jnp.dot(hm, wd, preferred_element_type=F32)


def _ffn_kernel(x_ref, wg_ref, wu_ref, wd_ref, h_ref, wa_ref, o_ref, wa_out):
    wa_out[...] = wa_ref[...].astype(BF16)
    x = x_ref[...]
    acc = h_ref[...]
    for c in range(D_FF // FFN_COL_TILE):
        cols = slice(c * FFN_COL_TILE, (c + 1) * FFN_COL_TILE)
        acc = acc + _swiglu_partial(x, wg_ref[:, cols], wu_ref[:, cols], wd_ref[cols, :])
    o_ref[...] = acc


def _ffn(x, h, wg, wu, wd, wa):
    tm = ROW_TILE
    steps = SEQ // tm
    resident = lambda shape: pl.BlockSpec(shape, lambda i: (0, 0), pipeline_mode=pl.Buffered(1))
    return pl.pallas_call(
        _ffn_kernel,
        grid=(steps,),
        in_specs=[pl.BlockSpec((tm, D_MODEL), lambda i: (i, 0)),
                  resident((D_MODEL, D_FF)), resident((D_MODEL, D_FF)), resident((D_FF, D_MODEL)),
                  pl.BlockSpec((tm, D_MODEL), lambda i: (i, 0)),
                  _cast_slice_spec(wa, steps)],
        out_specs=[pl.BlockSpec((tm, D_MODEL), lambda i: (i, 0)), _cast_slice_spec(wa, steps)],
        out_shape=[jax.ShapeDtypeStruct((SEQ, D_MODEL), F32), jax.ShapeDtypeStruct(wa.shape, BF16)],
        compiler_params=pltpu.CompilerParams(dimension_semantics=("arbitrary",), vmem_limit_bytes=VMEM_LIMIT),
        name="ffn",
    )(x, wg, wu, wd, h, wa)


MOE_SUB = 512
MOE_ROWS = 1024
MOE_COL_TILE = D_FF // 2
MOE_SUBS = MOE_ROWS // MOE_SUB
MOE_SLOTS = 2 * SEQ + N_EXPERTS * MOE_SUB
MOE_TILES = MOE_SLOTS // MOE_ROWS
MOE_ITEMS = MOE_TILES + N_EXPERTS - 1
PACK_W = D_MODEL // 2
ITEM_TILE, ITEM_EXPERT, ITEM_LO, ITEM_HI, ITEM_FIRST = range(5)
DMA_UNROLL = 8


def _pack_bf16_pairs(x):
    lo = lax.bitcast_convert_type(x[:, :PACK_W].astype(F32), jnp.uint32) >> 16
    hi = lax.bitcast_convert_type(x[:, PACK_W:].astype(F32), jnp.uint32) & jnp.uint32(0xFFFF0000)
    return hi | lo


def _unpack_bf16_pairs(pk):
    lo = lax.bitcast_convert_type(pk << 16, F32).astype(BF16)
    hi = lax.bitcast_convert_type(pk & jnp.uint32(0xFFFF0000), F32).astype(BF16)
    return lo, hi


def _vmem_row(ref, group, sub):
    return ref.at[group, pl.ds(sub, 1), :]


def _hbm_row(ref, row):
    return ref.at[pl.ds(row, 1), :]


def _scatter_kernel(pos0_ref, pos1_ref, x_ref, xs_in_ref, xs_ref, pk_sc, sem):
    del xs_in_ref
    pk_sc[...] = _pack_bf16_pairs(x_ref[...]).reshape(pk_sc.shape)

    def issue(g, c):
        for u in range(SUBLANES):
            t = g * SUBLANES + u
            pltpu.make_async_copy(_vmem_row(pk_sc, g, u), _hbm_row(xs_ref, pos0_ref[t]), sem).start(priority=0)
            pltpu.make_async_copy(_vmem_row(pk_sc, g, u), _hbm_row(xs_ref, pos1_ref[t]), sem).start(priority=1)
        return c

    def drain(g, c):
        for _ in range(2 * SUBLANES):
            pltpu.make_async_copy(_vmem_row(pk_sc, 0, 0), _hbm_row(xs_ref, 0), sem).wait()
        return c

    lax.fori_loop(0, ROW_TILE // SUBLANES, issue, 0)
    lax.fori_loop(0, ROW_TILE // SUBLANES, drain, 0)


def _moe_scatter(x, pos0, pos1):
    smem_tile = pl.BlockSpec((ROW_TILE,), lambda i: (i,), memory_space=pltpu.SMEM)
    return pl.pallas_call(
        _scatter_kernel,
        grid=(SEQ // ROW_TILE,),
        in_specs=[smem_tile, smem_tile,
                  pl.BlockSpec((ROW_TILE, D_MODEL), lambda i: (i, 0)),
                  pl.BlockSpec(memory_space=pl.ANY)],
        out_specs=pl.BlockSpec(memory_space=pl.ANY),
        out_shape=jax.ShapeDtypeStruct((MOE_SLOTS, PACK_W), jnp.uint32),
        scratch_shapes=[pltpu.VMEM((ROW_TILE // SUBLANES, SUBLANES, PACK_W), jnp.uint32),
                        pltpu.SemaphoreType.DMA],
        input_output_aliases={3: 0},
        compiler_params=pltpu.CompilerParams(dimension_semantics=("arbitrary",), vmem_limit_bytes=VMEM_LIMIT),
        name="moe_scatter",
    )(pos0, pos1, x, jnp.zeros((MOE_SLOTS, PACK_W), jnp.uint32))


def _moe_kernel(items_ref, xs_ref, wg_ref, wu_ref, wd_ref, y_ref, xb_sc):
    w = pl.program_id(0)
    j = pl.program_id(1)
    lo = items_ref[ITEM_LO, w]
    hi = items_ref[ITEM_HI, w]

    @pl.when((j == 0) & (hi > lo))
    def _():
        xlo, xhi = _unpack_bf16_pairs(xs_ref[...])
        xb_sc[:, :PACK_W] = xlo
        xb_sc[:, PACK_W:] = xhi

    @pl.when((j == 0) & (items_ref[ITEM_FIRST, w] == 1))
    def _():
        y_ref[...] = jnp.zeros_like(y_ref)

    for sb in range(MOE_SUBS):
        @pl.when((lo <= sb) & (sb < hi))
        def _(sb=sb):
            rows = slice(sb * MOE_SUB, (sb + 1) * MOE_SUB)
            x = xb_sc[rows, :]
            acc = y_ref[rows, :]
            for c0 in range(0, MOE_COL_TILE, FFN_COL_TILE):
                cols = slice(c0, min(c0 + FFN_COL_TILE, MOE_COL_TILE))
                acc = acc + _swiglu_partial(x, wg_ref[0, :, cols], wu_ref[0, :, cols], wd_ref[0, cols, :])
            y_ref[rows, :] = acc


def _moe_grouped(items, xs, wg, wu, wd):
    tf = MOE_COL_TILE
    nf = D_FF // tf
    jf = lambda w, j, it: jnp.where(it[ITEM_HI, w] > it[ITEM_LO, w], j, nf - 1)
    grid_spec = pltpu.PrefetchScalarGridSpec(
        num_scalar_prefetch=1,
        grid=(MOE_ITEMS, nf),
        in_specs=[pl.BlockSpec((MOE_ROWS, PACK_W), lambda w, j, it: (it[ITEM_TILE, w], 0)),
                  pl.BlockSpec((1, D_MODEL, tf), lambda w, j, it: (it[ITEM_EXPERT, w], 0, jf(w, j, it))),
                  pl.BlockSpec((1, D_MODEL, tf), lambda w, j, it: (it[ITEM_EXPERT, w], 0, jf(w, j, it))),
                  pl.BlockSpec((1, tf, D_MODEL), lambda w, j, it: (it[ITEM_EXPERT, w], jf(w, j, it), 0))],
        out_specs=pl.BlockSpec((MOE_ROWS, D_MODEL), lambda w, j, it: (it[ITEM_TILE, w], 0)),
        scratch_shapes=[pltpu.VMEM((MOE_ROWS, D_MODEL), BF16)],
    )
    return pl.pallas_call(
        _moe_kernel,
        grid_spec=grid_spec,
        out_shape=jax.ShapeDtypeStruct((MOE_SLOTS, D_MODEL), F32),
        compiler_params=pltpu.CompilerParams(dimension_semantics=("arbitrary", "arbitrary"),
                                             vmem_limit_bytes=VMEM_LIMIT),
        name="moe_grouped",
    )(items, xs, wg, wu, wd)


def _combine_kernel(pos0_ref, pos1_ref, meta_ref, h_ref, gf_ref, y_ref, o_ref, y0_sc, y1_sc, sem):
    def issue(g, c):
        for u in range(SUBLANES):
            t = g * SUBLANES + u
            pltpu.make_async_copy(_hbm_row(y_ref, pos0_ref[t]), _vmem_row(y0_sc, g, u), sem).start(priority=0)
            pltpu.make_async_copy(_hbm_row(y_ref, pos1_ref[t]), _vmem_row(y1_sc, g, u), sem).start(priority=1)
        return c

    def drain(g, c):
        for _ in range(SUBLANES):
            pltpu.make_async_copy(_hbm_row(y_ref, 0), _vmem_row(y0_sc, 0, 0), sem).wait()
            pltpu.make_async_copy(_hbm_row(y_ref, 0), _vmem_row(y1_sc, 0, 0), sem).wait()
        return c

    lax.fori_loop(0, ROW_TILE // SUBLANES, issue, 0)
    lax.fori_loop(0, ROW_TILE // SUBLANES, drain, 0)
    meta = meta_ref[...]
    g0 = meta[:, META_G0:META_G0 + 1]
    g1 = meta[:, META_G1:META_G1 + 1]
    y0 = y0_sc[...].reshape(ROW_TILE, D_MODEL)
    y1 = y1_sc[...].reshape(ROW_TILE, D_MODEL)
    o_ref[...] = _rms(h_ref[...] + (g0 * y0 + g1 * y1), gf_ref[...])


def _moe_combine(pos0, pos1, meta, h, g_final, y):
    smem_tile = pl.BlockSpec((ROW_TILE,), lambda i: (i,), memory_space=pltpu.SMEM)
    tiles = (ROW_TILE // SUBLANES, SUBLANES, D_MODEL)
    return pl.pallas_call(
        _combine_kernel,
        grid=(SEQ // ROW_TILE,),
        in_specs=[smem_tile, smem_tile,
                  pl.BlockSpec((ROW_TILE, LANES), lambda i: (i, 0)),
                  pl.BlockSpec((ROW_TILE, D_MODEL), lambda i: (i, 0)),
                  pl.BlockSpec((1, D_MODEL), lambda i: (0, 0)),
                  pl.BlockSpec(memory_space=pl.ANY)],
        out_specs=pl.BlockSpec((ROW_TILE, D_MODEL), lambda i: (i, 0)),
        out_shape=jax.ShapeDtypeStruct((SEQ, D_MODEL), F32),
        scratch_shapes=[pltpu.VMEM(tiles, F32), pltpu.VMEM(tiles, F32), pltpu.SemaphoreType.DMA],
        compiler_params=pltpu.CompilerParams(dimension_semantics=("arbitrary",), vmem_limit_bytes=VMEM_LIMIT),
        name="moe_combine",
    )(pos0, pos1, meta, h, g_final.reshape(1, D_MODEL), y)


def _moe_plan(meta_t, counts):
    i32 = jnp.int32
    cnt = counts[:N_EXPERTS, 0].astype(i32)
    padded = (cnt + MOE_SUB - 1) // MOE_SUB * MOE_SUB
    start = jnp.cumsum(padded) - padded

    def slot(e_row, rank_row):
        e = meta_t[e_row].astype(i32)
        base = sum(jnp.where(e == k, start[k], 0) for k in range(N_EXPERTS))
        return base + meta_t[rank_row].astype(i32)

    pos0, pos1 = slot(META_E0, META_RANK0), slot(META_E1, META_RANK1)

    first_sb, end_sb = start // MOE_SUB, (start + padded) // MOE_SUB
    t0 = jnp.arange(MOE_TILES, dtype=i32)[:, None] * MOE_SUBS
    lo = jnp.clip(first_sb[None, :] - t0, 0, MOE_SUBS)
    hi = jnp.clip(end_sb[None, :] - t0, 0, MOE_SUBS)
    active = hi > lo
    none = ~jnp.any(active, axis=1, keepdims=True)
    active = jnp.concatenate([active, none], axis=1)
    lo = jnp.concatenate([lo, jnp.zeros_like(t0)], axis=1)
    hi = jnp.concatenate([hi, jnp.zeros_like(t0)], axis=1)
    first = active & (jnp.cumsum(active.astype(i32), axis=1) == 1)
    ncol = N_EXPERTS + 1
    flat = jnp.arange(MOE_TILES * ncol, dtype=i32)
    key = jnp.where(active.reshape(-1), flat, MOE_TILES * ncol)
    order = jnp.argsort(key)[:MOE_ITEMS]
    live = key[order] < MOE_TILES * ncol
    tile = jnp.where(live, order // ncol, MOE_TILES - 1)
    lo_i = jnp.where(live, lo.reshape(-1)[order], 0)
    hi_i = jnp.where(live, hi.reshape(-1)[order], 0)
    first_i = jnp.where(live, first.reshape(-1)[order], False).astype(i32)
    idx = jnp.arange(MOE_ITEMS, dtype=i32)
    src = jnp.maximum(lax.cummax(jnp.where(hi_i > lo_i, idx, -1), axis=0), 0)
    expert = jnp.minimum(order % ncol, N_EXPERTS - 1)[src]
    items = jnp.stack([tile, expert, lo_i, hi_i, first_i]).astype(i32)
    return pos0, pos1, items


def _take_runs(w, idx, axis):
    idx = list(idx)
    runs, start = [], 0
    for i in range(1, len(idx) + 1):
        if i == len(idx) or idx[i] != idx[i - 1] + 1:
            runs.append(lax.slice_in_dim(w, idx[start], idx[i - 1] + 1, axis=axis))
            start = i
    return jnp.concatenate(runs, axis=axis)


def _swa_weight_layout():
    head = lambda base, j: list(range(base + j * HEAD_DIM, base + (j + 1) * HEAD_DIM))
    k0, v0, m0 = SWA_Q, SWA_Q + SWA_KV, SWA_Q + 2 * SWA_KV
    cols, rows = [], []
    for a, b in SWA_PAIR_SLABS:
        cols += head(0, a) + head(0, b)
        rows += head(0, a) + head(0, b)
    for base in (k0, v0):
        cols += head(base, 0) + head(base, 1) + head(base, 2) + head(base, 2)
    cols += list(range(m0, m0 + MEM_W))
    rows += list(range(SWA_Q, SWA_Q + MEM_W))
    scale = np.ones((len(cols),), np.float32)
    scale[:SWA_Q] = Q_SCALE
    scale[SWA_M0:] = Q_SCALE
    assert len(cols) == SWA_PW and len(rows) == D_MODEL
    return np.asarray(cols), scale, np.asarray(rows)


def _dil_weight_layout():
    n_grp = len(DIL_GROUPS) * DIL_SLAB
    cols = list(range(DIL_SLAB)) + list(range(n_grp, n_grp + MEM_W)) + list(range(DIL_SLAB, n_grp))
    scale = np.ones((len(cols),), np.float32)
    for c0 in (0, DIL_SLAB, TOK_W, TOK_W + DIL_SLAB):
        scale[c0:c0 + DIL_W] = Q_SCALE
    return np.asarray(cols), scale


def kernel(x, mem, rel_bias_table, mem_norm, norm_mix, norm_ffn, final_norm, swa_w_in, swa_sinks, swa_w_mem_kv,
           swa_w_out, dil_w_in, dil_w_mem_kv, dil_w_out, ffn_gate, ffn_up, ffn_down, router, moe_gate, moe_up,
           moe_down):
    assert x.shape == (1, SEQ, D_MODEL) and mem.shape == (1, N_MEM, D_MODEL)
    assert norm_mix.shape == (2, D_MODEL) and swa_w_in.shape == (1, D_MODEL, SWA_IN)
    assert dil_w_in.shape == (1, D_MODEL, DIL_IN) and moe_gate.shape == (1, N_EXPERTS, D_MODEL, D_FF)
    bf = lambda a: a.astype(BF16)
    h0 = x.reshape(SEQ, D_MODEL)

    kvm = _memkv(mem[0], mem_norm, bf(jnp.concatenate([swa_w_mem_kv[0], dil_w_mem_kv[0]], axis=1)))

    cols0, scale0, rows0 = _swa_weight_layout()
    p0 = _proj(h0, norm_mix[0], bf(_take_runs(swa_w_in[0], cols0, 1) * scale0), "swa_proj")
    h1, hn1, moe_up_b, moe_down_b = _swa_layer(p0, kvm, rel_bias_table, swa_sinks[0],
                                               bf(_take_runs(swa_w_out[0], rows0, 0)), h0, norm_ffn[0],
                                               moe_up[0], moe_down[0])
    h2, moe_gate_b = _ffn(hn1, h1, bf(ffn_gate[0]), bf(ffn_up[0]), bf(ffn_down[0]), moe_gate[0])

    cols1, scale1 = _dil_weight_layout()
    tok, g1, g2 = _dilproj(h2, norm_mix[1], bf(_take_runs(dil_w_in[0], cols1, 1) * scale1))
    groups = [_dil_group(qkv, rel_bias_table, gi) for gi, qkv in enumerate((tok[None], g1, g2))]
    router_p = jnp.pad(bf(router[0].T), ((0, ROUTER_ROWS - N_EXPERTS), (0, 0)))
    h3, hn3, meta, counts, meta_t = _dilout([g[0] for g in groups], [g[1] for g in groups], tok, kvm,
                                            bf(dil_w_out[0]), h2, norm_ffn[1], router_p)
    pos0, pos1, items = _moe_plan(meta_t, counts)
    xs = _moe_scatter(hn3, pos0, pos1)
    y = _moe_grouped(items, xs, moe_gate_b, moe_up_b, moe_down_b)
    out = _moe_combine(pos0, pos1, meta, h3, final_norm, y)
    return out.reshape(1, SEQ, D_MODEL)
```

```python
import functools
import math

import jax
import jax.numpy as jnp
import numpy as np
from jax import lax
from jax.experimental import pallas as pl
from jax.experimental.pallas import tpu as pltpu

F32 = jnp.float32
BF16 = jnp.bfloat16

D_MODEL = 1024
SEQ = 16384
HEAD_DIM = 64
N_MIX_HEADS = 12
SWA_KV_HEADS = 3
SWA_GROUP = N_MIX_HEADS // SWA_KV_HEADS
SWA_WINDOW = 128
DIL_GROUPS = ((128, 1), (512, 4), (2048, 16))
DIL_HEADS = 4
N_MEM = 256
MEM_HEADS = 4
BLOCK = 128
N_BUCKETS = 32
MAX_DISTANCE = 2048
D_FF = 3584
N_EXPERTS = 8
EPS = 1e-5
NEG = -1e30
Q_SCALE = HEAD_DIM ** -0.5

SWA_Q = N_MIX_HEADS * HEAD_DIM
SWA_KV = SWA_KV_HEADS * HEAD_DIM
SWA_IN = SWA_Q + 2 * SWA_KV + MEM_HEADS * HEAD_DIM
DIL_W = DIL_HEADS * HEAD_DIM
DIL_IN = len(DIL_GROUPS) * 3 * DIL_W + MEM_HEADS * HEAD_DIM
MEM_W = MEM_HEADS * HEAD_DIM

LANES = 128
SUBLANES = 8
VMEM_LIMIT = 56 * 1024 * 1024

ROW_TILE = 512
BLOCKS_PER_TILE = ROW_TILE // BLOCK
FFN_COL_TILE = 512


def _bucket_map(dil):
    qi = np.arange(BLOCK)[:, None]
    kj = np.arange(2 * BLOCK)[None, :]
    d = np.maximum((qi + BLOCK - kj) * dil, 0)
    max_exact = N_BUCKETS // 2
    ratio = np.maximum(d, 1).astype(np.float32) / np.float32(max_exact)
    large = max_exact + (np.log(ratio) / np.float32(math.log(MAX_DISTANCE / max_exact))
                         * np.float32(N_BUCKETS - max_exact)).astype(np.int32)
    return np.where(d < max_exact, d, np.minimum(large, N_BUCKETS - 1)).astype(np.int32)


def _rms(x, g):
    ms = jnp.mean(x * x, axis=-1, keepdims=True)
    return x * lax.rsqrt(ms + EPS) * g


def _dot_nt(a, b):
    return lax.dot_general(a, b, (((1,), (1,)), ((), ())), preferred_element_type=F32)


def _dot_tn(a, b):
    return lax.dot_general(a, b, (((0,), (0,)), ((), ())), preferred_element_type=F32)


def _fill_bias(bias_sc, tab_ref, bucket, heads):
    for j, h in enumerate(heads):
        def body(k, b, h=h):
            return jnp.where(bucket == k, tab_ref[k, h], b)
        bias_sc[:, j * BLOCK:(j + 1) * BLOCK] = lax.fori_loop(0, N_BUCKETS, body, jnp.zeros(bucket.shape, F32))


def _fill_row(row_sc, values):
    blk = lax.broadcasted_iota(jnp.int32, row_sc.shape, 1) // BLOCK
    row = jnp.zeros(row_sc.shape, F32)
    for j, v in enumerate(values):
        row = jnp.where(blk == j, v, row)
    row_sc[...] = row


def _folded_bucket_map(dil):
    full = _bucket_map(dil)
    qi = np.arange(BLOCK)[:, None]
    c = np.arange(BLOCK)[None, :]
    return np.ascontiguousarray(np.where(c <= qi, full[:, BLOCK:], full[:, :BLOCK]).T.astype(np.int32))


def _diag_bucket(dil):
    return int(_bucket_map(dil)[0, 0])


def _fold_masks(n):
    c = lax.broadcasted_iota(jnp.int32, (BLOCK, n), 0)
    qi = lax.broadcasted_iota(jnp.int32, (BLOCK, n), 1) % BLOCK
    return c <= qi, c == qi


def _pair_scores(k_slab, q_rows):
    lane = lax.broadcasted_iota(jnp.int32, q_rows.shape, 1)
    zero = jnp.zeros_like(q_rows)
    qa = jnp.where(lane < HEAD_DIM, q_rows, zero)
    qb = jnp.where(lane < HEAD_DIM, zero, q_rows)
    return _dot_nt(k_slab, qa), _dot_nt(k_slab, qb)


def _pair_values(v_slab, p_a, p_b):
    n = p_a.shape[1]
    o = _dot_tn(v_slab, jnp.concatenate([p_a, p_b], axis=1))
    row = lax.broadcasted_iota(jnp.int32, (LANES, n), 0)
    return jnp.where(row < HEAD_DIM, o[:, :n], o[:, n:])


def _band_softmax(st, bias, own, eye, has_prev, sink, diag_bias):
    s_prev, s_own = st[:BLOCK], st[BLOCK:]
    t = jnp.where(own, s_own, s_prev) + bias
    if has_prev is not None:
        t = jnp.where(own | has_prev, t, NEG)
    m = jnp.max(t, axis=0, keepdims=True)
    if diag_bias is not None:
        s_d = jnp.sum(jnp.where(eye, s_prev, 0.0), axis=0, keepdims=True) + diag_bias
        if has_prev is not None:
            s_d = jnp.where(has_prev, s_d, NEG)
        m = jnp.maximum(m, s_d)
    if sink is not None:
        m = jnp.maximum(m, sink)
    e = jnp.exp(t - m)
    den = jnp.sum(e, axis=0, keepdims=True)
    if diag_bias is not None:
        e_d = jnp.exp(s_d - m)
        den = den + e_d
    if sink is not None:
        den = den + jnp.exp(sink - m)
    inv = 1.0 / den
    p = e * inv
    p_own = jnp.where(own, p, 0.0)
    p_prev = jnp.where(own, 0.0, p)
    if diag_bias is not None:
        p_prev = jnp.where(eye, e_d * inv, p_prev)
    return jnp.concatenate([p_prev, p_own], axis=0).astype(BF16), m + jnp.log(den)


def _mem_attention(qm, kvm):
    outs = []
    for s in range(MEM_W // LANES):
        sa, sb = _pair_scores(kvm[:, s * LANES:(s + 1) * LANES], qm[:, s * LANES:(s + 1) * LANES])
        ps = []
        for st in (sa, sb):
            e = jnp.exp(st - jnp.max(st, axis=0, keepdims=True))
            ps.append((e * (1.0 / jnp.sum(e, axis=0, keepdims=True))).astype(BF16))
        outs.append(_pair_values(kvm[:, MEM_W + s * LANES:MEM_W + (s + 1) * LANES], ps[0], ps[1]))
    return outs


def _to_rows(slabs_t):
    return jnp.concatenate([jnp.transpose(x).astype(BF16) for x in slabs_t], axis=-1)


def _memkv_kernel(mem_ref, g_ref, w_ref, o_ref):
    mn = _rms(mem_ref[...], g_ref[...]).astype(BF16)
    o_ref[...] = jnp.dot(mn, w_ref[...], preferred_element_type=F32).astype(BF16)


def _memkv(mem, g, w):
    return pl.pallas_call(
        _memkv_kernel,
        out_shape=jax.ShapeDtypeStruct((N_MEM, w.shape[1]), BF16),
        name="mem_kv",
    )(mem, g.reshape(1, D_MODEL), w)


SWA_PAIR_SLABS = [(g, SWA_GROUP + g) for g in range(SWA_GROUP)] + [(8, 9), (10, 11)]
SWA_K0 = SWA_Q
SWA_V0 = SWA_K0 + 2 * LANES
SWA_M0 = SWA_V0 + 2 * LANES
SWA_PW = SWA_M0 + MEM_W
SWA_SCORE_HEADS = [0, 1, 2, 3, 4, 5, 6, 7, 8, 10, 9, 11]
SWA_NQ = N_MIX_HEADS * BLOCK


def _cast_slice_spec(w, steps):
    e, r, c = w.shape
    per_e = steps // e
    assert steps % e == 0 and r % per_e == 0 and (r // per_e) % 16 == 0
    return pl.BlockSpec((1, r // per_e, c), lambda i: (i // per_e, i % per_e, 0))


def _swa_kernel(tab_ref, sink_ref, bucket_ref, gmix_ref, win_ref, kvm_ref, wout_ref, h_ref, g_ref,
                wa_ref, wb_ref, h1_ref, hn1_ref, wa_out, wb_out, bias_sc, sink_sc, k_sc, v_sc, cat_sc, p_ref,
                kprev_sc, vprev_sc):
    i = pl.program_id(0)
    wa_out[...] = wa_ref[...].astype(BF16)
    wb_out[...] = wb_ref[...].astype(BF16)

    @pl.when(i == 0)
    def _():
        _fill_bias(bias_sc, tab_ref, bucket_ref[...], SWA_SCORE_HEADS)
        _fill_row(sink_sc, [sink_ref[h] for h in SWA_SCORE_HEADS])

        kprev_sc[...] = jnp.zeros(kprev_sc.shape, BF16)
        vprev_sc[...] = jnp.zeros(vprev_sc.shape, BF16)

    k_sc[0:BLOCK, :] = kprev_sc[...]
    v_sc[0:BLOCK, :] = vprev_sc[...]

    hn = _rms(h_ref[...], gmix_ref[...]).astype(BF16)
    p_ref[...] = jnp.dot(hn, win_ref[...], preferred_element_type=F32).astype(BF16)

    k_sc[BLOCK:, :] = p_ref[:, SWA_K0:SWA_V0]
    v_sc[BLOCK:, :] = p_ref[:, SWA_V0:SWA_M0]
    own, eye = _fold_masks(SWA_NQ)

    def block_body(b, carry):
        r0 = b * BLOCK
        has_prev = (i > 0) if b == 0 else None
        qb = p_ref[pl.ds(r0, BLOCK), 0:SWA_Q]
        slab = lambda x, s: x[:, s * LANES:(s + 1) * LANES]
        qa = jnp.concatenate([slab(qb, s) for s in range(4)], axis=0)
        qc = jnp.concatenate([slab(qb, s) for s in (4, 5)], axis=0)
        kb = k_sc[pl.ds(r0, 2 * BLOCK), :]
        vb = v_sc[pl.ds(r0, 2 * BLOCK), :]
        s0, s1 = _pair_scores(slab(kb, 0), qa)
        s2, s3 = _pair_scores(slab(kb, 1), qc)
        st = jnp.concatenate([s0, s1, s2, s3], axis=1)
        pt, _ = _band_softmax(st, bias_sc[...], own, eye, has_prev, sink_sc[...], None)
        na, nc = 4 * BLOCK, 2 * BLOCK
        oa = _pair_values(slab(vb, 0), pt[:, 0:na], pt[:, na:2 * na])
        oc = _pair_values(slab(vb, 1), pt[:, 2 * na:2 * na + nc], pt[:, 2 * na + nc:])
        outs = [slab(oa, s) for s in range(4)] + [slab(oc, s) for s in range(2)]
        outs += _mem_attention(p_ref[pl.ds(r0, BLOCK), SWA_M0:], kvm_ref[...])
        cat_sc[pl.ds(r0, BLOCK), :] = _to_rows(outs)
        return carry

    for b in range(BLOCKS_PER_TILE):
        block_body(b, None)

    kprev_sc[...] = p_ref[ROW_TILE - BLOCK:, SWA_K0:SWA_V0]
    vprev_sc[...] = p_ref[ROW_TILE - BLOCK:, SWA_V0:SWA_M0]

    out = h_ref[...] + jnp.dot(cat_sc[...], wout_ref[...], preferred_element_type=F32)
    h1_ref[...] = out
    hn1_ref[...] = _rms(out, g_ref[...]).astype(BF16)


def _swa_layer(w_in, g_mix, kvm, table, sinks, wout, h, g_ffn, wa, wb):
    assert SWA_WINDOW == BLOCK
    kv_w = 2 * LANES
    steps = SEQ // ROW_TILE
    return pl.pallas_call(
        _swa_kernel,
        grid=(SEQ // ROW_TILE,),
        in_specs=[
            pl.BlockSpec(memory_space=pltpu.SMEM),
            pl.BlockSpec(memory_space=pltpu.SMEM),
            pl.BlockSpec((BLOCK, BLOCK), lambda i: (0, 0)),
            pl.BlockSpec((1, D_MODEL), lambda i: (0, 0)),
            pl.BlockSpec((D_MODEL, SWA_PW), lambda i: (0, 0)),
            pl.BlockSpec((N_MEM, 2 * MEM_W), lambda i: (0, 0)),
            pl.BlockSpec((D_MODEL, D_MODEL), lambda i: (0, 0)),
            pl.BlockSpec((ROW_TILE, D_MODEL), lambda i: (i, 0)),
            pl.BlockSpec((1, D_MODEL), lambda i: (0, 0)),
            _cast_slice_spec(wa, steps), _cast_slice_spec(wb, steps),
        ],
        out_specs=[pl.BlockSpec((ROW_TILE, D_MODEL), lambda i: (i, 0)),
                   pl.BlockSpec((ROW_TILE, D_MODEL), lambda i: (i, 0)),
                   _cast_slice_spec(wa, steps), _cast_slice_spec(wb, steps)],
        out_shape=[jax.ShapeDtypeStruct((SEQ, D_MODEL), F32), jax.ShapeDtypeStruct((SEQ, D_MODEL), BF16),
                   jax.ShapeDtypeStruct(wa.shape, BF16), jax.ShapeDtypeStruct(wb.shape, BF16)],
        scratch_shapes=[pltpu.VMEM((BLOCK, SWA_NQ), F32),
                        pltpu.VMEM((1, SWA_NQ), F32),
                        pltpu.VMEM((ROW_TILE + BLOCK, kv_w), BF16),
                        pltpu.VMEM((ROW_TILE + BLOCK, kv_w), BF16),
                        pltpu.VMEM((ROW_TILE, D_MODEL), BF16),
                        pltpu.VMEM((ROW_TILE, SWA_PW), BF16),
                        pltpu.VMEM((BLOCK, kv_w), BF16),
                        pltpu.VMEM((BLOCK, kv_w), BF16)],
        compiler_params=pltpu.CompilerParams(dimension_semantics=("arbitrary",), vmem_limit_bytes=VMEM_LIMIT),
        name="swa_mixer",
    )(table, sinks, jnp.asarray(_folded_bucket_map(1)), g_mix.reshape(1, D_MODEL), w_in, kvm, wout, h,
      g_ffn.reshape(1, D_MODEL), wa, wb)


DIL_SLAB = 3 * DIL_W
TOK_W = DIL_SLAB + MEM_W
SLABS_PER_GROUP = DIL_SLAB // LANES


def _dilproj_kernel(h_ref, g_ref, w_ref, tok_ref, g1_ref, g2_ref, slab_sc):
    hn = _rms(h_ref[...], g_ref[...]).astype(BF16)
    res = jnp.dot(hn, w_ref[...], preferred_element_type=F32)
    tok_ref[...] = res[:, :TOK_W].astype(BF16)
    for s in range(2 * SLABS_PER_GROUP):
        slab_sc[s] = res[:, TOK_W + s * LANES:TOK_W + (s + 1) * LANES]
    for gi, out_ref in ((1, g1_ref), (2, g2_ref)):
        d = DIL_GROUPS[gi][1]
        for s in range(SLABS_PER_GROUP):
            for r in range(d):
                rows = slab_sc[(gi - 1) * SLABS_PER_GROUP + s, pl.ds(r, ROW_TILE // d, stride=d), :]
                out_ref[r, :, s * LANES:(s + 1) * LANES] = rows.astype(BF16)


def _dilproj(h, g, w):
    d1, d2 = DIL_GROUPS[1][1], DIL_GROUPS[2][1]
    return pl.pallas_call(
        _dilproj_kernel,
        grid=(SEQ // ROW_TILE,),
        in_specs=[pl.BlockSpec((ROW_TILE, D_MODEL), lambda i: (i, 0)),
                  pl.BlockSpec((1, D_MODEL), lambda i: (0, 0)),
                  pl.BlockSpec((D_MODEL, DIL_IN), lambda i: (0, 0))],
        out_specs=[pl.BlockSpec((ROW_TILE, TOK_W), lambda i: (i, 0)),
                   pl.BlockSpec((d1, ROW_TILE // d1, DIL_SLAB), lambda i: (0, i, 0)),
                   pl.BlockSpec((d2, ROW_TILE // d2, DIL_SLAB), lambda i: (0, i, 0))],
        out_shape=[jax.ShapeDtypeStruct((SEQ, TOK_W), BF16),
                   jax.ShapeDtypeStruct((d1, SEQ // d1, DIL_SLAB), BF16),
                   jax.ShapeDtypeStruct((d2, SEQ // d2, DIL_SLAB), BF16)],
        scratch_shapes=[pltpu.VMEM((2 * SLABS_PER_GROUP, ROW_TILE, LANES), F32)],
        compiler_params=pltpu.CompilerParams(dimension_semantics=("arbitrary",), vmem_limit_bytes=VMEM_LIMIT),
        name="dil_proj",
    )(h, g.reshape(1, D_MODEL), w)


def _dil_kernel(tab_ref, bucket_ref, q_ref, k_ref, v_ref, kp_ref, vp_ref, o_ref, l_ref,
                bias_sc, diag_sc, k_sc, v_sc, *, head0, diag_bucket):
    r = pl.program_id(0)
    n = pl.program_id(1)
    heads = [head0 + h for h in range(DIL_HEADS)]
    nq = DIL_HEADS * BLOCK

    @pl.when((r == 0) & (n == 0))
    def _():
        _fill_bias(bias_sc, tab_ref, bucket_ref[...], heads)
        _fill_row(diag_sc, [tab_ref[diag_bucket, h] for h in heads])

    k_sc[0:BLOCK, :] = kp_ref[...]
    k_sc[BLOCK:, :] = k_ref[...]
    v_sc[0:BLOCK, :] = vp_ref[...]
    v_sc[BLOCK:, :] = v_ref[...]
    own, eye = _fold_masks(nq)
    upper_rows = lax.broadcasted_iota(jnp.int32, (BLOCK, BLOCK), 0) < HEAD_DIM

    def block_body(b, carry):
        r0 = b * BLOCK
        has_prev = (n > 0) if b == 0 else None
        qb = q_ref[pl.ds(r0, BLOCK), :]
        kb = k_sc[pl.ds(r0, 2 * BLOCK), :]
        vb = v_sc[pl.ds(r0, 2 * BLOCK), :]
        slab = lambda x, s: x[:, s * LANES:(s + 1) * LANES]
        scores = []
        for s in range(DIL_W // LANES):
            scores += _pair_scores(slab(kb, s), slab(qb, s))
        pt, lse = _band_softmax(jnp.concatenate(scores, axis=1), bias_sc[...], own, eye, has_prev, None,
                                diag_sc[...])
        outs, lses = [], []
        for s in range(DIL_W // LANES):
            ca, cb = 2 * s * BLOCK, (2 * s + 1) * BLOCK
            outs.append(jnp.transpose(_pair_values(slab(vb, s), pt[:, ca:ca + BLOCK], pt[:, cb:cb + BLOCK])))
            lse_t = jnp.where(upper_rows, jnp.broadcast_to(lse[:, ca:ca + BLOCK], (BLOCK, BLOCK)),
                              jnp.broadcast_to(lse[:, cb:cb + BLOCK], (BLOCK, BLOCK)))
            lses.append(jnp.transpose(lse_t))
        o_ref[pl.ds(r0, BLOCK), :] = jnp.concatenate(outs, axis=-1)
        l_ref[pl.ds(r0, BLOCK), :] = jnp.concatenate(lses, axis=-1)
        return carry

    for b in range(BLOCKS_PER_TILE):
        block_body(b, None)


def _dil_group(qkv, table, gi):
    window, d = DIL_GROUPS[gi]
    rows = SEQ // d
    assert qkv.shape[:2] == (d, rows) and window // d == BLOCK
    prev = lambda n: jnp.maximum(n * BLOCKS_PER_TILE - 1, 0)
    tile = lambda c: pl.BlockSpec((None, ROW_TILE, DIL_W), lambda r, n: (r, n, c))
    prev_block = lambda c: pl.BlockSpec((None, BLOCK, DIL_W), lambda r, n: (r, prev(n), c))
    return pl.pallas_call(
        functools.partial(_dil_kernel, head0=gi * DIL_HEADS, diag_bucket=_diag_bucket(d)),
        grid=(d, rows // ROW_TILE),
        in_specs=[
            pl.BlockSpec(memory_space=pltpu.SMEM),
            pl.BlockSpec((BLOCK, BLOCK), lambda r, n: (0, 0)),
            tile(0), tile(1), tile(2), prev_block(1), prev_block(2),
        ],
        out_specs=[tile(0), tile(0)],
        out_shape=[jax.ShapeDtypeStruct((d, rows, DIL_W), F32), jax.ShapeDtypeStruct((d, rows, DIL_W), F32)],
        scratch_shapes=[pltpu.VMEM((BLOCK, DIL_HEADS * BLOCK), F32),
                        pltpu.VMEM((1, DIL_HEADS * BLOCK), F32),
                        pltpu.VMEM((ROW_TILE + BLOCK, DIL_W), BF16),
                        pltpu.VMEM((ROW_TILE + BLOCK, DIL_W), BF16)],
        compiler_params=pltpu.CompilerParams(dimension_semantics=("arbitrary", "arbitrary"),
                                             vmem_limit_bytes=VMEM_LIMIT),
        name=f"dil_attn_{gi}",
    )(table, jnp.asarray(_folded_bucket_map(d)), qkv, qkv, qkv, qkv, qkv)


def _dilout_kernel(o0_ref, o1_ref, o2_ref, l0_ref, l1_ref, l2_ref, qm_ref, kvm_ref, wout_ref, h_ref, g_ref,
                   router_ref, h2_ref, hn2_ref, meta_ref, cnt_ref, metat_ref, carry_sc, tok_sc):
    @pl.when(pl.program_id(0) == 0)
    def _():
        carry_sc[...] = jnp.zeros_like(carry_sc)

    for k, src_ref in enumerate((o1_ref, l1_ref, o2_ref, l2_ref)):
        d = src_ref.shape[0]
        for s in range(DIL_W // LANES):
            for r in range(d):
                tok_sc[k, s, pl.ds(r, ROW_TILE // d, stride=d), :] = src_ref[r, :, s * LANES:(s + 1) * LANES]

    mixed = []
    for s in range(DIL_W // LANES):
        cols = slice(s * LANES, (s + 1) * LANES)
        o0, o1, o2 = o0_ref[:, cols], tok_sc[0, s], tok_sc[2, s]
        l0, l1, l2 = l0_ref[:, cols], tok_sc[1, s], tok_sc[3, s]
        mx = jnp.maximum(jnp.maximum(l0, l1), l2)
        e0, e1, e2 = jnp.exp(l0 - mx), jnp.exp(l1 - mx), jnp.exp(l2 - mx)
        inv = 1.0 / (e0 + e1 + e2)
        mixed.append(((e0 * inv) * o0 + (e1 * inv) * o1 + (e2 * inv) * o2).astype(BF16))
    cat = jnp.concatenate(mixed + [_to_rows(_mem_attention(qm_ref[...], kvm_ref[...]))], axis=-1)
    out = h_ref[...] + jnp.dot(cat, wout_ref[...], preferred_element_type=F32)
    h2_ref[...] = out
    hn = _rms(out, g_ref[...]).astype(BF16)
    hn2_ref[...] = hn

    logits = _dot_nt(router_ref[...], hn)
    row = lax.broadcasted_iota(jnp.int32, logits.shape, 0)
    masked = jnp.where(row < N_EXPERTS, logits, -jnp.inf)
    v0 = jnp.max(masked, axis=0, keepdims=True)
    i0 = jnp.min(jnp.where(masked == v0, row, ROUTER_ROWS), axis=0, keepdims=True)
    rest = jnp.where(row == i0, -jnp.inf, masked)
    v1 = jnp.max(rest, axis=0, keepdims=True)
    i1 = jnp.min(jnp.where(rest == v1, row, ROUTER_ROWS), axis=0, keepdims=True)
    ex = jnp.exp(v1 - v0)
    inv2 = 1.0 / (1.0 + ex)

    oh0, oh1 = row == i0, row == i1
    sel = (oh0 | oh1).astype(F32)
    tok_r = lax.broadcasted_iota(jnp.int32, (ROW_TILE, ROW_TILE), 0)
    tok_c = lax.broadcasted_iota(jnp.int32, (ROW_TILE, ROW_TILE), 1)
    earlier = (tok_r < tok_c).astype(F32).astype(BF16)
    carry = carry_sc[:, 0:1]
    before = jnp.dot(sel.astype(BF16), earlier, preferred_element_type=F32) + carry
    rank0 = jnp.sum(jnp.where(oh0, before, 0.0), axis=0, keepdims=True)
    rank1 = jnp.sum(jnp.where(oh1, before, 0.0), axis=0, keepdims=True)
    count = jnp.broadcast_to(carry + jnp.sum(sel, axis=1, keepdims=True), carry_sc.shape)
    carry_sc[...] = count
    cnt_ref[...] = count[:SUBLANES]

    fields = (i0.astype(F32), i1.astype(F32), inv2, ex * inv2, rank0, rank1)
    meta_t = jnp.zeros((LANES, ROW_TILE), F32)
    frow = lax.broadcasted_iota(jnp.int32, meta_t.shape, 0)
    for k, f in enumerate(fields):
        meta_t = jnp.where(frow == k, f, meta_t)
    metat_ref[...] = meta_t[:SUBLANES]
    meta_ref[...] = jnp.transpose(meta_t)


META_E0, META_E1, META_G0, META_G1, META_RANK0, META_RANK1 = range(6)
ROUTER_ROWS = 16


def _dilout(os, ls, tok, kvm, wout, h, g_ffn, router):
    row = lambda i: (i, 0)
    const = lambda i: (0, 0)

    def group_spec(a):
        d = a.shape[0]
        if d == 1:
            return pl.BlockSpec((None, ROW_TILE, DIL_W), lambda i: (0, i, 0))
        return pl.BlockSpec((d, ROW_TILE // d, DIL_W), lambda i: (0, i, 0))

    return pl.pallas_call(
        _dilout_kernel,
        grid=(SEQ // ROW_TILE,),
        in_specs=[group_spec(a) for a in (*os, *ls)] + [
                  pl.BlockSpec((ROW_TILE, MEM_W), lambda i: (i, DIL_SLAB // MEM_W)),
                  pl.BlockSpec((N_MEM, 2 * MEM_W), lambda i: (0, 1)),
                  pl.BlockSpec((DIL_W + MEM_W, D_MODEL), const),
                  pl.BlockSpec((ROW_TILE, D_MODEL), row),
                  pl.BlockSpec((1, D_MODEL), const),
                  pl.BlockSpec((ROUTER_ROWS, D_MODEL), const)],
        out_specs=[pl.BlockSpec((ROW_TILE, D_MODEL), row),
                   pl.BlockSpec((ROW_TILE, D_MODEL), row),
                   pl.BlockSpec((ROW_TILE, LANES), row),
                   pl.BlockSpec((SUBLANES, LANES), const),
                   pl.BlockSpec((SUBLANES, ROW_TILE), lambda i: (0, i))],
        out_shape=[jax.ShapeDtypeStruct((SEQ, D_MODEL), F32), jax.ShapeDtypeStruct((SEQ, D_MODEL), BF16),
                   jax.ShapeDtypeStruct((SEQ, LANES), F32), jax.ShapeDtypeStruct((SUBLANES, LANES), F32),
                   jax.ShapeDtypeStruct((SUBLANES, SEQ), F32)],
        scratch_shapes=[pltpu.VMEM((ROUTER_ROWS, LANES), F32),
                        pltpu.VMEM((4, DIL_W // LANES, ROW_TILE, LANES), F32)],
        compiler_params=pltpu.CompilerParams(dimension_semantics=("arbitrary",), vmem_limit_bytes=VMEM_LIMIT),
        name="dil_out",
    )(*os, *ls, tok, kvm, wout, h, g_ffn.reshape(1, D_MODEL), router)


def _swiglu_partial(x, wg, wu, wd):
    a = jnp.dot(x, wg, preferred_element_type=F32)
    b = jnp.dot(x, wu, preferred_element_type=F32)
    hm = (a * jax.nn.sigmoid(a) * b).astype(BF16)
    return jnp.dot(hm, wd, preferred_element_type=F32)


def _ffn_kernel(x_ref, wg_ref, wu_ref, wd_ref, h_ref, wa_ref, o_ref, wa_out):
    wa_out[...] = wa_ref[...].astype(BF16)
    x = x_ref[...]
    acc = h_ref[...]
    for c in range(D_FF // FFN_COL_TILE):
        cols = slice(c * FFN_COL_TILE, (c + 1) * FFN_COL_TILE)
        acc = acc + _swiglu_partial(x, wg_ref[:, cols], wu_ref[:, cols], wd_ref[cols, :])
    o_ref[...] = acc


def _ffn(x, h, wg, wu, wd, wa):
    tm = ROW_TILE
    steps = SEQ // tm
    resident = lambda shape: pl.BlockSpec(shape, lambda i: (0, 0), pipeline_mode=pl.Buffered(1))
    return pl.pallas_call(
        _ffn_kernel,
        grid=(steps,),
        in_specs=[pl.BlockSpec((tm, D_MODEL), lambda i: (i, 0)),
                  resident((D_MODEL, D_FF)), resident((D_MODEL, D_FF)), resident((D_FF, D_MODEL)),
                  pl.BlockSpec((tm, D_MODEL), lambda i: (i, 0)),
                  _cast_slice_spec(wa, steps)],
        out_specs=[pl.BlockSpec((tm, D_MODEL), lambda i: (i, 0)), _cast_slice_spec(wa, steps)],
        out_shape=[jax.ShapeDtypeStruct((SEQ, D_MODEL), F32), jax.ShapeDtypeStruct(wa.shape, BF16)],
        compiler_params=pltpu.CompilerParams(dimension_semantics=("arbitrary",), vmem_limit_bytes=VMEM_LIMIT),
        name="ffn",
    )(x, wg, wu, wd, h, wa)


MOE_SUB = 512
MOE_ROWS = 2048
MOE_COL_TILE = D_FF // 2
MOE_SUBS = MOE_ROWS // MOE_SUB
MOE_SLOTS = 2 * SEQ + N_EXPERTS * MOE_SUB
MOE_TILES = MOE_SLOTS // MOE_ROWS
MOE_ITEMS = MOE_TILES + N_EXPERTS - 1
PACK_W = D_MODEL // 2
ITEM_TILE, ITEM_EXPERT, ITEM_LO, ITEM_HI, ITEM_FIRST = range(5)


def _pack_bf16_pairs(x):
    lo = lax.bitcast_convert_type(x[:, :PACK_W].astype(F32), jnp.uint32) >> 16
    hi = lax.bitcast_convert_type(x[:, PACK_W:].astype(F32), jnp.uint32) & jnp.uint32(0xFFFF0000)
    return hi | lo


def _unpack_bf16_pairs(pk):
    lo = lax.bitcast_convert_type(pk << 16, F32).astype(BF16)
    hi = lax.bitcast_convert_type(pk & jnp.uint32(0xFFFF0000), F32).astype(BF16)
    return lo, hi


def _vmem_row(ref, group, sub):
    return ref.at[group, pl.ds(sub, 1), :]


def _hbm_row(ref, row):
    return ref.at[pl.ds(row, 1), :]


def _scatter_kernel(pos0_ref, pos1_ref, x_ref, xs_in_ref, xs_ref, pk_sc, sem):
    del xs_in_ref
    pk_sc[...] = _pack_bf16_pairs(x_ref[...]).reshape(pk_sc.shape)

    def issue(g, c):
        for u in range(SUBLANES):
            t = g * SUBLANES + u
            pltpu.make_async_copy(_vmem_row(pk_sc, g, u), _hbm_row(xs_ref, pos0_ref[t]), sem).start(priority=0)
            pltpu.make_async_copy(_vmem_row(pk_sc, g, u), _hbm_row(xs_ref, pos1_ref[t]), sem).start(priority=1)
        return c

    def drain(g, c):
        for _ in range(2 * SUBLANES):
            pltpu.make_async_copy(_vmem_row(pk_sc, 0, 0), _hbm_row(xs_ref, 0), sem).wait()
        return c

    lax.fori_loop(0, ROW_TILE // SUBLANES, issue, 0)
    lax.fori_loop(0, ROW_TILE // SUBLANES, drain, 0)


def _moe_scatter(x, pos0, pos1):
    smem_tile = pl.BlockSpec((ROW_TILE,), lambda i: (i,), memory_space=pltpu.SMEM)
    return pl.pallas_call(
        _scatter_kernel,
        grid=(SEQ // ROW_TILE,),
        in_specs=[smem_tile, smem_tile,
                  pl.BlockSpec((ROW_TILE, D_MODEL), lambda i: (i, 0)),
                  pl.BlockSpec(memory_space=pl.ANY)],
        out_specs=pl.BlockSpec(memory_space=pl.ANY),
        out_shape=jax.ShapeDtypeStruct((MOE_SLOTS, PACK_W), jnp.uint32),
        scratch_shapes=[pltpu.VMEM((ROW_TILE // SUBLANES, SUBLANES, PACK_W), jnp.uint32),
                        pltpu.SemaphoreType.DMA],
        input_output_aliases={3: 0},
        compiler_params=pltpu.CompilerParams(dimension_semantics=("arbitrary",), vmem_limit_bytes=VMEM_LIMIT),
        name="moe_scatter",
    )(pos0, pos1, x, jnp.zeros((MOE_SLOTS, PACK_W), jnp.uint32))


def _moe_kernel(items_ref, xs_ref, wg_ref, wu_ref, wd_ref, y_ref, xb_sc):
    w = pl.program_id(0)
    j = pl.program_id(1)
    lo = items_ref[ITEM_LO, w]
    hi = items_ref[ITEM_HI, w]

    @pl.when((j == 0) & (hi > lo))
    def _():
        xlo, xhi = _unpack_bf16_pairs(xs_ref[...])
        xb_sc[:, :PACK_W] = xlo
        xb_sc[:, PACK_W:] = xhi

    @pl.when((j == 0) & (items_ref[ITEM_FIRST, w] == 1))
    def _():
        y_ref[...] = jnp.zeros_like(y_ref)

    for sb in range(MOE_SUBS):
        @pl.when((lo <= sb) & (sb < hi))
        def _(sb=sb):
            rows = slice(sb * MOE_SUB, (sb + 1) * MOE_SUB)
            x = xb_sc[rows, :]
            acc = y_ref[rows, :]
            for c0 in range(0, MOE_COL_TILE, FFN_COL_TILE):
                cols = slice(c0, min(c0 + FFN_COL_TILE, MOE_COL_TILE))
                acc = acc + _swiglu_partial(x, wg_ref[0, :, cols], wu_ref[0, :, cols], wd_ref[0, cols, :])
            y_ref[rows, :] = acc


def _moe_grouped(items, xs, wg, wu, wd):
    tf = MOE_COL_TILE
    nf = D_FF // tf
    jf = lambda w, j, it: jnp.where(it[ITEM_HI, w] > it[ITEM_LO, w], j, nf - 1)
    grid_spec = pltpu.PrefetchScalarGridSpec(
        num_scalar_prefetch=1,
        grid=(MOE_ITEMS, nf),
        in_specs=[pl.BlockSpec((MOE_ROWS, PACK_W), lambda w, j, it: (it[ITEM_TILE, w], 0)),
                  pl.BlockSpec((1, D_MODEL, tf), lambda w, j, it: (it[ITEM_EXPERT, w], 0, jf(w, j, it))),
                  pl.BlockSpec((1, D_MODEL, tf), lambda w, j, it: (it[ITEM_EXPERT, w], 0, jf(w, j, it))),
                  pl.BlockSpec((1, tf, D_MODEL), lambda w, j, it: (it[ITEM_EXPERT, w], jf(w, j, it), 0))],
        out_specs=pl.BlockSpec((MOE_ROWS, D_MODEL), lambda w, j, it: (it[ITEM_TILE, w], 0)),
        scratch_shapes=[pltpu.VMEM((MOE_ROWS, D_MODEL), BF16)],
    )
    return pl.pallas_call(
        _moe_kernel,
        grid_spec=grid_spec,
        out_shape=jax.ShapeDtypeStruct((MOE_SLOTS, D_MODEL), F32),
        compiler_params=pltpu.CompilerParams(dimension_semantics=("arbitrary", "arbitrary"),
                                             vmem_limit_bytes=VMEM_LIMIT),
        name="moe_grouped",
    )(items, xs, wg, wu, wd)


def _combine_kernel(pos0_ref, pos1_ref, meta_ref, h_ref, gf_ref, y_ref, o_ref, y0_sc, y1_sc, sem):
    def issue(g, c):
        for u in range(SUBLANES):
            t = g * SUBLANES + u
            pltpu.make_async_copy(_hbm_row(y_ref, pos0_ref[t]), _vmem_row(y0_sc, g, u), sem).start(priority=0)
            pltpu.make_async_copy(_hbm_row(y_ref, pos1_ref[t]), _vmem_row(y1_sc, g, u), sem).start(priority=1)
        return c

    def drain(g, c):
        for _ in range(SUBLANES):
            pltpu.make_async_copy(_hbm_row(y_ref, 0), _vmem_row(y0_sc, 0, 0), sem).wait()
            pltpu.make_async_copy(_hbm_row(y_ref, 0), _vmem_row(y1_sc, 0, 0), sem).wait()
        return c

    lax.fori_loop(0, ROW_TILE // SUBLANES, issue, 0)
    lax.fori_loop(0, ROW_TILE // SUBLANES, drain, 0)
    meta = meta_ref[...]
    g0 = meta[:, META_G0:META_G0 + 1]
    g1 = meta[:, META_G1:META_G1 + 1]
    y0 = y0_sc[...].reshape(ROW_TILE, D_MODEL)
    y1 = y1_sc[...].reshape(ROW_TILE, D_MODEL)
    o_ref[...] = _rms(h_ref[...] + (g0 * y0 + g1 * y1), gf_ref[...])


def _moe_combine(pos0, pos1, meta, h, g_final, y):
    smem_tile = pl.BlockSpec((ROW_TILE,), lambda i: (i,), memory_space=pltpu.SMEM)
    tiles = (ROW_TILE // SUBLANES, SUBLANES, D_MODEL)
    return pl.pallas_call(
        _combine_kernel,
        grid=(SEQ // ROW_TILE,),
        in_specs=[smem_tile, smem_tile,
                  pl.BlockSpec((ROW_TILE, LANES), lambda i: (i, 0)),
                  pl.BlockSpec((ROW_TILE, D_MODEL), lambda i: (i, 0)),
                  pl.BlockSpec((1, D_MODEL), lambda i: (0, 0)),
                  pl.BlockSpec(memory_space=pl.ANY)],
        out_specs=pl.BlockSpec((ROW_TILE, D_MODEL), lambda i: (i, 0)),
        out_shape=jax.ShapeDtypeStruct((SEQ, D_MODEL), F32),
        scratch_shapes=[pltpu.VMEM(tiles, F32), pltpu.VMEM(tiles, F32), pltpu.SemaphoreType.DMA],
        compiler_params=pltpu.CompilerParams(dimension_semantics=("arbitrary",), vmem_limit_bytes=VMEM_LIMIT),
        name="moe_combine",
    )(pos0, pos1, meta, h, g_final.reshape(1, D_MODEL), y)


def _moe_plan(meta_t, counts):
    i32 = jnp.int32
    cnt = counts[:N_EXPERTS, 0].astype(i32)
    padded = (cnt + MOE_SUB - 1) // MOE_SUB * MOE_SUB
    start = jnp.cumsum(padded) - padded

    def slot(e_row, rank_row):
        e = meta_t[e_row].astype(i32)
        base = sum(jnp.where(e == k, start[k], 0) for k in range(N_EXPERTS))
        return base + meta_t[rank_row].astype(i32)

    pos0, pos1 = slot(META_E0, META_RANK0), slot(META_E1, META_RANK1)

    first_sb, end_sb = start // MOE_SUB, (start + padded) // MOE_SUB
    t0 = jnp.arange(MOE_TILES, dtype=i32)[:, None] * MOE_SUBS
    lo = jnp.clip(first_sb[None, :] - t0, 0, MOE_SUBS)
    hi = jnp.clip(end_sb[None, :] - t0, 0, MOE_SUBS)
    active = hi > lo
    none = ~jnp.any(active, axis=1, keepdims=True)
    active = jnp.concatenate([active, none], axis=1)
    lo = jnp.concatenate([lo, jnp.zeros_like(t0)], axis=1)
    hi = jnp.concatenate([hi, jnp.zeros_like(t0)], axis=1)
    first = active & (jnp.cumsum(active.astype(i32), axis=1) == 1)
    ncol = N_EXPERTS + 1
    flat = jnp.arange(MOE_TILES * ncol, dtype=i32)
    key = jnp.where(active.reshape(-1), flat, MOE_TILES * ncol)
    order = jnp.argsort(key)[:MOE_ITEMS]
    live = key[order] < MOE_TILES * ncol
    tile = jnp.where(live, order // ncol, MOE_TILES - 1)
    lo_i = jnp.where(live, lo.reshape(-1)[order], 0)
    hi_i = jnp.where(live, hi.reshape(-1)[order], 0)
    first_i = jnp.where(live, first.reshape(-1)[order], False).astype(i32)
    idx = jnp.arange(MOE_ITEMS, dtype=i32)
    src = jnp.maximum(lax.cummax(jnp.where(hi_i > lo_i, idx, -1), axis=0), 0)
    expert = jnp.minimum(order % ncol, N_EXPERTS - 1)[src]
    items = jnp.stack([tile, expert, lo_i, hi_i, first_i]).astype(i32)
    return pos0, pos1, items


def _take_runs(w, idx, axis):
    idx = list(idx)
    runs, start = [], 0
    for i in range(1, len(idx) + 1):
        if i == len(idx) or idx[i] != idx[i - 1] + 1:
            runs.append(lax.slice_in_dim(w, idx[start], idx[i - 1] + 1, axis=axis))
            start = i
    return jnp.concatenate(runs, axis=axis)


def _swa_weight_layout():
    head = lambda base, j: list(range(base + j * HEAD_DIM, base + (j + 1) * HEAD_DIM))
    k0, v0, m0 = SWA_Q, SWA_Q + SWA_KV, SWA_Q + 2 * SWA_KV
    cols, rows = [], []
    for a, b in SWA_PAIR_SLABS:
        cols += head(0, a) + head(0, b)
        rows += head(0, a) + head(0, b)
    for base in (k0, v0):
        cols += head(base, 0) + head(base, 1) + head(base, 2) + head(base, 2)
    cols += list(range(m0, m0 + MEM_W))
    rows += list(range(SWA_Q, SWA_Q + MEM_W))
    scale = np.ones((len(cols),), np.float32)
    scale[:SWA_Q] = Q_SCALE
    scale[SWA_M0:] = Q_SCALE
    assert len(cols) == SWA_PW and len(rows) == D_MODEL
    return np.asarray(cols), scale, np.asarray(rows)


def _dil_weight_layout():
    n_grp = len(DIL_GROUPS) * DIL_SLAB
    cols = list(range(DIL_SLAB)) + list(range(n_grp, n_grp + MEM_W)) + list(range(DIL_SLAB, n_grp))
    scale = np.ones((len(cols),), np.float32)
    for c0 in (0, DIL_SLAB, TOK_W, TOK_W + DIL_SLAB):
        scale[c0:c0 + DIL_W] = Q_SCALE
    return np.asarray(cols), scale


def kernel(x, mem, rel_bias_table, mem_norm, norm_mix, norm_ffn, final_norm, swa_w_in, swa_sinks, swa_w_mem_kv,
           swa_w_out, dil_w_in, dil_w_mem_kv, dil_w_out, ffn_gate, ffn_up, ffn_down, router, moe_gate, moe_up,
           moe_down):
    assert x.shape == (1, SEQ, D_MODEL) and mem.shape == (1, N_MEM, D_MODEL)
    assert norm_mix.shape == (2, D_MODEL) and swa_w_in.shape == (1, D_MODEL, SWA_IN)
    assert dil_w_in.shape == (1, D_MODEL, DIL_IN) and moe_gate.shape == (1, N_EXPERTS, D_MODEL, D_FF)
    bf = lambda a: a.astype(BF16)
    h0 = x.reshape(SEQ, D_MODEL)

    kvm = _memkv(mem[0], mem_norm, bf(jnp.concatenate([swa_w_mem_kv[0], dil_w_mem_kv[0]], axis=1)))

    cols0, scale0, rows0 = _swa_weight_layout()
    h1, hn1, moe_up_b, moe_down_b = _swa_layer(bf(_take_runs(swa_w_in[0], cols0, 1) * scale0), norm_mix[0], kvm,
                                               rel_bias_table, swa_sinks[0],
                                               bf(_take_runs(swa_w_out[0], rows0, 0)), h0, norm_ffn[0],
                                               moe_up[0], moe_down[0])
    h2, moe_gate_b = _ffn(hn1, h1, bf(ffn_gate[0]), bf(ffn_up[0]), bf(ffn_down[0]), moe_gate[0])

    cols1, scale1 = _dil_weight_layout()
    tok, g1, g2 = _dilproj(h2, norm_mix[1], bf(_take_runs(dil_w_in[0], cols1, 1) * scale1))
    groups = [_dil_group(qkv, rel_bias_table, gi) for gi, qkv in enumerate((tok[None], g1, g2))]
    router_p = jnp.pad(bf(router[0].T), ((0, ROUTER_ROWS - N_EXPERTS), (0, 0)))
    h3, hn3, meta, counts, meta_t = _dilout([g[0] for g in groups], [g[1] for g in groups], tok, kvm,
                                            bf(dil_w_out[0]), h2, norm_ffn[1], router_p)
    pos0, pos1, items = _moe_plan(meta_t, counts)
    xs = _moe_scatter(hn3, pos0, pos1)
    y = _moe_grouped(items, xs, moe_gate_b, moe_up_b, moe_down_b)
    out = _moe_combine(pos0, pos1, meta, h3, final_norm, y)
    return out.reshape(1, SEQ, D_MODEL)
```

```python
import functools
import math

import jax
import jax.numpy as jnp
import numpy as np
from jax import lax
from jax.experimental import pallas as pl
from jax.experimental.pallas import tpu as pltpu

F32 = jnp.float32
BF16 = jnp.bfloat16

D_MODEL = 1024
SEQ = 16384
HEAD_DIM = 64
N_MIX_HEADS = 12
SWA_KV_HEADS = 3
SWA_GROUP = N_MIX_HEADS // SWA_KV_HEADS
SWA_WINDOW = 128
DIL_GROUPS = ((128, 1), (512, 4), (2048, 16))
DIL_HEADS = 4
N_MEM = 256
MEM_HEADS = 4
BLOCK = 128
N_BUCKETS = 32
MAX_DISTANCE = 2048
D_FF = 3584
N_EXPERTS = 8
EPS = 1e-5
NEG = -1e30
Q_SCALE = HEAD_DIM ** -0.5

SWA_Q = N_MIX_HEADS * HEAD_DIM
SWA_KV = SWA_KV_HEADS * HEAD_DIM
SWA_IN = SWA_Q + 2 * SWA_KV + MEM_HEADS * HEAD_DIM
DIL_W = DIL_HEADS * HEAD_DIM
DIL_IN = len(DIL_GROUPS) * 3 * DIL_W + MEM_HEADS * HEAD_DIM
MEM_W = MEM_HEADS * HEAD_DIM

LANES = 128
SUBLANES = 8
VMEM_LIMIT = 56 * 1024 * 1024

ROW_TILE = 512
BLOCKS_PER_TILE = ROW_TILE // BLOCK
FFN_COL_TILE = 512


def _bucket_map(dil):
    qi = np.arange(BLOCK)[:, None]
    kj = np.arange(2 * BLOCK)[None, :]
    d = np.maximum((qi + BLOCK - kj) * dil, 0)
    max_exact = N_BUCKETS // 2
    ratio = np.maximum(d, 1).astype(np.float32) / np.float32(max_exact)
    large = max_exact + (np.log(ratio) / np.float32(math.log(MAX_DISTANCE / max_exact))
                         * np.float32(N_BUCKETS - max_exact)).astype(np.int32)
    return np.where(d < max_exact, d, np.minimum(large, N_BUCKETS - 1)).astype(np.int32)


def _rms(x, g):
    ms = jnp.mean(x * x, axis=-1, keepdims=True)
    return x * lax.rsqrt(ms + EPS) * g


def _dot_nt(a, b):
    return lax.dot_general(a, b, (((1,), (1,)), ((), ())), preferred_element_type=F32)


def _dot_tn(a, b):
    return lax.dot_general(a, b, (((0,), (0,)), ((), ())), preferred_element_type=F32)


def _fill_bias(bias_sc, tab_ref, bucket, heads):
    for j, h in enumerate(heads):
        def body(k, b, h=h):
            return jnp.where(bucket == k, tab_ref[k, h], b)
        bias_sc[:, j * BLOCK:(j + 1) * BLOCK] = lax.fori_loop(0, N_BUCKETS, body, jnp.zeros(bucket.shape, F32))


def _fill_row(row_sc, values):
    blk = lax.broadcasted_iota(jnp.int32, row_sc.shape, 1) // BLOCK
    row = jnp.zeros(row_sc.shape, F32)
    for j, v in enumerate(values):
        row = jnp.where(blk == j, v, row)
    row_sc[...] = row


def _folded_bucket_map(dil):
    full = _bucket_map(dil)
    qi = np.arange(BLOCK)[:, None]
    c = np.arange(BLOCK)[None, :]
    return np.ascontiguousarray(np.where(c <= qi, full[:, BLOCK:], full[:, :BLOCK]).T.astype(np.int32))


def _diag_bucket(dil):
    return int(_bucket_map(dil)[0, 0])


def _fold_masks(n):
    c = lax.broadcasted_iota(jnp.int32, (BLOCK, n), 0)
    qi = lax.broadcasted_iota(jnp.int32, (BLOCK, n), 1) % BLOCK
    return c <= qi, c == qi


def _pair_scores(k_slab, q_rows):
    lane = lax.broadcasted_iota(jnp.int32, q_rows.shape, 1)
    zero = jnp.zeros_like(q_rows)
    qa = jnp.where(lane < HEAD_DIM, q_rows, zero)
    qb = jnp.where(lane < HEAD_DIM, zero, q_rows)
    return _dot_nt(k_slab, qa), _dot_nt(k_slab, qb)


def _pair_values(v_slab, p_a, p_b):
    n = p_a.shape[1]
    o = _dot_tn(v_slab, jnp.concatenate([p_a, p_b], axis=1))
    row = lax.broadcasted_iota(jnp.int32, (LANES, n), 0)
    return jnp.where(row < HEAD_DIM, o[:, :n], o[:, n:])


def _band_softmax(st, bias, own, eye, has_prev, sink, diag_bias):
    s_prev, s_own = st[:BLOCK], st[BLOCK:]
    t = jnp.where(own, s_own, s_prev) + bias
    if has_prev is not None:
        t = jnp.where(own | has_prev, t, NEG)
    m = jnp.max(t, axis=0, keepdims=True)
    if diag_bias is not None:
        s_d = jnp.sum(jnp.where(eye, s_prev, 0.0), axis=0, keepdims=True) + diag_bias
        if has_prev is not None:
            s_d = jnp.where(has_prev, s_d, NEG)
        m = jnp.maximum(m, s_d)
    if sink is not None:
        m = jnp.maximum(m, sink)
    e = jnp.exp(t - m)
    den = jnp.sum(e, axis=0, keepdims=True)
    if diag_bias is not None:
        e_d = jnp.exp(s_d - m)
        den = den + e_d
    if sink is not None:
        den = den + jnp.exp(sink - m)
    inv = 1.0 / den
    p = e * inv
    p_own = jnp.where(own, p, 0.0)
    p_prev = jnp.where(own, 0.0, p)
    if diag_bias is not None:
        p_prev = jnp.where(eye, e_d * inv, p_prev)
    return jnp.concatenate([p_prev, p_own], axis=0).astype(BF16), m + jnp.log(den)


def _mem_attention(qm, kvm):
    outs = []
    for s in range(MEM_W // LANES):
        sa, sb = _pair_scores(kvm[:, s * LANES:(s + 1) * LANES], qm[:, s * LANES:(s + 1) * LANES])
        ps = []
        for st in (sa, sb):
            e = jnp.exp(st - jnp.max(st, axis=0, keepdims=True))
            ps.append((e * (1.0 / jnp.sum(e, axis=0, keepdims=True))).astype(BF16))
        outs.append(_pair_values(kvm[:, MEM_W + s * LANES:MEM_W + (s + 1) * LANES], ps[0], ps[1]))
    return outs


def _to_rows(slabs_t):
    return jnp.concatenate([jnp.transpose(x).astype(BF16) for x in slabs_t], axis=-1)


def _memkv_kernel(mem_ref, g_ref, w_ref, o_ref):
    mn = _rms(mem_ref[...], g_ref[...]).astype(BF16)
    o_ref[...] = jnp.dot(mn, w_ref[...], preferred_element_type=F32).astype(BF16)


def _memkv(mem, g, w):
    return pl.pallas_call(
        _memkv_kernel,
        out_shape=jax.ShapeDtypeStruct((N_MEM, w.shape[1]), BF16),
        name="mem_kv",
    )(mem, g.reshape(1, D_MODEL), w)


SWA_PAIR_SLABS = [(g, SWA_GROUP + g) for g in range(SWA_GROUP)] + [(8, 9), (10, 11)]
SWA_K0 = SWA_Q
SWA_V0 = SWA_K0 + 2 * LANES
SWA_M0 = SWA_V0 + 2 * LANES
SWA_PW = SWA_M0 + MEM_W
SWA_SCORE_HEADS = [0, 1, 2, 3, 4, 5, 6, 7, 8, 10, 9, 11]
SWA_NQ = N_MIX_HEADS * BLOCK


def _cast_slice_spec(w, steps):
    e, r, c = w.shape
    per_e = steps // e
    assert steps % e == 0 and r % per_e == 0 and (r // per_e) % 16 == 0
    return pl.BlockSpec((1, r // per_e, c), lambda i: (i // per_e, i % per_e, 0))


def _swa_kernel(*refs, n_cast):
    (tab_ref, sink_ref, bucket_ref, gmix_ref, win_ref, kvm_ref, wout_ref, h_ref, g_ref), refs = refs[:9], refs[9:]
    cast_in, (h1_ref, hn1_ref), refs = refs[:n_cast], refs[n_cast:n_cast + 2], refs[n_cast + 2:]
    cast_out, (bias_sc, sink_sc, k_sc, v_sc, cat_sc, p_ref, kprev_sc, vprev_sc) = refs[:n_cast], refs[n_cast:]
    i = pl.program_id(0)
    for src, dst in zip(cast_in, cast_out):
        dst[...] = src[...].astype(BF16)

    @pl.when(i == 0)
    def _():
        _fill_bias(bias_sc, tab_ref, bucket_ref[...], SWA_SCORE_HEADS)
        _fill_row(sink_sc, [sink_ref[h] for h in SWA_SCORE_HEADS])

        kprev_sc[...] = jnp.zeros(kprev_sc.shape, BF16)
        vprev_sc[...] = jnp.zeros(vprev_sc.shape, BF16)

    k_sc[0:BLOCK, :] = kprev_sc[...]
    v_sc[0:BLOCK, :] = vprev_sc[...]

    hn = _rms(h_ref[...], gmix_ref[...]).astype(BF16)
    p_ref[...] = jnp.dot(hn, win_ref[...], preferred_element_type=F32).astype(BF16)

    k_sc[BLOCK:, :] = p_ref[:, SWA_K0:SWA_V0]
    v_sc[BLOCK:, :] = p_ref[:, SWA_V0:SWA_M0]
    own, eye = _fold_masks(SWA_NQ)

    def block_body(b, carry):
        r0 = b * BLOCK
        has_prev = (i > 0) if b == 0 else None
        qb = p_ref[pl.ds(r0, BLOCK), 0:SWA_Q]
        slab = lambda x, s: x[:, s * LANES:(s + 1) * LANES]
        qa = jnp.concatenate([slab(qb, s) for s in range(4)], axis=0)
        qc = jnp.concatenate([slab(qb, s) for s in (4, 5)], axis=0)
        kb = k_sc[pl.ds(r0, 2 * BLOCK), :]
        vb = v_sc[pl.ds(r0, 2 * BLOCK), :]
        s0, s1 = _pair_scores(slab(kb, 0), qa)
        s2, s3 = _pair_scores(slab(kb, 1), qc)
        st = jnp.concatenate([s0, s1, s2, s3], axis=1)
        pt, _ = _band_softmax(st, bias_sc[...], own, eye, has_prev, sink_sc[...], None)
        na, nc = 4 * BLOCK, 2 * BLOCK
        oa = _pair_values(slab(vb, 0), pt[:, 0:na], pt[:, na:2 * na])
        oc = _pair_values(slab(vb, 1), pt[:, 2 * na:2 * na + nc], pt[:, 2 * na + nc:])
        outs = [slab(oa, s) for s in range(4)] + [slab(oc, s) for s in range(2)]
        outs += _mem_attention(p_ref[pl.ds(r0, BLOCK), SWA_M0:], kvm_ref[...])
        cat_sc[pl.ds(r0, BLOCK), :] = _to_rows(outs)
        return carry

    for b in range(BLOCKS_PER_TILE):
        block_body(b, None)

    kprev_sc[...] = p_ref[ROW_TILE - BLOCK:, SWA_K0:SWA_V0]
    vprev_sc[...] = p_ref[ROW_TILE - BLOCK:, SWA_V0:SWA_M0]

    out = h_ref[...] + jnp.dot(cat_sc[...], wout_ref[...], preferred_element_type=F32)
    h1_ref[...] = out
    hn1_ref[...] = _rms(out, g_ref[...]).astype(BF16)


def _swa_layer(w_in, g_mix, kvm, table, sinks, wout, h, g_ffn, casts):
    assert SWA_WINDOW == BLOCK
    kv_w = 2 * LANES
    steps = SEQ // ROW_TILE
    cast_specs = [_cast_slice_spec(w, steps) for w in casts]
    return pl.pallas_call(
        functools.partial(_swa_kernel, n_cast=len(casts)),
        grid=(SEQ // ROW_TILE,),
        in_specs=[
            pl.BlockSpec(memory_space=pltpu.SMEM),
            pl.BlockSpec(memory_space=pltpu.SMEM),
            pl.BlockSpec((BLOCK, BLOCK), lambda i: (0, 0)),
            pl.BlockSpec((1, D_MODEL), lambda i: (0, 0)),
            pl.BlockSpec((D_MODEL, SWA_PW), lambda i: (0, 0)),
            pl.BlockSpec((N_MEM, 2 * MEM_W), lambda i: (0, 0)),
            pl.BlockSpec((D_MODEL, D_MODEL), lambda i: (0, 0)),
            pl.BlockSpec((ROW_TILE, D_MODEL), lambda i: (i, 0)),
            pl.BlockSpec((1, D_MODEL), lambda i: (0, 0)),
        ] + cast_specs,
        out_specs=[pl.BlockSpec((ROW_TILE, D_MODEL), lambda i: (i, 0)),
                   pl.BlockSpec((ROW_TILE, D_MODEL), lambda i: (i, 0))] + cast_specs,
        out_shape=[jax.ShapeDtypeStruct((SEQ, D_MODEL), F32), jax.ShapeDtypeStruct((SEQ, D_MODEL), BF16)]
                  + [jax.ShapeDtypeStruct(w.shape, BF16) for w in casts],
        scratch_shapes=[pltpu.VMEM((BLOCK, SWA_NQ), F32),
                        pltpu.VMEM((1, SWA_NQ), F32),
                        pltpu.VMEM((ROW_TILE + BLOCK, kv_w), BF16),
                        pltpu.VMEM((ROW_TILE + BLOCK, kv_w), BF16),
                        pltpu.VMEM((ROW_TILE, D_MODEL), BF16),
                        pltpu.VMEM((ROW_TILE, SWA_PW), BF16),
                        pltpu.VMEM((BLOCK, kv_w), BF16),
                        pltpu.VMEM((BLOCK, kv_w), BF16)],
        compiler_params=pltpu.CompilerParams(dimension_semantics=("arbitrary",), vmem_limit_bytes=VMEM_LIMIT),
        name="swa_mixer",
    )(table, sinks, jnp.asarray(_folded_bucket_map(1)), g_mix.reshape(1, D_MODEL), w_in, kvm, wout, h,
      g_ffn.reshape(1, D_MODEL), *casts)


DIL_SLAB = 3 * DIL_W
TOK_W = DIL_SLAB + MEM_W
SLABS_PER_GROUP = DIL_SLAB // LANES


def _dilproj_kernel(h_ref, g_ref, w_ref, tok_ref, g1_ref, g2_ref, slab_sc):
    hn = _rms(h_ref[...], g_ref[...]).astype(BF16)
    res = jnp.dot(hn, w_ref[...], preferred_element_type=F32)
    tok_ref[...] = res[:, :TOK_W].astype(BF16)
    for s in range(2 * SLABS_PER_GROUP):
        slab_sc[s] = res[:, TOK_W + s * LANES:TOK_W + (s + 1) * LANES]
    for gi, out_ref in ((1, g1_ref), (2, g2_ref)):
        d = DIL_GROUPS[gi][1]
        for s in range(SLABS_PER_GROUP):
            for r in range(d):
                rows = slab_sc[(gi - 1) * SLABS_PER_GROUP + s, pl.ds(r, ROW_TILE // d, stride=d), :]
                out_ref[r, :, s * LANES:(s + 1) * LANES] = rows.astype(BF16)


def _dilproj(h, g, w):
    d1, d2 = DIL_GROUPS[1][1], DIL_GROUPS[2][1]
    return pl.pallas_call(
        _dilproj_kernel,
        grid=(SEQ // ROW_TILE,),
        in_specs=[pl.BlockSpec((ROW_TILE, D_MODEL), lambda i: (i, 0)),
                  pl.BlockSpec((1, D_MODEL), lambda i: (0, 0)),
                  pl.BlockSpec((D_MODEL, DIL_IN), lambda i: (0, 0))],
        out_specs=[pl.BlockSpec((ROW_TILE, TOK_W), lambda i: (i, 0)),
                   pl.BlockSpec((d1, ROW_TILE // d1, DIL_SLAB), lambda i: (0, i, 0)),
                   pl.BlockSpec((d2, ROW_TILE // d2, DIL_SLAB), lambda i: (0, i, 0))],
        out_shape=[jax.ShapeDtypeStruct((SEQ, TOK_W), BF16),
                   jax.ShapeDtypeStruct((d1, SEQ // d1, DIL_SLAB), BF16),
                   jax.ShapeDtypeStruct((d2, SEQ // d2, DIL_SLAB), BF16)],
        scratch_shapes=[pltpu.VMEM((2 * SLABS_PER_GROUP, ROW_TILE, LANES), F32)],
        compiler_params=pltpu.CompilerParams(dimension_semantics=("arbitrary",), vmem_limit_bytes=VMEM_LIMIT),
        name="dil_proj",
    )(h, g.reshape(1, D_MODEL), w)


def _dil_kernel(tab_ref, bucket_ref, q_ref, k_ref, v_ref, kp_ref, vp_ref, o_ref, l_ref,
                bias_sc, diag_sc, k_sc, v_sc, *, head0, diag_bucket):
    r = pl.program_id(0)
    n = pl.program_id(1)
    heads = [head0 + h for h in range(DIL_HEADS)]
    nq = DIL_HEADS * BLOCK

    @pl.when((r == 0) & (n == 0))
    def _():
        _fill_bias(bias_sc, tab_ref, bucket_ref[...], heads)
        _fill_row(diag_sc, [tab_ref[diag_bucket, h] for h in heads])

    k_sc[0:BLOCK, :] = kp_ref[...]
    k_sc[BLOCK:, :] = k_ref[...]
    v_sc[0:BLOCK, :] = vp_ref[...]
    v_sc[BLOCK:, :] = v_ref[...]
    own, eye = _fold_masks(nq)
    upper_rows = lax.broadcasted_iota(jnp.int32, (BLOCK, BLOCK), 0) < HEAD_DIM

    def block_body(b, carry):
        r0 = b * BLOCK
        has_prev = (n > 0) if b == 0 else None
        qb = q_ref[pl.ds(r0, BLOCK), :]
        kb = k_sc[pl.ds(r0, 2 * BLOCK), :]
        vb = v_sc[pl.ds(r0, 2 * BLOCK), :]
        slab = lambda x, s: x[:, s * LANES:(s + 1) * LANES]
        scores = []
        for s in range(DIL_W // LANES):
            scores += _pair_scores(slab(kb, s), slab(qb, s))
        pt, lse = _band_softmax(jnp.concatenate(scores, axis=1), bias_sc[...], own, eye, has_prev, None,
                                diag_sc[...])
        outs, lses = [], []
        for s in range(DIL_W // LANES):
            ca, cb = 2 * s * BLOCK, (2 * s + 1) * BLOCK
            outs.append(jnp.transpose(_pair_values(slab(vb, s), pt[:, ca:ca + BLOCK], pt[:, cb:cb + BLOCK])))
            lse_t = jnp.where(upper_rows, jnp.broadcast_to(lse[:, ca:ca + BLOCK], (BLOCK, BLOCK)),
                              jnp.broadcast_to(lse[:, cb:cb + BLOCK], (BLOCK, BLOCK)))
            lses.append(jnp.transpose(lse_t))
        o_ref[pl.ds(r0, BLOCK), :] = jnp.concatenate(outs, axis=-1)
        l_ref[pl.ds(r0, BLOCK), :] = jnp.concatenate(lses, axis=-1)
        return carry

    for b in range(DIL_TILE // BLOCK):
        block_body(b, None)


DIL_TILE = 1024


def _dil_group(qkv, table, gi):
    window, d = DIL_GROUPS[gi]
    rows = SEQ // d
    assert qkv.shape[:2] == (d, rows) and window // d == BLOCK
    prev = lambda n: jnp.maximum(n * (DIL_TILE // BLOCK) - 1, 0)
    tile = lambda c: pl.BlockSpec((None, DIL_TILE, DIL_W), lambda r, n: (r, n, c))
    prev_block = lambda c: pl.BlockSpec((None, BLOCK, DIL_W), lambda r, n: (r, prev(n), c))
    return pl.pallas_call(
        functools.partial(_dil_kernel, head0=gi * DIL_HEADS, diag_bucket=_diag_bucket(d)),
        grid=(d, rows // DIL_TILE),
        in_specs=[
            pl.BlockSpec(memory_space=pltpu.SMEM),
            pl.BlockSpec((BLOCK, BLOCK), lambda r, n: (0, 0)),
            tile(0), tile(1), tile(2), prev_block(1), prev_block(2),
        ],
        out_specs=[tile(0), tile(0)],
        out_shape=[jax.ShapeDtypeStruct((d, rows, DIL_W), F32), jax.ShapeDtypeStruct((d, rows, DIL_W), F32)],
        scratch_shapes=[pltpu.VMEM((BLOCK, DIL_HEADS * BLOCK), F32),
                        pltpu.VMEM((1, DIL_HEADS * BLOCK), F32),
                        pltpu.VMEM((DIL_TILE + BLOCK, DIL_W), BF16),
                        pltpu.VMEM((DIL_TILE + BLOCK, DIL_W), BF16)],
        compiler_params=pltpu.CompilerParams(dimension_semantics=("arbitrary", "arbitrary"),
                                             vmem_limit_bytes=VMEM_LIMIT),
        name=f"dil_attn_{gi}",
    )(table, jnp.asarray(_folded_bucket_map(d)), qkv, qkv, qkv, qkv, qkv)


def _dilout_kernel(o0_ref, o1_ref, o2_ref, l0_ref, l1_ref, l2_ref, qm_ref, kvm_ref, wout_ref, h_ref, g_ref,
                   router_ref, h2_ref, hn2_ref, meta_ref, cnt_ref, metat_ref, carry_sc, tok_sc):
    @pl.when(pl.program_id(0) == 0)
    def _():
        carry_sc[...] = jnp.zeros_like(carry_sc)

    for k, src_ref in enumerate((o1_ref, l1_ref, o2_ref, l2_ref)):
        d = src_ref.shape[0]
        for s in range(DIL_W // LANES):
            for r in range(d):
                tok_sc[k, s, pl.ds(r, ROW_TILE // d, stride=d), :] = src_ref[r, :, s * LANES:(s + 1) * LANES]

    mixed = []
    for s in range(DIL_W // LANES):
        cols = slice(s * LANES, (s + 1) * LANES)
        o0, o1, o2 = o0_ref[:, cols], tok_sc[0, s], tok_sc[2, s]
        l0, l1, l2 = l0_ref[:, cols], tok_sc[1, s], tok_sc[3, s]
        mx = jnp.maximum(jnp.maximum(l0, l1), l2)
        e0, e1, e2 = jnp.exp(l0 - mx), jnp.exp(l1 - mx), jnp.exp(l2 - mx)
        inv = 1.0 / (e0 + e1 + e2)
        mixed.append(((e0 * inv) * o0 + (e1 * inv) * o1 + (e2 * inv) * o2).astype(BF16))
    cat = jnp.concatenate(mixed + [_to_rows(_mem_attention(qm_ref[...], kvm_ref[...]))], axis=-1)
    out = h_ref[...] + jnp.dot(cat, wout_ref[...], preferred_element_type=F32)
    h2_ref[...] = out
    hn = _rms(out, g_ref[...]).astype(BF16)
    hn2_ref[...] = hn

    logits = _dot_nt(router_ref[...], hn)
    row = lax.broadcasted_iota(jnp.int32, logits.shape, 0)
    masked = jnp.where(row < N_EXPERTS, logits, -jnp.inf)
    v0 = jnp.max(masked, axis=0, keepdims=True)
    i0 = jnp.min(jnp.where(masked == v0, row, ROUTER_ROWS), axis=0, keepdims=True)
    rest = jnp.where(row == i0, -jnp.inf, masked)
    v1 = jnp.max(rest, axis=0, keepdims=True)
    i1 = jnp.min(jnp.where(rest == v1, row, ROUTER_ROWS), axis=0, keepdims=True)
    ex = jnp.exp(v1 - v0)
    inv2 = 1.0 / (1.0 + ex)

    oh0, oh1 = row == i0, row == i1
    sel = (oh0 | oh1).astype(F32)
    tok_r = lax.broadcasted_iota(jnp.int32, (ROW_TILE, ROW_TILE), 0)
    tok_c = lax.broadcasted_iota(jnp.int32, (ROW_TILE, ROW_TILE), 1)
    earlier = (tok_r < tok_c).astype(F32).astype(BF16)
    carry = carry_sc[:, 0:1]
    before = jnp.dot(sel.astype(BF16), earlier, preferred_element_type=F32) + carry
    rank0 = jnp.sum(jnp.where(oh0, before, 0.0), axis=0, keepdims=True)
    rank1 = jnp.sum(jnp.where(oh1, before, 0.0), axis=0, keepdims=True)
    count = jnp.broadcast_to(carry + jnp.sum(sel, axis=1, keepdims=True), carry_sc.shape)
    carry_sc[...] = count
    cnt_ref[...] = count[:SUBLANES]

    fields = (i0.astype(F32), i1.astype(F32), inv2, ex * inv2, rank0, rank1)
    meta_t = jnp.zeros((LANES, ROW_TILE), F32)
    frow = lax.broadcasted_iota(jnp.int32, meta_t.shape, 0)
    for k, f in enumerate(fields):
        meta_t = jnp.where(frow == k, f, meta_t)
    metat_ref[...] = meta_t[:SUBLANES]
    meta_ref[...] = jnp.transpose(meta_t)


META_E0, META_E1, META_G0, META_G1, META_RANK0, META_RANK1 = range(6)
ROUTER_ROWS = 16


def _dilout(os, ls, tok, kvm, wout, h, g_ffn, router):
    row = lambda i: (i, 0)
    const = lambda i: (0, 0)

    def group_spec(a):
        d = a.shape[0]
        if d == 1:
            return pl.BlockSpec((None, ROW_TILE, DIL_W), lambda i: (0, i, 0))
        return pl.BlockSpec((d, ROW_TILE // d, DIL_W), lambda i: (0, i, 0))

    return pl.pallas_call(
        _dilout_kernel,
        grid=(SEQ // ROW_TILE,),
        in_specs=[group_spec(a) for a in (*os, *ls)] + [
                  pl.BlockSpec((ROW_TILE, MEM_W), lambda i: (i, DIL_SLAB // MEM_W)),
                  pl.BlockSpec((N_MEM, 2 * MEM_W), lambda i: (0, 1)),
                  pl.BlockSpec((DIL_W + MEM_W, D_MODEL), const),
                  pl.BlockSpec((ROW_TILE, D_MODEL), row),
                  pl.BlockSpec((1, D_MODEL), const),
                  pl.BlockSpec((ROUTER_ROWS, D_MODEL), const)],
        out_specs=[pl.BlockSpec((ROW_TILE, D_MODEL), row),
                   pl.BlockSpec((ROW_TILE, D_MODEL), row),
                   pl.BlockSpec((ROW_TILE, LANES), row),
                   pl.BlockSpec((SUBLANES, LANES), const),
                   pl.BlockSpec((SUBLANES, ROW_TILE), lambda i: (0, i))],
        out_shape=[jax.ShapeDtypeStruct((SEQ, D_MODEL), F32), jax.ShapeDtypeStruct((SEQ, D_MODEL), BF16),
                   jax.ShapeDtypeStruct((SEQ, LANES), F32), jax.ShapeDtypeStruct((SUBLANES, LANES), F32),
                   jax.ShapeDtypeStruct((SUBLANES, SEQ), F32)],
        scratch_shapes=[pltpu.VMEM((ROUTER_ROWS, LANES), F32),
                        pltpu.VMEM((4, DIL_W // LANES, ROW_TILE, LANES), F32)],
        compiler_params=pltpu.CompilerParams(dimension_semantics=("arbitrary",), vmem_limit_bytes=VMEM_LIMIT),
        name="dil_out",
    )(*os, *ls, tok, kvm, wout, h, g_ffn.reshape(1, D_MODEL), router)


def _swiglu_partial(x, wg, wu, wd):
    a = jnp.dot(x, wg, preferred_element_type=F32)
    b = jnp.dot(x, wu, preferred_element_type=F32)
    hm = (a * jax.nn.sigmoid(a) * b).astype(BF16)
    return jnp.dot(hm, wd, preferred_element_type=F32)


def _ffn_kernel(x_ref, wg_ref, wu_ref, wd_ref, h_ref, wa_ref, o_ref, wa_out, zero_out):
    wa_out[...] = wa_ref[...].astype(BF16)
    zero_out[...] = jnp.zeros(zero_out.shape, zero_out.dtype)
    x = x_ref[...]
    acc = h_ref[...]
    for c in range(D_FF // FFN_COL_TILE):
        cols = slice(c * FFN_COL_TILE, (c + 1) * FFN_COL_TILE)
        acc = acc + _swiglu_partial(x, wg_ref[:, cols], wu_ref[:, cols], wd_ref[cols, :])
    o_ref[...] = acc


def _ffn(x, h, wg, wu, wd, wa):
    tm = ROW_TILE
    steps = SEQ // tm
    resident = lambda shape: pl.BlockSpec(shape, lambda i: (0, 0), pipeline_mode=pl.Buffered(1))
    return pl.pallas_call(
        _ffn_kernel,
        grid=(steps,),
        in_specs=[pl.BlockSpec((tm, D_MODEL), lambda i: (i, 0)),
                  resident((D_MODEL, D_FF)), resident((D_MODEL, D_FF)), resident((D_FF, D_MODEL)),
                  pl.BlockSpec((tm, D_MODEL), lambda i: (i, 0)),
                  _cast_slice_spec(wa, steps)],
        out_specs=[pl.BlockSpec((tm, D_MODEL), lambda i: (i, 0)), _cast_slice_spec(wa, steps),
                   pl.BlockSpec((MOE_SLOTS // steps, PACK_W), lambda i: (i, 0))],
        out_shape=[jax.ShapeDtypeStruct((SEQ, D_MODEL), F32), jax.ShapeDtypeStruct(wa.shape, BF16),
                   jax.ShapeDtypeStruct((MOE_SLOTS, PACK_W), jnp.uint32)],
        compiler_params=pltpu.CompilerParams(dimension_semantics=("arbitrary",), vmem_limit_bytes=VMEM_LIMIT),
        name="ffn",
    )(x, wg, wu, wd, h, wa)


MOE_SUB = 512
MOE_ROWS = 2048
MOE_COL_TILE = D_FF // 2
MOE_SUBS = MOE_ROWS // MOE_SUB
MOE_SLOTS = 2 * SEQ + N_EXPERTS * MOE_SUB
MOE_TILES = MOE_SLOTS // MOE_ROWS
MOE_ITEMS = MOE_TILES + N_EXPERTS - 1
PACK_W = D_MODEL // 2
ITEM_TILE, ITEM_EXPERT, ITEM_LO, ITEM_HI, ITEM_FIRST = range(5)


def _pack_bf16_pairs(x):
    lo = lax.bitcast_convert_type(x[:, :PACK_W].astype(F32), jnp.uint32) >> 16
    hi = lax.bitcast_convert_type(x[:, PACK_W:].astype(F32), jnp.uint32) & jnp.uint32(0xFFFF0000)
    return hi | lo


def _unpack_bf16_pairs(pk):
    lo = lax.bitcast_convert_type(pk << 16, F32).astype(BF16)
    hi = lax.bitcast_convert_type(pk & jnp.uint32(0xFFFF0000), F32).astype(BF16)
    return lo, hi


def _vmem_row(ref, group, sub):
    return ref.at[group, pl.ds(sub, 1), :]


def _hbm_row(ref, row):
    return ref.at[pl.ds(row, 1), :]


def _scatter_kernel(pos0_ref, pos1_ref, x_ref, xs_in_ref, xs_ref, pk_sc, sem):
    del xs_in_ref
    pk_sc[...] = _pack_bf16_pairs(x_ref[...]).reshape(pk_sc.shape)

    def issue(g, c):
        for u in range(SUBLANES):
            t = g * SUBLANES + u
            pltpu.make_async_copy(_vmem_row(pk_sc, g, u), _hbm_row(xs_ref, pos0_ref[t]), sem).start(priority=0)
            pltpu.make_async_copy(_vmem_row(pk_sc, g, u), _hbm_row(xs_ref, pos1_ref[t]), sem).start(priority=1)
        return c

    def drain(g, c):
        for _ in range(2 * SUBLANES):
            pltpu.make_async_copy(_vmem_row(pk_sc, 0, 0), _hbm_row(xs_ref, 0), sem).wait()
        return c

    lax.fori_loop(0, ROW_TILE // SUBLANES, issue, 0)
    lax.fori_loop(0, ROW_TILE // SUBLANES, drain, 0)


def _moe_scatter(x, pos0, pos1, zero_slots):
    smem_tile = pl.BlockSpec((ROW_TILE,), lambda i: (i,), memory_space=pltpu.SMEM)
    return pl.pallas_call(
        _scatter_kernel,
        grid=(SEQ // ROW_TILE,),
        in_specs=[smem_tile, smem_tile,
                  pl.BlockSpec((ROW_TILE, D_MODEL), lambda i: (i, 0)),
                  pl.BlockSpec(memory_space=pl.ANY)],
        out_specs=pl.BlockSpec(memory_space=pl.ANY),
        out_shape=jax.ShapeDtypeStruct((MOE_SLOTS, PACK_W), jnp.uint32),
        scratch_shapes=[pltpu.VMEM((ROW_TILE // SUBLANES, SUBLANES, PACK_W), jnp.uint32),
                        pltpu.SemaphoreType.DMA],
        input_output_aliases={3: 0},
        compiler_params=pltpu.CompilerParams(dimension_semantics=("arbitrary",), vmem_limit_bytes=VMEM_LIMIT),
        name="moe_scatter",
    )(pos0, pos1, x, zero_slots)


def _moe_kernel(items_ref, xs_ref, wg_ref, wu_ref, wd_ref, y_ref, xb_sc):
    w = pl.program_id(0)
    j = pl.program_id(1)
    lo = items_ref[ITEM_LO, w]
    hi = items_ref[ITEM_HI, w]

    @pl.when((j == 0) & (hi > lo))
    def _():
        xlo, xhi = _unpack_bf16_pairs(xs_ref[...])
        xb_sc[:, :PACK_W] = xlo
        xb_sc[:, PACK_W:] = xhi

    @pl.when((j == 0) & (items_ref[ITEM_FIRST, w] == 1))
    def _():
        y_ref[...] = jnp.zeros_like(y_ref)

    for sb in range(MOE_SUBS):
        @pl.when((lo <= sb) & (sb < hi))
        def _(sb=sb):
            rows = slice(sb * MOE_SUB, (sb + 1) * MOE_SUB)
            x = xb_sc[rows, :]
            acc = y_ref[rows, :]
            for c0 in range(0, MOE_COL_TILE, FFN_COL_TILE):
                cols = slice(c0, min(c0 + FFN_COL_TILE, MOE_COL_TILE))
                acc = acc + _swiglu_partial(x, wg_ref[0, :, cols], wu_ref[0, :, cols], wd_ref[0, cols, :])
            y_ref[rows, :] = acc


def _moe_grouped(items, xs, wg, wu, wd):
    tf = MOE_COL_TILE
    nf = D_FF // tf
    jf = lambda w, j, it: jnp.where(it[ITEM_HI, w] > it[ITEM_LO, w], j, nf - 1)
    grid_spec = pltpu.PrefetchScalarGridSpec(
        num_scalar_prefetch=1,
        grid=(MOE_ITEMS, nf),
        in_specs=[pl.BlockSpec((MOE_ROWS, PACK_W), lambda w, j, it: (it[ITEM_TILE, w], 0)),
                  pl.BlockSpec((1, D_MODEL, tf), lambda w, j, it: (it[ITEM_EXPERT, w], 0, jf(w, j, it))),
                  pl.BlockSpec((1, D_MODEL, tf), lambda w, j, it: (it[ITEM_EXPERT, w], 0, jf(w, j, it))),
                  pl.BlockSpec((1, tf, D_MODEL), lambda w, j, it: (it[ITEM_EXPERT, w], jf(w, j, it), 0))],
        out_specs=pl.BlockSpec((MOE_ROWS, D_MODEL), lambda w, j, it: (it[ITEM_TILE, w], 0)),
        scratch_shapes=[pltpu.VMEM((MOE_ROWS, D_MODEL), BF16)],
    )
    return pl.pallas_call(
        _moe_kernel,
        grid_spec=grid_spec,
        out_shape=jax.ShapeDtypeStruct((MOE_SLOTS, D_MODEL), F32),
        compiler_params=pltpu.CompilerParams(dimension_semantics=("arbitrary", "arbitrary"),
                                             vmem_limit_bytes=VMEM_LIMIT),
        name="moe_grouped",
    )(items, xs, wg, wu, wd)


def _combine_kernel(pos0_ref, pos1_ref, meta_ref, h_ref, gf_ref, y_ref, o_ref, y0_sc, y1_sc, sem):
    def issue(g, c):
        for u in range(SUBLANES):
            t = g * SUBLANES + u
            pltpu.make_async_copy(_hbm_row(y_ref, pos0_ref[t]), _vmem_row(y0_sc, g, u), sem).start(priority=0)
            pltpu.make_async_copy(_hbm_row(y_ref, pos1_ref[t]), _vmem_row(y1_sc, g, u), sem).start(priority=1)
        return c

    def drain(g, c):
        for _ in range(SUBLANES):
            pltpu.make_async_copy(_hbm_row(y_ref, 0), _vmem_row(y0_sc, 0, 0), sem).wait()
            pltpu.make_async_copy(_hbm_row(y_ref, 0), _vmem_row(y1_sc, 0, 0), sem).wait()
        return c

    lax.fori_loop(0, ROW_TILE // SUBLANES, issue, 0)
    lax.fori_loop(0, ROW_TILE // SUBLANES, drain, 0)
    meta = meta_ref[...]
    g0 = meta[:, META_G0:META_G0 + 1]
    g1 = meta[:, META_G1:META_G1 + 1]
    y0 = y0_sc[...].reshape(ROW_TILE, D_MODEL)
    y1 = y1_sc[...].reshape(ROW_TILE, D_MODEL)
    o_ref[...] = _rms(h_ref[...] + (g0 * y0 + g1 * y1), gf_ref[...])


def _moe_combine(pos0, pos1, meta, h, g_final, y):
    smem_tile = pl.BlockSpec((ROW_TILE,), lambda i: (i,), memory_space=pltpu.SMEM)
    tiles = (ROW_TILE // SUBLANES, SUBLANES, D_MODEL)
    return pl.pallas_call(
        _combine_kernel,
        grid=(SEQ // ROW_TILE,),
        in_specs=[smem_tile, smem_tile,
                  pl.BlockSpec((ROW_TILE, LANES), lambda i: (i, 0)),
                  pl.BlockSpec((ROW_TILE, D_MODEL), lambda i: (i, 0)),
                  pl.BlockSpec((1, D_MODEL), lambda i: (0, 0)),
                  pl.BlockSpec(memory_space=pl.ANY)],
        out_specs=pl.BlockSpec((ROW_TILE, D_MODEL), lambda i: (i, 0)),
        out_shape=jax.ShapeDtypeStruct((SEQ, D_MODEL), F32),
        scratch_shapes=[pltpu.VMEM(tiles, F32), pltpu.VMEM(tiles, F32), pltpu.SemaphoreType.DMA],
        compiler_params=pltpu.CompilerParams(dimension_semantics=("arbitrary",), vmem_limit_bytes=VMEM_LIMIT),
        name="moe_combine",
    )(pos0, pos1, meta, h, g_final.reshape(1, D_MODEL), y)


def _moe_plan(meta_t, counts):
    i32 = jnp.int32
    cnt = counts[:N_EXPERTS, 0].astype(i32)
    padded = (cnt + MOE_SUB - 1) // MOE_SUB * MOE_SUB
    start = jnp.cumsum(padded) - padded

    def slot(e_row, rank_row):
        e = meta_t[e_row].astype(i32)
        base = sum(jnp.where(e == k, start[k], 0) for k in range(N_EXPERTS))
        return base + meta_t[rank_row].astype(i32)

    pos0, pos1 = slot(META_E0, META_RANK0), slot(META_E1, META_RANK1)

    first_sb, end_sb = start // MOE_SUB, (start + padded) // MOE_SUB
    t0 = jnp.arange(MOE_TILES, dtype=i32)[:, None] * MOE_SUBS
    lo = jnp.clip(first_sb[None, :] - t0, 0, MOE_SUBS)
    hi = jnp.clip(end_sb[None, :] - t0, 0, MOE_SUBS)
    active = hi > lo
    none = ~jnp.any(active, axis=1, keepdims=True)
    active = jnp.concatenate([active, none], axis=1)
    lo = jnp.concatenate([lo, jnp.zeros_like(t0)], axis=1)
    hi = jnp.concatenate([hi, jnp.zeros_like(t0)], axis=1)
    first = active & (jnp.cumsum(active.astype(i32), axis=1) == 1)
    ncol = N_EXPERTS + 1
    flat = jnp.arange(MOE_TILES * ncol, dtype=i32)
    key = jnp.where(active.reshape(-1), flat, MOE_TILES * ncol)
    order = jnp.argsort(key)[:MOE_ITEMS]
    live = key[order] < MOE_TILES * ncol
    tile = jnp.where(live, order // ncol, MOE_TILES - 1)
    lo_i = jnp.where(live, lo.reshape(-1)[order], 0)
    hi_i = jnp.where(live, hi.reshape(-1)[order], 0)
    first_i = jnp.where(live, first.reshape(-1)[order], False).astype(i32)
    idx = jnp.arange(MOE_ITEMS, dtype=i32)
    src = jnp.maximum(lax.cummax(jnp.where(hi_i > lo_i, idx, -1), axis=0), 0)
    expert = jnp.minimum(order % ncol, N_EXPERTS - 1)[src]
    items = jnp.stack([tile, expert, lo_i, hi_i, first_i]).astype(i32)
    return pos0, pos1, items


def _take_runs(w, idx, axis):
    idx = list(idx)
    runs, start = [], 0
    for i in range(1, len(idx) + 1):
        if i == len(idx) or idx[i] != idx[i - 1] + 1:
            runs.append(lax.slice_in_dim(w, idx[start], idx[i - 1] + 1, axis=axis))
            start = i
    return jnp.concatenate(runs, axis=axis)


def _swa_weight_layout():
    head = lambda base, j: list(range(base + j * HEAD_DIM, base + (j + 1) * HEAD_DIM))
    k0, v0, m0 = SWA_Q, SWA_Q + SWA_KV, SWA_Q + 2 * SWA_KV
    cols, rows = [], []
    for a, b in SWA_PAIR_SLABS:
        cols += head(0, a) + head(0, b)
        rows += head(0, a) + head(0, b)
    for base in (k0, v0):
        cols += head(base, 0) + head(base, 1) + head(base, 2) + head(base, 2)
    cols += list(range(m0, m0 + MEM_W))
    rows += list(range(SWA_Q, SWA_Q + MEM_W))
    scale = np.ones((len(cols),), np.float32)
    scale[:SWA_Q] = Q_SCALE
    scale[SWA_M0:] = Q_SCALE
    assert len(cols) == SWA_PW and len(rows) == D_MODEL
    return np.asarray(cols), scale, np.asarray(rows)


def _dil_weight_layout():
    n_grp = len(DIL_GROUPS) * DIL_SLAB
    cols = list(range(DIL_SLAB)) + list(range(n_grp, n_grp + MEM_W)) + list(range(DIL_SLAB, n_grp))
    scale = np.ones((len(cols),), np.float32)
    for c0 in (0, DIL_SLAB, TOK_W, TOK_W + DIL_SLAB):
        scale[c0:c0 + DIL_W] = Q_SCALE
    return np.asarray(cols), scale


def kernel(x, mem, rel_bias_table, mem_norm, norm_mix, norm_ffn, final_norm, swa_w_in, swa_sinks, swa_w_mem_kv,
           swa_w_out, dil_w_in, dil_w_mem_kv, dil_w_out, ffn_gate, ffn_up, ffn_down, router, moe_gate, moe_up,
           moe_down):
    assert x.shape == (1, SEQ, D_MODEL) and mem.shape == (1, N_MEM, D_MODEL)
    assert norm_mix.shape == (2, D_MODEL) and swa_w_in.shape == (1, D_MODEL, SWA_IN)
    assert dil_w_in.shape == (1, D_MODEL, DIL_IN) and moe_gate.shape == (1, N_EXPERTS, D_MODEL, D_FF)
    bf = lambda a: a.astype(BF16)
    h0 = x.reshape(SEQ, D_MODEL)

    kvm = _memkv(mem[0], mem_norm, bf(jnp.concatenate([swa_w_mem_kv[0], dil_w_mem_kv[0]], axis=1)))

    cols0, scale0, rows0 = _swa_weight_layout()
    h1, hn1, moe_up_b, moe_down_b, ffn_gate_b, ffn_up_b, ffn_down_b = _swa_layer(
        bf(_take_runs(swa_w_in[0], cols0, 1) * scale0), norm_mix[0], kvm, rel_bias_table, swa_sinks[0],
        bf(_take_runs(swa_w_out[0], rows0, 0)), h0, norm_ffn[0],
        [moe_up[0], moe_down[0], ffn_gate, ffn_up, ffn_down])
    h2, moe_gate_b, xs_zero = _ffn(hn1, h1, ffn_gate_b[0], ffn_up_b[0], ffn_down_b[0], moe_gate[0])

    cols1, scale1 = _dil_weight_layout()
    tok, g1, g2 = _dilproj(h2, norm_mix[1], bf(_take_runs(dil_w_in[0], cols1, 1) * scale1))
    groups = [_dil_group(qkv, rel_bias_table, gi) for gi, qkv in enumerate((tok[None], g1, g2))]
    router_p = jnp.pad(bf(router[0].T), ((0, ROUTER_ROWS - N_EXPERTS), (0, 0)))
    h3, hn3, meta, counts, meta_t = _dilout([g[0] for g in groups], [g[1] for g in groups], tok, kvm,
                                            bf(dil_w_out[0]), h2, norm_ffn[1], router_p)
    pos0, pos1, items = _moe_plan(meta_t, counts)
    xs = _moe_scatter(hn3, pos0, pos1, xs_zero)
    y = _moe_grouped(items, xs, moe_gate_b, moe_up_b, moe_down_b)
    out = _moe_combine(pos0, pos1, meta, h3, final_norm, y)
    return out.reshape(1, SEQ, D_MODEL)
```

```python
import functools
import math

import jax
import jax.numpy as jnp
import numpy as np
from jax import lax
from jax.experimental import pallas as pl
from jax.experimental.pallas import tpu as pltpu

F32 = jnp.float32
BF16 = jnp.bfloat16

D_MODEL = 1024
SEQ = 16384
HEAD_DIM = 64
N_MIX_HEADS = 12
SWA_KV_HEADS = 3
SWA_GROUP = N_MIX_HEADS // SWA_KV_HEADS
SWA_WINDOW = 128
DIL_GROUPS = ((128, 1), (512, 4), (2048, 16))
DIL_HEADS = 4
N_MEM = 256
MEM_HEADS = 4
BLOCK = 128
N_BUCKETS = 32
MAX_DISTANCE = 2048
D_FF = 3584
N_EXPERTS = 8
EPS = 1e-5
NEG = -1e30
Q_SCALE = HEAD_DIM ** -0.5

SWA_Q = N_MIX_HEADS * HEAD_DIM
SWA_KV = SWA_KV_HEADS * HEAD_DIM
SWA_IN = SWA_Q + 2 * SWA_KV + MEM_HEADS * HEAD_DIM
DIL_W = DIL_HEADS * HEAD_DIM
DIL_IN = len(DIL_GROUPS) * 3 * DIL_W + MEM_HEADS * HEAD_DIM
MEM_W = MEM_HEADS * HEAD_DIM

LANES = 128
SUBLANES = 8
VMEM_LIMIT = 56 * 1024 * 1024

ROW_TILE = 512
BLOCKS_PER_TILE = ROW_TILE // BLOCK
FFN_COL_TILE = 256


def _bucket_map(dil):
    qi = np.arange(BLOCK)[:, None]
    kj = np.arange(2 * BLOCK)[None, :]
    d = np.maximum((qi + BLOCK - kj) * dil, 0)
    max_exact = N_BUCKETS // 2
    ratio = np.maximum(d, 1).astype(np.float32) / np.float32(max_exact)
    large = max_exact + (np.log(ratio) / np.float32(math.log(MAX_DISTANCE / max_exact))
                         * np.float32(N_BUCKETS - max_exact)).astype(np.int32)
    return np.where(d < max_exact, d, np.minimum(large, N_BUCKETS - 1)).astype(np.int32)


def _rms(x, g):
    ms = jnp.mean(x * x, axis=-1, keepdims=True)
    return x * lax.rsqrt(ms + EPS) * g


def _dot_nt(a, b):
    return lax.dot_general(a, b, (((1,), (1,)), ((), ())), preferred_element_type=F32)


def _dot_tn(a, b):
    return lax.dot_general(a, b, (((0,), (0,)), ((), ())), preferred_element_type=F32)


def _fill_bias(bias_sc, tab_ref, bucket, heads):
    for j, h in enumerate(heads):
        def body(k, b, h=h):
            return jnp.where(bucket == k, tab_ref[k, h], b)
        bias_sc[:, j * BLOCK:(j + 1) * BLOCK] = lax.fori_loop(0, N_BUCKETS, body, jnp.zeros(bucket.shape, F32))


def _fill_row(row_sc, values):
    blk = lax.broadcasted_iota(jnp.int32, row_sc.shape, 1) // BLOCK
    row = jnp.zeros(row_sc.shape, F32)
    for j, v in enumerate(values):
        row = jnp.where(blk == j, v, row)
    row_sc[...] = row


def _folded_bucket_map(dil):
    full = _bucket_map(dil)
    qi = np.arange(BLOCK)[:, None]
    c = np.arange(BLOCK)[None, :]
    return np.ascontiguousarray(np.where(c <= qi, full[:, BLOCK:], full[:, :BLOCK]).T.astype(np.int32))


def _diag_bucket(dil):
    return int(_bucket_map(dil)[0, 0])


def _fold_masks(n):
    c = lax.broadcasted_iota(jnp.int32, (BLOCK, n), 0)
    qi = lax.broadcasted_iota(jnp.int32, (BLOCK, n), 1) % BLOCK
    return c <= qi, c == qi


def _pair_scores(k_slab, q_rows):
    lane = lax.broadcasted_iota(jnp.int32, q_rows.shape, 1)
    zero = jnp.zeros_like(q_rows)
    qa = jnp.where(lane < HEAD_DIM, q_rows, zero)
    qb = jnp.where(lane < HEAD_DIM, zero, q_rows)
    return _dot_nt(k_slab, qa), _dot_nt(k_slab, qb)


def _pair_values(v_slab, p_a, p_b):
    n = p_a.shape[1]
    o = _dot_tn(v_slab, jnp.concatenate([p_a, p_b], axis=1))
    row = lax.broadcasted_iota(jnp.int32, (LANES, n), 0)
    return jnp.where(row < HEAD_DIM, o[:, :n], o[:, n:])


def _band_softmax(st, bias, own, eye, has_prev, sink, diag_bias):
    s_prev, s_own = st[:BLOCK], st[BLOCK:]
    t = jnp.where(own, s_own, s_prev) + bias
    if has_prev is not None:
        t = jnp.where(own | has_prev, t, NEG)
    m = jnp.max(t, axis=0, keepdims=True)
    if diag_bias is not None:
        s_d = jnp.sum(jnp.where(eye, s_prev, 0.0), axis=0, keepdims=True) + diag_bias
        if has_prev is not None:
            s_d = jnp.where(has_prev, s_d, NEG)
        m = jnp.maximum(m, s_d)
    if sink is not None:
        m = jnp.maximum(m, sink)
    e = jnp.exp(t - m)
    den = jnp.sum(e, axis=0, keepdims=True)
    if diag_bias is not None:
        e_d = jnp.exp(s_d - m)
        den = den + e_d
    if sink is not None:
        den = den + jnp.exp(sink - m)
    inv = 1.0 / den
    p = e * inv
    p_own = jnp.where(own, p, 0.0)
    p_prev = jnp.where(own, 0.0, p)
    if diag_bias is not None:
        p_prev = jnp.where(eye, e_d * inv, p_prev)
    return jnp.concatenate([p_prev, p_own], axis=0).astype(BF16), m + jnp.log(den)


def _mem_attention(qm, kvm):
    outs = []
    for s in range(MEM_W // LANES):
        sa, sb = _pair_scores(kvm[:, s * LANES:(s + 1) * LANES], qm[:, s * LANES:(s + 1) * LANES])
        ps = []
        for st in (sa, sb):
            e = jnp.exp(st - jnp.max(st, axis=0, keepdims=True))
            ps.append((e * (1.0 / jnp.sum(e, axis=0, keepdims=True))).astype(BF16))
        outs.append(_pair_values(kvm[:, MEM_W + s * LANES:MEM_W + (s + 1) * LANES], ps[0], ps[1]))
    return outs


def _to_rows(slabs_t):
    return jnp.concatenate([jnp.transpose(x).astype(BF16) for x in slabs_t], axis=-1)


def _memkv_kernel(mem_ref, g_ref, w_ref, o_ref):
    mn = _rms(mem_ref[...], g_ref[...]).astype(BF16)
    o_ref[...] = jnp.dot(mn, w_ref[...], preferred_element_type=F32).astype(BF16)


def _memkv(mem, g, w):
    return pl.pallas_call(
        _memkv_kernel,
        out_shape=jax.ShapeDtypeStruct((N_MEM, w.shape[1]), BF16),
        name="mem_kv",
    )(mem, g.reshape(1, D_MODEL), w)


SWA_PAIR_SLABS = [(g, SWA_GROUP + g) for g in range(SWA_GROUP)] + [(8, 9), (10, 11)]
SWA_K0 = SWA_Q
SWA_V0 = SWA_K0 + 2 * LANES
SWA_M0 = SWA_V0 + 2 * LANES
SWA_PW = SWA_M0 + MEM_W
SWA_SCORE_HEADS = [0, 1, 2, 3, 4, 5, 6, 7, 8, 10, 9, 11]
SWA_NQ = N_MIX_HEADS * BLOCK


def _cast_slice_spec(w, steps):
    e, r, c = w.shape
    per_e = steps // e
    assert steps % e == 0 and r % per_e == 0 and (r // per_e) % 16 == 0
    return pl.BlockSpec((1, r // per_e, c), lambda i: (i // per_e, i % per_e, 0))


def _swa_kernel(*refs, n_cast):
    (tab_ref, sink_ref, bucket_ref, gmix_ref, win_ref, kvm_ref, wout_ref, h_ref, g_ref), refs = refs[:9], refs[9:]
    cast_in, (h1_ref, hn1_ref), refs = refs[:n_cast], refs[n_cast:n_cast + 2], refs[n_cast + 2:]
    cast_out, (bias_sc, sink_sc, k_sc, v_sc, cat_sc, p_ref, kprev_sc, vprev_sc) = refs[:n_cast], refs[n_cast:]
    i = pl.program_id(0)
    for src, dst in zip(cast_in, cast_out):
        dst[...] = src[...].astype(BF16)

    @pl.when(i == 0)
    def _():
        _fill_bias(bias_sc, tab_ref, bucket_ref[...], SWA_SCORE_HEADS)
        _fill_row(sink_sc, [sink_ref[h] for h in SWA_SCORE_HEADS])

        kprev_sc[...] = jnp.zeros(kprev_sc.shape, BF16)
        vprev_sc[...] = jnp.zeros(vprev_sc.shape, BF16)

    k_sc[0:BLOCK, :] = kprev_sc[...]
    v_sc[0:BLOCK, :] = vprev_sc[...]

    hn = _rms(h_ref[...], gmix_ref[...]).astype(BF16)
    p_ref[...] = jnp.dot(hn, win_ref[...], preferred_element_type=F32).astype(BF16)

    k_sc[BLOCK:, :] = p_ref[:, SWA_K0:SWA_V0]
    v_sc[BLOCK:, :] = p_ref[:, SWA_V0:SWA_M0]
    own, eye = _fold_masks(SWA_NQ)

    def block_body(b, carry):
        r0 = b * BLOCK
        has_prev = (i > 0) if b == 0 else None
        qb = p_ref[pl.ds(r0, BLOCK), 0:SWA_Q]
        slab = lambda x, s: x[:, s * LANES:(s + 1) * LANES]
        qa = jnp.concatenate([slab(qb, s) for s in range(4)], axis=0)
        qc = jnp.concatenate([slab(qb, s) for s in (4, 5)], axis=0)
        kb = k_sc[pl.ds(r0, 2 * BLOCK), :]
        vb = v_sc[pl.ds(r0, 2 * BLOCK), :]
        s0, s1 = _pair_scores(slab(kb, 0), qa)
        s2, s3 = _pair_scores(slab(kb, 1), qc)
        st = jnp.concatenate([s0, s1, s2, s3], axis=1)
        pt, _ = _band_softmax(st, bias_sc[...], own, eye, has_prev, sink_sc[...], None)
        na, nc = 4 * BLOCK, 2 * BLOCK
        oa = _pair_values(slab(vb, 0), pt[:, 0:na], pt[:, na:2 * na])
        oc = _pair_values(slab(vb, 1), pt[:, 2 * na:2 * na + nc], pt[:, 2 * na + nc:])
        outs = [slab(oa, s) for s in range(4)] + [slab(oc, s) for s in range(2)]
        outs += _mem_attention(p_ref[pl.ds(r0, BLOCK), SWA_M0:], kvm_ref[...])
        cat_sc[pl.ds(r0, BLOCK), :] = _to_rows(outs)
        return carry

    for b in range(BLOCKS_PER_TILE):
        block_body(b, None)

    kprev_sc[...] = p_ref[ROW_TILE - BLOCK:, SWA_K0:SWA_V0]
    vprev_sc[...] = p_ref[ROW_TILE - BLOCK:, SWA_V0:SWA_M0]

    out = h_ref[...] + jnp.dot(cat_sc[...], wout_ref[...], preferred_element_type=F32)
    h1_ref[...] = out
    hn1_ref[...] = _rms(out, g_ref[...]).astype(BF16)


def _swa_layer(w_in, g_mix, kvm, table, sinks, wout, h, g_ffn, casts):
    assert SWA_WINDOW == BLOCK
    kv_w = 2 * LANES
    steps = SEQ // ROW_TILE
    cast_specs = [_cast_slice_spec(w, steps) for w in casts]
    return pl.pallas_call(
        functools.partial(_swa_kernel, n_cast=len(casts)),
        grid=(SEQ // ROW_TILE,),
        in_specs=[
            pl.BlockSpec(memory_space=pltpu.SMEM),
            pl.BlockSpec(memory_space=pltpu.SMEM),
            pl.BlockSpec((BLOCK, BLOCK), lambda i: (0, 0)),
            pl.BlockSpec((1, D_MODEL), lambda i: (0, 0)),
            pl.BlockSpec((D_MODEL, SWA_PW), lambda i: (0, 0)),
            pl.BlockSpec((N_MEM, 2 * MEM_W), lambda i: (0, 0)),
            pl.BlockSpec((D_MODEL, D_MODEL), lambda i: (0, 0)),
            pl.BlockSpec((ROW_TILE, D_MODEL), lambda i: (i, 0)),
            pl.BlockSpec((1, D_MODEL), lambda i: (0, 0)),
        ] + cast_specs,
        out_specs=[pl.BlockSpec((ROW_TILE, D_MODEL), lambda i: (i, 0)),
                   pl.BlockSpec((ROW_TILE, D_MODEL), lambda i: (i, 0))] + cast_specs,
        out_shape=[jax.ShapeDtypeStruct((SEQ, D_MODEL), F32), jax.ShapeDtypeStruct((SEQ, D_MODEL), BF16)]
                  + [jax.ShapeDtypeStruct(w.shape, BF16) for w in casts],
        scratch_shapes=[pltpu.VMEM((BLOCK, SWA_NQ), F32),
                        pltpu.VMEM((1, SWA_NQ), F32),
                        pltpu.VMEM((ROW_TILE + BLOCK, kv_w), BF16),
                        pltpu.VMEM((ROW_TILE + BLOCK, kv_w), BF16),
                        pltpu.VMEM((ROW_TILE, D_MODEL), BF16),
                        pltpu.VMEM((ROW_TILE, SWA_PW), BF16),
                        pltpu.VMEM((BLOCK, kv_w), BF16),
                        pltpu.VMEM((BLOCK, kv_w), BF16)],
        compiler_params=pltpu.CompilerParams(dimension_semantics=("arbitrary",), vmem_limit_bytes=VMEM_LIMIT),
        name="swa_mixer",
    )(table, sinks, jnp.asarray(_folded_bucket_map(1)), g_mix.reshape(1, D_MODEL), w_in, kvm, wout, h,
      g_ffn.reshape(1, D_MODEL), *casts)


DIL_SLAB = 3 * DIL_W
TOK_W = DIL_SLAB + MEM_W
SLABS_PER_GROUP = DIL_SLAB // LANES


def _dilproj_kernel(h_ref, g_ref, w_ref, tok_ref, g1_ref, g2_ref, slab_sc):
    hn = _rms(h_ref[...], g_ref[...]).astype(BF16)
    res = jnp.dot(hn, w_ref[...], preferred_element_type=F32)
    tok_ref[...] = res[:, :TOK_W].astype(BF16)
    for s in range(2 * SLABS_PER_GROUP):
        slab_sc[s] = res[:, TOK_W + s * LANES:TOK_W + (s + 1) * LANES]
    for gi, out_ref in ((1, g1_ref), (2, g2_ref)):
        d = DIL_GROUPS[gi][1]
        for s in range(SLABS_PER_GROUP):
            for r in range(d):
                rows = slab_sc[(gi - 1) * SLABS_PER_GROUP + s, pl.ds(r, ROW_TILE // d, stride=d), :]
                out_ref[r, :, s * LANES:(s + 1) * LANES] = rows.astype(BF16)


def _dilproj(h, g, w):
    d1, d2 = DIL_GROUPS[1][1], DIL_GROUPS[2][1]
    return pl.pallas_call(
        _dilproj_kernel,
        grid=(SEQ // ROW_TILE,),
        in_specs=[pl.BlockSpec((ROW_TILE, D_MODEL), lambda i: (i, 0)),
                  pl.BlockSpec((1, D_MODEL), lambda i: (0, 0)),
                  pl.BlockSpec((D_MODEL, DIL_IN), lambda i: (0, 0))],
        out_specs=[pl.BlockSpec((ROW_TILE, TOK_W), lambda i: (i, 0)),
                   pl.BlockSpec((d1, ROW_TILE // d1, DIL_SLAB), lambda i: (0, i, 0)),
                   pl.BlockSpec((d2, ROW_TILE // d2, DIL_SLAB), lambda i: (0, i, 0))],
        out_shape=[jax.ShapeDtypeStruct((SEQ, TOK_W), BF16),
                   jax.ShapeDtypeStruct((d1, SEQ // d1, DIL_SLAB), BF16),
                   jax.ShapeDtypeStruct((d2, SEQ // d2, DIL_SLAB), BF16)],
        scratch_shapes=[pltpu.VMEM((2 * SLABS_PER_GROUP, ROW_TILE, LANES), F32)],
        compiler_params=pltpu.CompilerParams(dimension_semantics=("arbitrary",), vmem_limit_bytes=VMEM_LIMIT),
        name="dil_proj",
    )(h, g.reshape(1, D_MODEL), w)


def _dil_kernel(tab_ref, bucket_ref, q_ref, k_ref, v_ref, kp_ref, vp_ref, o_ref, l_ref,
                bias_sc, diag_sc, k_sc, v_sc, *, head0, diag_bucket):
    r = pl.program_id(0)
    n = pl.program_id(1)
    heads = [head0 + h for h in range(DIL_HEADS)]
    nq = DIL_HEADS * BLOCK

    @pl.when((r == 0) & (n == 0))
    def _():
        _fill_bias(bias_sc, tab_ref, bucket_ref[...], heads)
        _fill_row(diag_sc, [tab_ref[diag_bucket, h] for h in heads])

    k_sc[0:BLOCK, :] = kp_ref[...]
    k_sc[BLOCK:, :] = k_ref[...]
    v_sc[0:BLOCK, :] = vp_ref[...]
    v_sc[BLOCK:, :] = v_ref[...]
    own, eye = _fold_masks(nq)
    upper_rows = lax.broadcasted_iota(jnp.int32, (BLOCK, BLOCK), 0) < HEAD_DIM

    def block_body(b, carry):
        r0 = b * BLOCK
        has_prev = (n > 0) if b == 0 else None
        qb = q_ref[pl.ds(r0, BLOCK), :]
        kb = k_sc[pl.ds(r0, 2 * BLOCK), :]
        vb = v_sc[pl.ds(r0, 2 * BLOCK), :]
        slab = lambda x, s: x[:, s * LANES:(s + 1) * LANES]
        scores = []
        for s in range(DIL_W // LANES):
            scores += _pair_scores(slab(kb, s), slab(qb, s))
        pt, lse = _band_softmax(jnp.concatenate(scores, axis=1), bias_sc[...], own, eye, has_prev, None,
                                diag_sc[...])
        outs, lses = [], []
        for s in range(DIL_W // LANES):
            ca, cb = 2 * s * BLOCK, (2 * s + 1) * BLOCK
            outs.append(jnp.transpose(_pair_values(slab(vb, s), pt[:, ca:ca + BLOCK], pt[:, cb:cb + BLOCK])))
            lse_t = jnp.where(upper_rows, jnp.broadcast_to(lse[:, ca:ca + BLOCK], (BLOCK, BLOCK)),
                              jnp.broadcast_to(lse[:, cb:cb + BLOCK], (BLOCK, BLOCK)))
            lses.append(jnp.transpose(lse_t))
        o_ref[pl.ds(r0, BLOCK), :] = jnp.concatenate(outs, axis=-1)
        l_ref[pl.ds(r0, BLOCK), :] = jnp.concatenate(lses, axis=-1)
        return carry

    for b in range(DIL_TILE // BLOCK):
        block_body(b, None)


DIL_TILE = 1024


def _dil_group(qkv, table, gi):
    window, d = DIL_GROUPS[gi]
    rows = SEQ // d
    assert qkv.shape[:2] == (d, rows) and window // d == BLOCK
    prev = lambda n: jnp.maximum(n * (DIL_TILE // BLOCK) - 1, 0)
    tile = lambda c: pl.BlockSpec((None, DIL_TILE, DIL_W), lambda r, n: (r, n, c))
    prev_block = lambda c: pl.BlockSpec((None, BLOCK, DIL_W), lambda r, n: (r, prev(n), c))
    return pl.pallas_call(
        functools.partial(_dil_kernel, head0=gi * DIL_HEADS, diag_bucket=_diag_bucket(d)),
        grid=(d, rows // DIL_TILE),
        in_specs=[
            pl.BlockSpec(memory_space=pltpu.SMEM),
            pl.BlockSpec((BLOCK, BLOCK), lambda r, n: (0, 0)),
            tile(0), tile(1), tile(2), prev_block(1), prev_block(2),
        ],
        out_specs=[tile(0), tile(0)],
        out_shape=[jax.ShapeDtypeStruct((d, rows, DIL_W), F32), jax.ShapeDtypeStruct((d, rows, DIL_W), F32)],
        scratch_shapes=[pltpu.VMEM((BLOCK, DIL_HEADS * BLOCK), F32),
                        pltpu.VMEM((1, DIL_HEADS * BLOCK), F32),
                        pltpu.VMEM((DIL_TILE + BLOCK, DIL_W), BF16),
                        pltpu.VMEM((DIL_TILE + BLOCK, DIL_W), BF16)],
        compiler_params=pltpu.CompilerParams(dimension_semantics=("arbitrary", "arbitrary"),
                                             vmem_limit_bytes=VMEM_LIMIT),
        name=f"dil_attn_{gi}",
    )(table, jnp.asarray(_folded_bucket_map(d)), qkv, qkv, qkv, qkv, qkv)


def _dilout_kernel(o0_ref, o1_ref, o2_ref, l0_ref, l1_ref, l2_ref, qm_ref, kvm_ref, wout_ref, h_ref, g_ref,
                   router_ref, h2_ref, hn2_ref, meta_ref, cnt_ref, metat_ref, carry_sc, tok_sc):
    @pl.when(pl.program_id(0) == 0)
    def _():
        carry_sc[...] = jnp.zeros_like(carry_sc)

    for k, src_ref in enumerate((o1_ref, l1_ref, o2_ref, l2_ref)):
        d = src_ref.shape[0]
        for s in range(DIL_W // LANES):
            for r in range(d):
                tok_sc[k, s, pl.ds(r, ROW_TILE // d, stride=d), :] = src_ref[r, :, s * LANES:(s + 1) * LANES]

    mixed = []
    for s in range(DIL_W // LANES):
        cols = slice(s * LANES, (s + 1) * LANES)
        o0, o1, o2 = o0_ref[:, cols], tok_sc[0, s], tok_sc[2, s]
        l0, l1, l2 = l0_ref[:, cols], tok_sc[1, s], tok_sc[3, s]
        mx = jnp.maximum(jnp.maximum(l0, l1), l2)
        e0, e1, e2 = jnp.exp(l0 - mx), jnp.exp(l1 - mx), jnp.exp(l2 - mx)
        inv = 1.0 / (e0 + e1 + e2)
        mixed.append(((e0 * inv) * o0 + (e1 * inv) * o1 + (e2 * inv) * o2).astype(BF16))
    cat = jnp.concatenate(mixed + [_to_rows(_mem_attention(qm_ref[...], kvm_ref[...]))], axis=-1)
    out = h_ref[...] + jnp.dot(cat, wout_ref[...], preferred_element_type=F32)
    h2_ref[...] = out
    hn = _rms(out, g_ref[...]).astype(BF16)
    hn2_ref[...] = hn

    logits = _dot_nt(router_ref[...], hn)
    row = lax.broadcasted_iota(jnp.int32, logits.shape, 0)
    masked = jnp.where(row < N_EXPERTS, logits, -jnp.inf)
    v0 = jnp.max(masked, axis=0, keepdims=True)
    i0 = jnp.min(jnp.where(masked == v0, row, ROUTER_ROWS), axis=0, keepdims=True)
    rest = jnp.where(row == i0, -jnp.inf, masked)
    v1 = jnp.max(rest, axis=0, keepdims=True)
    i1 = jnp.min(jnp.where(rest == v1, row, ROUTER_ROWS), axis=0, keepdims=True)
    ex = jnp.exp(v1 - v0)
    inv2 = 1.0 / (1.0 + ex)

    oh0, oh1 = row == i0, row == i1
    sel = (oh0 | oh1).astype(F32)
    tok_r = lax.broadcasted_iota(jnp.int32, (ROW_TILE, ROW_TILE), 0)
    tok_c = lax.broadcasted_iota(jnp.int32, (ROW_TILE, ROW_TILE), 1)
    earlier = (tok_r < tok_c).astype(F32).astype(BF16)
    carry = carry_sc[:, 0:1]
    before = jnp.dot(sel.astype(BF16), earlier, preferred_element_type=F32) + carry
    rank0 = jnp.sum(jnp.where(oh0, before, 0.0), axis=0, keepdims=True)
    rank1 = jnp.sum(jnp.where(oh1, before, 0.0), axis=0, keepdims=True)
    count = jnp.broadcast_to(carry + jnp.sum(sel, axis=1, keepdims=True), carry_sc.shape)
    carry_sc[...] = count
    cnt_ref[...] = count[:SUBLANES]

    fields = (i0.astype(F32), i1.astype(F32), inv2, ex * inv2, rank0, rank1)
    meta_t = jnp.zeros((LANES, ROW_TILE), F32)
    frow = lax.broadcasted_iota(jnp.int32, meta_t.shape, 0)
    for k, f in enumerate(fields):
        meta_t = jnp.where(frow == k, f, meta_t)
    metat_ref[...] = meta_t[:SUBLANES]
    meta_ref[...] = jnp.transpose(meta_t)


META_E0, META_E1, META_G0, META_G1, META_RANK0, META_RANK1 = range(6)
ROUTER_ROWS = 16


def _dilout(os, ls, tok, kvm, wout, h, g_ffn, router):
    row = lambda i: (i, 0)
    const = lambda i: (0, 0)

    def group_spec(a):
        d = a.shape[0]
        if d == 1:
            return pl.BlockSpec((None, ROW_TILE, DIL_W), lambda i: (0, i, 0))
        return pl.BlockSpec((d, ROW_TILE // d, DIL_W), lambda i: (0, i, 0))

    return pl.pallas_call(
        _dilout_kernel,
        grid=(SEQ // ROW_TILE,),
        in_specs=[group_spec(a) for a in (*os, *ls)] + [
                  pl.BlockSpec((ROW_TILE, MEM_W), lambda i: (i, DIL_SLAB // MEM_W)),
                  pl.BlockSpec((N_MEM, 2 * MEM_W), lambda i: (0, 1)),
                  pl.BlockSpec((DIL_W + MEM_W, D_MODEL), const),
                  pl.BlockSpec((ROW_TILE, D_MODEL), row),
                  pl.BlockSpec((1, D_MODEL), const),
                  pl.BlockSpec((ROUTER_ROWS, D_MODEL), const)],
        out_specs=[pl.BlockSpec((ROW_TILE, D_MODEL), row),
                   pl.BlockSpec((ROW_TILE, D_MODEL), row),
                   pl.BlockSpec((ROW_TILE, LANES), row),
                   pl.BlockSpec((SUBLANES, LANES), const),
                   pl.BlockSpec((SUBLANES, ROW_TILE), lambda i: (0, i))],
        out_shape=[jax.ShapeDtypeStruct((SEQ, D_MODEL), F32), jax.ShapeDtypeStruct((SEQ, D_MODEL), BF16),
                   jax.ShapeDtypeStruct((SEQ, LANES), F32), jax.ShapeDtypeStruct((SUBLANES, LANES), F32),
                   jax.ShapeDtypeStruct((SUBLANES, SEQ), F32)],
        scratch_shapes=[pltpu.VMEM((ROUTER_ROWS, LANES), F32),
                        pltpu.VMEM((4, DIL_W // LANES, ROW_TILE, LANES), F32)],
        compiler_params=pltpu.CompilerParams(dimension_semantics=("arbitrary",), vmem_limit_bytes=VMEM_LIMIT),
        name="dil_out",
    )(*os, *ls, tok, kvm, wout, h, g_ffn.reshape(1, D_MODEL), router)


def _swiglu_partial(x, wg, wu, wd):
    a = jnp.dot(x, wg, preferred_element_type=F32)
    b = jnp.dot(x, wu, preferred_element_type=F32)
    hm = (a * jax.nn.sigmoid(a) * b).astype(BF16)
    return jnp.dot(hm, wd, preferred_element_type=F32)


def _ffn_kernel(x_ref, wg_ref, wu_ref, wd_ref, h_ref, wa_ref, o_ref, wa_out, zero_out):
    wa_out[...] = wa_ref[...].astype(BF16)
    zero_out[...] = jnp.zeros(zero_out.shape, zero_out.dtype)
    x = x_ref[...]
    acc = h_ref[...]
    for c in range(D_FF // FFN_COL_TILE):
        cols = slice(c * FFN_COL_TILE, (c + 1) * FFN_COL_TILE)
        acc = acc + _swiglu_partial(x, wg_ref[:, cols], wu_ref[:, cols], wd_ref[cols, :])
    o_ref[...] = acc


def _ffn(x, h, wg, wu, wd, wa):
    tm = ROW_TILE
    steps = SEQ // tm
    resident = lambda shape: pl.BlockSpec(shape, lambda i: (0, 0), pipeline_mode=pl.Buffered(1))
    return pl.pallas_call(
        _ffn_kernel,
        grid=(steps,),
        in_specs=[pl.BlockSpec((tm, D_MODEL), lambda i: (i, 0)),
                  resident((D_MODEL, D_FF)), resident((D_MODEL, D_FF)), resident((D_FF, D_MODEL)),
                  pl.BlockSpec((tm, D_MODEL), lambda i: (i, 0)),
                  _cast_slice_spec(wa, steps)],
        out_specs=[pl.BlockSpec((tm, D_MODEL), lambda i: (i, 0)), _cast_slice_spec(wa, steps),
                   pl.BlockSpec((MOE_SLOTS // steps, PACK_W), lambda i: (i, 0))],
        out_shape=[jax.ShapeDtypeStruct((SEQ, D_MODEL), F32), jax.ShapeDtypeStruct(wa.shape, BF16),
                   jax.ShapeDtypeStruct((MOE_SLOTS, PACK_W), jnp.uint32)],
        compiler_params=pltpu.CompilerParams(dimension_semantics=("arbitrary",), vmem_limit_bytes=VMEM_LIMIT),
        name="ffn",
    )(x, wg, wu, wd, h, wa)


MOE_SUB = 512
MOE_ROWS = 2048
MOE_COL_TILE = D_FF // 2
MOE_SUBS = MOE_ROWS // MOE_SUB
MOE_SLOTS = 2 * SEQ + N_EXPERTS * MOE_SUB
MOE_TILES = MOE_SLOTS // MOE_ROWS
MOE_ITEMS = MOE_TILES + N_EXPERTS - 1
PACK_W = D_MODEL // 2
ITEM_TILE, ITEM_EXPERT, ITEM_LO, ITEM_HI, ITEM_FIRST = range(5)


def _pack_bf16_pairs(x):
    lo = lax.bitcast_convert_type(x[:, :PACK_W].astype(F32), jnp.uint32) >> 16
    hi = lax.bitcast_convert_type(x[:, PACK_W:].astype(F32), jnp.uint32) & jnp.uint32(0xFFFF0000)
    return hi | lo


def _unpack_bf16_pairs(pk):
    lo = lax.bitcast_convert_type(pk << 16, F32).astype(BF16)
    hi = lax.bitcast_convert_type(pk & jnp.uint32(0xFFFF0000), F32).astype(BF16)
    return lo, hi


def _vmem_row(ref, group, sub):
    return ref.at[group, pl.ds(sub, 1), :]


def _hbm_row(ref, row):
    return ref.at[pl.ds(row, 1), :]


def _scatter_kernel(pos0_ref, pos1_ref, x_ref, xs_in_ref, xs_ref, pk_sc, sem):
    del xs_in_ref
    pk_sc[...] = _pack_bf16_pairs(x_ref[...]).reshape(pk_sc.shape)

    def issue(g, c):
        for u in range(SUBLANES):
            t = g * SUBLANES + u
            pltpu.make_async_copy(_vmem_row(pk_sc, g, u), _hbm_row(xs_ref, pos0_ref[t]), sem).start(priority=0)
            pltpu.make_async_copy(_vmem_row(pk_sc, g, u), _hbm_row(xs_ref, pos1_ref[t]), sem).start(priority=1)
        return c

    def drain(g, c):
        for _ in range(2 * SUBLANES):
            pltpu.make_async_copy(_vmem_row(pk_sc, 0, 0), _hbm_row(xs_ref, 0), sem).wait()
        return c

    lax.fori_loop(0, ROW_TILE // SUBLANES, issue, 0)
    lax.fori_loop(0, ROW_TILE // SUBLANES, drain, 0)


def _moe_scatter(x, pos0, pos1, zero_slots):
    smem_tile = pl.BlockSpec((ROW_TILE,), lambda i: (i,), memory_space=pltpu.SMEM)
    return pl.pallas_call(
        _scatter_kernel,
        grid=(SEQ // ROW_TILE,),
        in_specs=[smem_tile, smem_tile,
                  pl.BlockSpec((ROW_TILE, D_MODEL), lambda i: (i, 0)),
                  pl.BlockSpec(memory_space=pl.ANY)],
        out_specs=pl.BlockSpec(memory_space=pl.ANY),
        out_shape=jax.ShapeDtypeStruct((MOE_SLOTS, PACK_W), jnp.uint32),
        scratch_shapes=[pltpu.VMEM((ROW_TILE // SUBLANES, SUBLANES, PACK_W), jnp.uint32),
                        pltpu.SemaphoreType.DMA],
        input_output_aliases={3: 0},
        compiler_params=pltpu.CompilerParams(dimension_semantics=("arbitrary",), vmem_limit_bytes=VMEM_LIMIT),
        name="moe_scatter",
    )(pos0, pos1, x, zero_slots)


def _moe_kernel(items_ref, xs_ref, wg_ref, wu_ref, wd_ref, y_ref, xb_sc):
    w = pl.program_id(0)
    j = pl.program_id(1)
    lo = items_ref[ITEM_LO, w]
    hi = items_ref[ITEM_HI, w]

    @pl.when((j == 0) & (hi > lo))
    def _():
        xlo, xhi = _unpack_bf16_pairs(xs_ref[...])
        xb_sc[:, :PACK_W] = xlo
        xb_sc[:, PACK_W:] = xhi

    @pl.when((j == 0) & (items_ref[ITEM_FIRST, w] == 1))
    def _():
        y_ref[...] = jnp.zeros_like(y_ref)

    for sb in range(MOE_SUBS):
        @pl.when((lo <= sb) & (sb < hi))
        def _(sb=sb):
            rows = slice(sb * MOE_SUB, (sb + 1) * MOE_SUB)
            x = xb_sc[rows, :]
            acc = y_ref[rows, :]
            for c0 in range(0, MOE_COL_TILE, FFN_COL_TILE):
                cols = slice(c0, min(c0 + FFN_COL_TILE, MOE_COL_TILE))
                acc = acc + _swiglu_partial(x, wg_ref[0, :, cols], wu_ref[0, :, cols], wd_ref[0, cols, :])
            y_ref[rows, :] = acc


def _moe_grouped(items, xs, wg, wu, wd):
    tf = MOE_COL_TILE
    nf = D_FF // tf
    jf = lambda w, j, it: jnp.where(it[ITEM_HI, w] > it[ITEM_LO, w], j, nf - 1)
    grid_spec = pltpu.PrefetchScalarGridSpec(
        num_scalar_prefetch=1,
        grid=(MOE_ITEMS, nf),
        in_specs=[pl.BlockSpec((MOE_ROWS, PACK_W), lambda w, j, it: (it[ITEM_TILE, w], 0)),
                  pl.BlockSpec((1, D_MODEL, tf), lambda w, j, it: (it[ITEM_EXPERT, w], 0, jf(w, j, it))),
                  pl.BlockSpec((1, D_MODEL, tf), lambda w, j, it: (it[ITEM_EXPERT, w], 0, jf(w, j, it))),
                  pl.BlockSpec((1, tf, D_MODEL), lambda w, j, it: (it[ITEM_EXPERT, w], jf(w, j, it), 0))],
        out_specs=pl.BlockSpec((MOE_ROWS, D_MODEL), lambda w, j, it: (it[ITEM_TILE, w], 0)),
        scratch_shapes=[pltpu.VMEM((MOE_ROWS, D_MODEL), BF16)],
    )
    return pl.pallas_call(
        _moe_kernel,
        grid_spec=grid_spec,
        out_shape=jax.ShapeDtypeStruct((MOE_SLOTS, D_MODEL), F32),
        compiler_params=pltpu.CompilerParams(dimension_semantics=("arbitrary", "arbitrary"),
                                             vmem_limit_bytes=VMEM_LIMIT),
        name="moe_grouped",
    )(items, xs, wg, wu, wd)


def _combine_kernel(pos0_ref, pos1_ref, meta_ref, h_ref, gf_ref, y_ref, o_ref, y0_sc, y1_sc, sem):
    def issue(g, c):
        for u in range(SUBLANES):
            t = g * SUBLANES + u
            pltpu.make_async_copy(_hbm_row(y_ref, pos0_ref[t]), _vmem_row(y0_sc, g, u), sem).start(priority=0)
            pltpu.make_async_copy(_hbm_row(y_ref, pos1_ref[t]), _vmem_row(y1_sc, g, u), sem).start(priority=1)
        return c

    def drain(g, c):
        for _ in range(SUBLANES):
            pltpu.make_async_copy(_hbm_row(y_ref, 0), _vmem_row(y0_sc, 0, 0), sem).wait()
            pltpu.make_async_copy(_hbm_row(y_ref, 0), _vmem_row(y1_sc, 0, 0), sem).wait()
        return c

    lax.fori_loop(0, ROW_TILE // SUBLANES, issue, 0)
    lax.fori_loop(0, ROW_TILE // SUBLANES, drain, 0)
    meta = meta_ref[...]
    g0 = meta[:, META_G0:META_G0 + 1]
    g1 = meta[:, META_G1:META_G1 + 1]
    y0 = y0_sc[...].reshape(ROW_TILE, D_MODEL)
    y1 = y1_sc[...].reshape(ROW_TILE, D_MODEL)
    o_ref[...] = _rms(h_ref[...] + (g0 * y0 + g1 * y1), gf_ref[...])


def _moe_combine(pos0, pos1, meta, h, g_final, y):
    smem_tile = pl.BlockSpec((ROW_TILE,), lambda i: (i,), memory_space=pltpu.SMEM)
    tiles = (ROW_TILE // SUBLANES, SUBLANES, D_MODEL)
    return pl.pallas_call(
        _combine_kernel,
        grid=(SEQ // ROW_TILE,),
        in_specs=[smem_tile, smem_tile,
                  pl.BlockSpec((ROW_TILE, LANES), lambda i: (i, 0)),
                  pl.BlockSpec((ROW_TILE, D_MODEL), lambda i: (i, 0)),
                  pl.BlockSpec((1, D_MODEL), lambda i: (0, 0)),
                  pl.BlockSpec(memory_space=pl.ANY)],
        out_specs=pl.BlockSpec((ROW_TILE, D_MODEL), lambda i: (i, 0)),
        out_shape=jax.ShapeDtypeStruct((SEQ, D_MODEL), F32),
        scratch_shapes=[pltpu.VMEM(tiles, F32), pltpu.VMEM(tiles, F32), pltpu.SemaphoreType.DMA],
        compiler_params=pltpu.CompilerParams(dimension_semantics=("arbitrary",), vmem_limit_bytes=VMEM_LIMIT),
        name="moe_combine",
    )(pos0, pos1, meta, h, g_final.reshape(1, D_MODEL), y)


def _moe_plan(meta_t, counts):
    i32 = jnp.int32
    cnt = counts[:N_EXPERTS, 0].astype(i32)
    padded = (cnt + MOE_SUB - 1) // MOE_SUB * MOE_SUB
    start = jnp.cumsum(padded) - padded

    def slot(e_row, rank_row):
        e = meta_t[e_row].astype(i32)
        base = sum(jnp.where(e == k, start[k], 0) for k in range(N_EXPERTS))
        return base + meta_t[rank_row].astype(i32)

    pos0, pos1 = slot(META_E0, META_RANK0), slot(META_E1, META_RANK1)

    first_sb, end_sb = start // MOE_SUB, (start + padded) // MOE_SUB
    t0 = jnp.arange(MOE_TILES, dtype=i32)[:, None] * MOE_SUBS
    lo = jnp.clip(first_sb[None, :] - t0, 0, MOE_SUBS)
    hi = jnp.clip(end_sb[None, :] - t0, 0, MOE_SUBS)
    active = hi > lo
    none = ~jnp.any(active, axis=1, keepdims=True)
    active = jnp.concatenate([active, none], axis=1)
    lo = jnp.concatenate([lo, jnp.zeros_like(t0)], axis=1)
    hi = jnp.concatenate([hi, jnp.zeros_like(t0)], axis=1)
    first = active & (jnp.cumsum(active.astype(i32), axis=1) == 1)
    ncol = N_EXPERTS + 1
    flat = jnp.arange(MOE_TILES * ncol, dtype=i32)
    key = jnp.where(active.reshape(-1), flat, MOE_TILES * ncol)
    order = jnp.argsort(key)[:MOE_ITEMS]
    live = key[order] < MOE_TILES * ncol
    tile = jnp.where(live, order // ncol, MOE_TILES - 1)
    lo_i = jnp.where(live, lo.reshape(-1)[order], 0)
    hi_i = jnp.where(live, hi.reshape(-1)[order], 0)
    first_i = jnp.where(live, first.reshape(-1)[order], False).astype(i32)
    idx = jnp.arange(MOE_ITEMS, dtype=i32)
    src = jnp.maximum(lax.cummax(jnp.where(hi_i > lo_i, idx, -1), axis=0), 0)
    expert = jnp.minimum(order % ncol, N_EXPERTS - 1)[src]
    items = jnp.stack([tile, expert, lo_i, hi_i, first_i]).astype(i32)
    return pos0, pos1, items


def _take_runs(w, idx, axis):
    idx = list(idx)
    runs, start = [], 0
    for i in range(1, len(idx) + 1):
        if i == len(idx) or idx[i] != idx[i - 1] + 1:
            runs.append(lax.slice_in_dim(w, idx[start], idx[i - 1] + 1, axis=axis))
            start = i
    return jnp.concatenate(runs, axis=axis)


def _swa_weight_layout():
    head = lambda base, j: list(range(base + j * HEAD_DIM, base + (j + 1) * HEAD_DIM))
    k0, v0, m0 = SWA_Q, SWA_Q + SWA_KV, SWA_Q + 2 * SWA_KV
    cols, rows = [], []
    for a, b in SWA_PAIR_SLABS:
        cols += head(0, a) + head(0, b)
        rows += head(0, a) + head(0, b)
    for base in (k0, v0):
        cols += head(base, 0) + head(base, 1) + head(base, 2) + head(base, 2)
    cols += list(range(m0, m0 + MEM_W))
    rows += list(range(SWA_Q, SWA_Q + MEM_W))
    scale = np.ones((len(cols),), np.float32)
    scale[:SWA_Q] = Q_SCALE
    scale[SWA_M0:] = Q_SCALE
    assert len(cols) == SWA_PW and len(rows) == D_MODEL
    return np.asarray(cols), scale, np.asarray(rows)


def _dil_weight_layout():
    n_grp = len(DIL_GROUPS) * DIL_SLAB
    cols = list(range(DIL_SLAB)) + list(range(n_grp, n_grp + MEM_W)) + list(range(DIL_SLAB, n_grp))
    scale = np.ones((len(cols),), np.float32)
    for c0 in (0, DIL_SLAB, TOK_W, TOK_W + DIL_SLAB):
        scale[c0:c0 + DIL_W] = Q_SCALE
    return np.asarray(cols), scale


def kernel(x, mem, rel_bias_table, mem_norm, norm_mix, norm_ffn, final_norm, swa_w_in, swa_sinks, swa_w_mem_kv,
           swa_w_out, dil_w_in, dil_w_mem_kv, dil_w_out, ffn_gate, ffn_up, ffn_down, router, moe_gate, moe_up,
           moe_down):
    assert x.shape == (1, SEQ, D_MODEL) and mem.shape == (1, N_MEM, D_MODEL)
    assert norm_mix.shape == (2, D_MODEL) and swa_w_in.shape == (1, D_MODEL, SWA_IN)
    assert dil_w_in.shape == (1, D_MODEL, DIL_IN) and moe_gate.shape == (1, N_EXPERTS, D_MODEL, D_FF)
    bf = lambda a: a.astype(BF16)
    h0 = x.reshape(SEQ, D_MODEL)

    kvm = _memkv(mem[0], mem_norm, bf(jnp.concatenate([swa_w_mem_kv[0], dil_w_mem_kv[0]], axis=1)))

    cols0, scale0, rows0 = _swa_weight_layout()
    h1, hn1, moe_up_b, moe_down_b, ffn_gate_b, ffn_up_b, ffn_down_b = _swa_layer(
        bf(_take_runs(swa_w_in[0], cols0, 1) * scale0), norm_mix[0], kvm, rel_bias_table, swa_sinks[0],
        bf(_take_runs(swa_w_out[0], rows0, 0)), h0, norm_ffn[0],
        [moe_up[0], moe_down[0], ffn_gate, ffn_up, ffn_down])
    h2, moe_gate_b, xs_zero = _ffn(hn1, h1, ffn_gate_b[0], ffn_up_b[0], ffn_down_b[0], moe_gate[0])

    cols1, scale1 = _dil_weight_layout()
    tok, g1, g2 = _dilproj(h2, norm_mix[1], bf(_take_runs(dil_w_in[0], cols1, 1) * scale1))
    groups = [_dil_group(qkv, rel_bias_table, gi) for gi, qkv in enumerate((tok[None], g1, g2))]
    router_p = jnp.pad(bf(router[0].T), ((0, ROUTER_ROWS - N_EXPERTS), (0, 0)))
    h3, hn3, meta, counts, meta_t = _dilout([g[0] for g in groups], [g[1] for g in groups], tok, kvm,
                                            bf(dil_w_out[0]), h2, norm_ffn[1], router_p)
    pos0, pos1, items = _moe_plan(meta_t, counts)
    xs = _moe_scatter(hn3, pos0, pos1, xs_zero)
    y = _moe_grouped(items, xs, moe_gate_b, moe_up_b, moe_down_b)
    out = _moe_combine(pos0, pos1, meta, h3, final_norm, y)
    return out.reshape(1, SEQ, D_MODEL)
```

```python
import functools
import math

import jax
import jax.numpy as jnp
import numpy as np
from jax import lax
from jax.experimental import pallas as pl
from jax.experimental.pallas import tpu as pltpu

F32 = jnp.float32
BF16 = jnp.bfloat16

D_MODEL = 1024
SEQ = 16384
HEAD_DIM = 64
N_MIX_HEADS = 12
SWA_KV_HEADS = 3
SWA_GROUP = N_MIX_HEADS // SWA_KV_HEADS
SWA_WINDOW = 128
DIL_GROUPS = ((128, 1), (512, 4), (2048, 16))
DIL_HEADS = 4
N_MEM = 256
MEM_HEADS = 4
BLOCK = 128
N_BUCKETS = 32
MAX_DISTANCE = 2048
D_FF = 3584
N_EXPERTS = 8
EPS = 1e-5
NEG = -1e30
Q_SCALE = HEAD_DIM ** -0.5

SWA_Q = N_MIX_HEADS * HEAD_DIM
SWA_KV = SWA_KV_HEADS * HEAD_DIM
SWA_IN = SWA_Q + 2 * SWA_KV + MEM_HEADS * HEAD_DIM
DIL_W = DIL_HEADS * HEAD_DIM
DIL_IN = len(DIL_GROUPS) * 3 * DIL_W + MEM_HEADS * HEAD_DIM
MEM_W = MEM_HEADS * HEAD_DIM

LANES = 128
SUBLANES = 8
VMEM_LIMIT = 56 * 1024 * 1024

ROW_TILE = 512
BLOCKS_PER_TILE = ROW_TILE // BLOCK
FFN_COL_TILE = 256


def _bucket_map(dil):
    qi = np.arange(BLOCK)[:, None]
    kj = np.arange(2 * BLOCK)[None, :]
    d = np.maximum((qi + BLOCK - kj) * dil, 0)
    max_exact = N_BUCKETS // 2
    ratio = np.maximum(d, 1).astype(np.float32) / np.float32(max_exact)
    large = max_exact + (np.log(ratio) / np.float32(math.log(MAX_DISTANCE / max_exact))
                         * np.float32(N_BUCKETS - max_exact)).astype(np.int32)
    return np.where(d < max_exact, d, np.minimum(large, N_BUCKETS - 1)).astype(np.int32)


def _rms(x, g):
    ms = jnp.mean(x * x, axis=-1, keepdims=True)
    return x * lax.rsqrt(ms + EPS) * g


def _dot_nt(a, b):
    return lax.dot_general(a, b, (((1,), (1,)), ((), ())), preferred_element_type=F32)


def _dot_tn(a, b):
    return lax.dot_general(a, b, (((0,), (0,)), ((), ())), preferred_element_type=F32)


def _fill_bias(bias_sc, tab_ref, bucket, heads):
    for j, h in enumerate(heads):
        def body(k, b, h=h):
            return jnp.where(bucket == k, tab_ref[k, h], b)
        bias_sc[:, j * BLOCK:(j + 1) * BLOCK] = lax.fori_loop(0, N_BUCKETS, body, jnp.zeros(bucket.shape, F32))


def _fill_row(row_sc, values):
    blk = lax.broadcasted_iota(jnp.int32, row_sc.shape, 1) // BLOCK
    row = jnp.zeros(row_sc.shape, F32)
    for j, v in enumerate(values):
        row = jnp.where(blk == j, v, row)
    row_sc[...] = row


def _folded_bucket_map(dil):
    full = _bucket_map(dil)
    qi = np.arange(BLOCK)[:, None]
    c = np.arange(BLOCK)[None, :]
    return np.ascontiguousarray(np.where(c <= qi, full[:, BLOCK:], full[:, :BLOCK]).T.astype(np.int32))


def _diag_bucket(dil):
    return int(_bucket_map(dil)[0, 0])


def _fold_masks(n):
    c = lax.broadcasted_iota(jnp.int32, (BLOCK, n), 0)
    qi = lax.broadcasted_iota(jnp.int32, (BLOCK, n), 1) % BLOCK
    return c <= qi, c == qi


def _pair_scores(k_slab, q_rows):
    lane = lax.broadcasted_iota(jnp.int32, q_rows.shape, 1)
    zero = jnp.zeros_like(q_rows)
    qa = jnp.where(lane < HEAD_DIM, q_rows, zero)
    qb = jnp.where(lane < HEAD_DIM, zero, q_rows)
    return _dot_nt(k_slab, qa), _dot_nt(k_slab, qb)


def _pair_values(v_slab, p_a, p_b):
    n = p_a.shape[1]
    o = _dot_tn(v_slab, jnp.concatenate([p_a, p_b], axis=1))
    row = lax.broadcasted_iota(jnp.int32, (LANES, n), 0)
    return jnp.where(row < HEAD_DIM, o[:, :n], o[:, n:])


def _band_softmax(st, bias, own, eye, has_prev, sink, diag_bias):
    s_prev, s_own = st[:BLOCK], st[BLOCK:]
    t = jnp.where(own, s_own, s_prev) + bias
    if has_prev is not None:
        t = jnp.where(own | has_prev, t, NEG)
    m = jnp.max(t, axis=0, keepdims=True)
    if diag_bias is not None:
        s_d = jnp.sum(jnp.where(eye, s_prev, 0.0), axis=0, keepdims=True) + diag_bias
        if has_prev is not None:
            s_d = jnp.where(has_prev, s_d, NEG)
        m = jnp.maximum(m, s_d)
    if sink is not None:
        m = jnp.maximum(m, sink)
    e = jnp.exp(t - m)
    den = jnp.sum(e, axis=0, keepdims=True)
    if diag_bias is not None:
        e_d = jnp.exp(s_d - m)
        den = den + e_d
    if sink is not None:
        den = den + jnp.exp(sink - m)
    inv = 1.0 / den
    p = e * inv
    p_own = jnp.where(own, p, 0.0)
    p_prev = jnp.where(own, 0.0, p)
    if diag_bias is not None:
        p_prev = jnp.where(eye, e_d * inv, p_prev)
    return jnp.concatenate([p_prev, p_own], axis=0).astype(BF16), m + jnp.log(den)


def _mem_attention(qm, kvm):
    outs = []
    for s in range(MEM_W // LANES):
        sa, sb = _pair_scores(kvm[:, s * LANES:(s + 1) * LANES], qm[:, s * LANES:(s + 1) * LANES])
        ps = []
        for st in (sa, sb):
            e = jnp.exp(st - jnp.max(st, axis=0, keepdims=True))
            ps.append((e * (1.0 / jnp.sum(e, axis=0, keepdims=True))).astype(BF16))
        outs.append(_pair_values(kvm[:, MEM_W + s * LANES:MEM_W + (s + 1) * LANES], ps[0], ps[1]))
    return outs


def _to_rows(slabs_t):
    return jnp.concatenate([jnp.transpose(x).astype(BF16) for x in slabs_t], axis=-1)


def _memkv_kernel(mem_ref, g_ref, w_ref, o_ref):
    mn = _rms(mem_ref[...], g_ref[...]).astype(BF16)
    o_ref[...] = jnp.dot(mn, w_ref[...], preferred_element_type=F32).astype(BF16)


def _memkv(mem, g, w):
    return pl.pallas_call(
        _memkv_kernel,
        out_shape=jax.ShapeDtypeStruct((N_MEM, w.shape[1]), BF16),
        name="mem_kv",
    )(mem, g.reshape(1, D_MODEL), w)


SWA_PAIR_SLABS = [(g, SWA_GROUP + g) for g in range(SWA_GROUP)] + [(8, 9), (10, 11)]
SWA_K0 = SWA_Q
SWA_V0 = SWA_K0 + 2 * LANES
SWA_M0 = SWA_V0 + 2 * LANES
SWA_PW = SWA_M0 + MEM_W
SWA_SCORE_HEADS = [0, 1, 2, 3, 4, 5, 6, 7, 8, 10, 9, 11]
SWA_NQ = N_MIX_HEADS * BLOCK


def _cast_slice_spec(w, steps):
    e, r, c = w.shape
    per_e = steps // e
    assert steps % e == 0 and r % per_e == 0 and (r // per_e) % 16 == 0
    return pl.BlockSpec((1, r // per_e, c), lambda i: (i // per_e, i % per_e, 0))


def _swa_kernel(*refs, n_cast):
    (tab_ref, sink_ref, bucket_ref, gmix_ref, win_ref, kvm_ref, wout_ref, h_ref, g_ref), refs = refs[:9], refs[9:]
    cast_in, (h1_ref, hn1_ref), refs = refs[:n_cast], refs[n_cast:n_cast + 2], refs[n_cast + 2:]
    cast_out, (bias_sc, sink_sc, k_sc, v_sc, cat_sc, p_ref, kprev_sc, vprev_sc) = refs[:n_cast], refs[n_cast:]
    i = pl.program_id(0)
    for src, dst in zip(cast_in, cast_out):
        dst[...] = src[...].astype(BF16)

    @pl.when(i == 0)
    def _():
        _fill_bias(bias_sc, tab_ref, bucket_ref[...], SWA_SCORE_HEADS)
        _fill_row(sink_sc, [sink_ref[h] for h in SWA_SCORE_HEADS])

        kprev_sc[...] = jnp.zeros(kprev_sc.shape, BF16)
        vprev_sc[...] = jnp.zeros(vprev_sc.shape, BF16)

    k_sc[0:BLOCK, :] = kprev_sc[...]
    v_sc[0:BLOCK, :] = vprev_sc[...]

    hn = _rms(h_ref[...], gmix_ref[...]).astype(BF16)
    p_ref[...] = jnp.dot(hn, win_ref[...], preferred_element_type=F32).astype(BF16)

    k_sc[BLOCK:, :] = p_ref[:, SWA_K0:SWA_V0]
    v_sc[BLOCK:, :] = p_ref[:, SWA_V0:SWA_M0]
    own, eye = _fold_masks(SWA_NQ)

    def block_body(b, carry):
        r0 = b * BLOCK
        has_prev = (i > 0) if b == 0 else None
        qb = p_ref[pl.ds(r0, BLOCK), 0:SWA_Q]
        slab = lambda x, s: x[:, s * LANES:(s + 1) * LANES]
        qa = jnp.concatenate([slab(qb, s) for s in range(4)], axis=0)
        qc = jnp.concatenate([slab(qb, s) for s in (4, 5)], axis=0)
        kb = k_sc[pl.ds(r0, 2 * BLOCK), :]
        vb = v_sc[pl.ds(r0, 2 * BLOCK), :]
        s0, s1 = _pair_scores(slab(kb, 0), qa)
        s2, s3 = _pair_scores(slab(kb, 1), qc)
        st = jnp.concatenate([s0, s1, s2, s3], axis=1)
        pt, _ = _band_softmax(st, bias_sc[...], own, eye, has_prev, sink_sc[...], None)
        na, nc = 4 * BLOCK, 2 * BLOCK
        oa = _pair_values(slab(vb, 0), pt[:, 0:na], pt[:, na:2 * na])
        oc = _pair_values(slab(vb, 1), pt[:, 2 * na:2 * na + nc], pt[:, 2 * na + nc:])
        outs = [slab(oa, s) for s in range(4)] + [slab(oc, s) for s in range(2)]
        outs += _mem_attention(p_ref[pl.ds(r0, BLOCK), SWA_M0:], kvm_ref[...])
        cat_sc[pl.ds(r0, BLOCK), :] = _to_rows(outs)
        return carry

    for b in range(BLOCKS_PER_TILE):
        block_body(b, None)

    kprev_sc[...] = p_ref[ROW_TILE - BLOCK:, SWA_K0:SWA_V0]
    vprev_sc[...] = p_ref[ROW_TILE - BLOCK:, SWA_V0:SWA_M0]

    out = h_ref[...] + jnp.dot(cat_sc[...], wout_ref[...], preferred_element_type=F32)
    h1_ref[...] = out
    hn1_ref[...] = _rms(out, g_ref[...]).astype(BF16)


def _swa_layer(w_in, g_mix, kvm, table, sinks, wout, h, g_ffn, casts):
    assert SWA_WINDOW == BLOCK
    kv_w = 2 * LANES
    steps = SEQ // ROW_TILE
    cast_specs = [_cast_slice_spec(w, steps) for w in casts]
    return pl.pallas_call(
        functools.partial(_swa_kernel, n_cast=len(casts)),
        grid=(SEQ // ROW_TILE,),
        in_specs=[
            pl.BlockSpec(memory_space=pltpu.SMEM),
            pl.BlockSpec(memory_space=pltpu.SMEM),
            pl.BlockSpec((BLOCK, BLOCK), lambda i: (0, 0)),
            pl.BlockSpec((1, D_MODEL), lambda i: (0, 0)),
            pl.BlockSpec((D_MODEL, SWA_PW), lambda i: (0, 0)),
            pl.BlockSpec((N_MEM, 2 * MEM_W), lambda i: (0, 0)),
            pl.BlockSpec((D_MODEL, D_MODEL), lambda i: (0, 0)),
            pl.BlockSpec((ROW_TILE, D_MODEL), lambda i: (i, 0)),
            pl.BlockSpec((1, D_MODEL), lambda i: (0, 0)),
        ] + cast_specs,
        out_specs=[pl.BlockSpec((ROW_TILE, D_MODEL), lambda i: (i, 0)),
                   pl.BlockSpec((ROW_TILE, D_MODEL), lambda i: (i, 0))] + cast_specs,
        out_shape=[jax.ShapeDtypeStruct((SEQ, D_MODEL), F32), jax.ShapeDtypeStruct((SEQ, D_MODEL), BF16)]
                  + [jax.ShapeDtypeStruct(w.shape, BF16) for w in casts],
        scratch_shapes=[pltpu.VMEM((BLOCK, SWA_NQ), F32),
                        pltpu.VMEM((1, SWA_NQ), F32),
                        pltpu.VMEM((ROW_TILE + BLOCK, kv_w), BF16),
                        pltpu.VMEM((ROW_TILE + BLOCK, kv_w), BF16),
                        pltpu.VMEM((ROW_TILE, D_MODEL), BF16),
                        pltpu.VMEM((ROW_TILE, SWA_PW), BF16),
                        pltpu.VMEM((BLOCK, kv_w), BF16),
                        pltpu.VMEM((BLOCK, kv_w), BF16)],
        compiler_params=pltpu.CompilerParams(dimension_semantics=("arbitrary",), vmem_limit_bytes=VMEM_LIMIT),
        name="swa_mixer",
    )(table, sinks, jnp.asarray(_folded_bucket_map(1)), g_mix.reshape(1, D_MODEL), w_in, kvm, wout, h,
      g_ffn.reshape(1, D_MODEL), *casts)


DIL_SLAB = 3 * DIL_W
TOK_W = DIL_SLAB + MEM_W
SLABS_PER_GROUP = DIL_SLAB // LANES


def _dilproj_kernel(h_ref, g_ref, w_ref, tok_ref, g1_ref, g2_ref, slab_sc):
    hn = _rms(h_ref[...], g_ref[...]).astype(BF16)
    res = jnp.dot(hn, w_ref[...], preferred_element_type=F32)
    tok_ref[...] = res[:, :TOK_W].astype(BF16)
    for s in range(2 * SLABS_PER_GROUP):
        slab_sc[s] = res[:, TOK_W + s * LANES:TOK_W + (s + 1) * LANES]
    for gi, out_ref in ((1, g1_ref), (2, g2_ref)):
        d = DIL_GROUPS[gi][1]
        for s in range(SLABS_PER_GROUP):
            for r in range(d):
                rows = slab_sc[(gi - 1) * SLABS_PER_GROUP + s, pl.ds(r, ROW_TILE // d, stride=d), :]
                out_ref[r, :, s * LANES:(s + 1) * LANES] = rows.astype(BF16)


def _dilproj(h, g, w):
    d1, d2 = DIL_GROUPS[1][1], DIL_GROUPS[2][1]
    return pl.pallas_call(
        _dilproj_kernel,
        grid=(SEQ // ROW_TILE,),
        in_specs=[pl.BlockSpec((ROW_TILE, D_MODEL), lambda i: (i, 0)),
                  pl.BlockSpec((1, D_MODEL), lambda i: (0, 0)),
                  pl.BlockSpec((D_MODEL, DIL_IN), lambda i: (0, 0))],
        out_specs=[pl.BlockSpec((ROW_TILE, TOK_W), lambda i: (i, 0)),
                   pl.BlockSpec((d1, ROW_TILE // d1, DIL_SLAB), lambda i: (0, i, 0)),
                   pl.BlockSpec((d2, ROW_TILE // d2, DIL_SLAB), lambda i: (0, i, 0))],
        out_shape=[jax.ShapeDtypeStruct((SEQ, TOK_W), BF16),
                   jax.ShapeDtypeStruct((d1, SEQ // d1, DIL_SLAB), BF16),
                   jax.ShapeDtypeStruct((d2, SEQ // d2, DIL_SLAB), BF16)],
        scratch_shapes=[pltpu.VMEM((2 * SLABS_PER_GROUP, ROW_TILE, LANES), F32)],
        compiler_params=pltpu.CompilerParams(dimension_semantics=("arbitrary",), vmem_limit_bytes=VMEM_LIMIT),
        name="dil_proj",
    )(h, g.reshape(1, D_MODEL), w)


DIL_TILE = 1024
DIL_STEPS = SEQ // DIL_TILE


def _dil_step(tab_ref, bucket, q_ref, k_ref, v_ref, kp_ref, vp_ref, o_ref, l_ref, bias_sc, diag_sc, k_sc, v_sc,
              *, first, n, head0, diag_bucket):
    heads = [head0 + h for h in range(DIL_HEADS)]
    nq = DIL_HEADS * BLOCK

    @pl.when(first)
    def _():
        _fill_bias(bias_sc, tab_ref, bucket, heads)
        _fill_row(diag_sc, [tab_ref[diag_bucket, h] for h in heads])

    k_sc[0:BLOCK, :] = kp_ref[...]
    k_sc[BLOCK:, :] = k_ref[...]
    v_sc[0:BLOCK, :] = vp_ref[...]
    v_sc[BLOCK:, :] = v_ref[...]
    own, eye = _fold_masks(nq)
    upper_rows = lax.broadcasted_iota(jnp.int32, (BLOCK, BLOCK), 0) < HEAD_DIM

    def block_body(b, carry):
        r0 = b * BLOCK
        has_prev = (n > 0) if b == 0 else None
        qb = q_ref[pl.ds(r0, BLOCK), :]
        kb = k_sc[pl.ds(r0, 2 * BLOCK), :]
        vb = v_sc[pl.ds(r0, 2 * BLOCK), :]
        slab = lambda x, s: x[:, s * LANES:(s + 1) * LANES]
        scores = []
        for s in range(DIL_W // LANES):
            scores += _pair_scores(slab(kb, s), slab(qb, s))
        pt, lse = _band_softmax(jnp.concatenate(scores, axis=1), bias_sc[...], own, eye, has_prev, None,
                                diag_sc[...])
        outs, lses = [], []
        for s in range(DIL_W // LANES):
            ca, cb = 2 * s * BLOCK, (2 * s + 1) * BLOCK
            outs.append(jnp.transpose(_pair_values(slab(vb, s), pt[:, ca:ca + BLOCK], pt[:, cb:cb + BLOCK])))
            lse_t = jnp.where(upper_rows, jnp.broadcast_to(lse[:, ca:ca + BLOCK], (BLOCK, BLOCK)),
                              jnp.broadcast_to(lse[:, cb:cb + BLOCK], (BLOCK, BLOCK)))
            lses.append(jnp.transpose(lse_t))
        o_ref[pl.ds(r0, BLOCK), :] = jnp.concatenate(outs, axis=-1)
        l_ref[pl.ds(r0, BLOCK), :] = jnp.concatenate(lses, axis=-1)
        return carry

    for b in range(DIL_TILE // BLOCK):
        block_body(b, None)


def _dil_kernel(*refs):
    n_grp = len(DIL_GROUPS)
    tab_ref, bucket_ref = refs[:2]
    ins, outs, scratch = refs[2:2 + 5 * n_grp], refs[2 + 5 * n_grp:2 + 7 * n_grp], refs[2 + 7 * n_grp:]
    s = pl.program_id(0)
    for gi, (_, d) in enumerate(DIL_GROUPS):
        @pl.when(s // DIL_STEPS == gi)
        def _(gi=gi, d=d):
            local = s - gi * DIL_STEPS
            tiles = SEQ // d // DIL_TILE
            _dil_step(tab_ref, bucket_ref[gi], *ins[5 * gi:5 * gi + 5], *outs[2 * gi:2 * gi + 2], *scratch,
                      first=local == 0, n=local % tiles, head0=gi * DIL_HEADS, diag_bucket=_diag_bucket(d))


def _dil_attention(qkvs, table):
    in_specs = [pl.BlockSpec(memory_space=pltpu.SMEM),
                pl.BlockSpec((len(DIL_GROUPS), BLOCK, BLOCK), lambda s: (0, 0, 0))]
    out_specs, out_shape, operands = [], [], []
    for gi, ((window, d), qkv) in enumerate(zip(DIL_GROUPS, qkvs)):
        rows = SEQ // d
        tiles = rows // DIL_TILE
        assert qkv.shape[:2] == (d, rows) and window // d == BLOCK and d * tiles == DIL_STEPS

        def pos(s, gi=gi, tiles=tiles):
            local = jnp.clip(s - gi * DIL_STEPS, 0, DIL_STEPS - 1)
            return local // tiles, local % tiles

        def tile(c, pos=pos):
            return pl.BlockSpec((None, DIL_TILE, DIL_W), lambda s: (*pos(s), c))

        def prev_block(c, pos=pos):
            def index(s):
                r, n = pos(s)
                return r, jnp.maximum(n * (DIL_TILE // BLOCK) - 1, 0), c
            return pl.BlockSpec((None, BLOCK, DIL_W), index)

        in_specs += [tile(0), tile(1), tile(2), prev_block(1), prev_block(2)]
        operands += [qkv] * 5
        out_specs += [tile(0), tile(0)]
        out_shape += [jax.ShapeDtypeStruct((d, rows, DIL_W), F32)] * 2
    buckets = jnp.asarray(np.stack([_folded_bucket_map(d) for _, d in DIL_GROUPS]))
    res = pl.pallas_call(
        _dil_kernel,
        grid=(len(DIL_GROUPS) * DIL_STEPS,),
        in_specs=in_specs,
        out_specs=out_specs,
        out_shape=out_shape,
        scratch_shapes=[pltpu.VMEM((BLOCK, DIL_HEADS * BLOCK), F32),
                        pltpu.VMEM((1, DIL_HEADS * BLOCK), F32),
                        pltpu.VMEM((DIL_TILE + BLOCK, DIL_W), BF16),
                        pltpu.VMEM((DIL_TILE + BLOCK, DIL_W), BF16)],
        compiler_params=pltpu.CompilerParams(dimension_semantics=("arbitrary",), vmem_limit_bytes=VMEM_LIMIT),
        name="dil_attn",
    )(table, buckets, *operands)
    return [(res[2 * gi], res[2 * gi + 1]) for gi in range(len(DIL_GROUPS))]


def _dilout_kernel(o0_ref, o1_ref, o2_ref, l0_ref, l1_ref, l2_ref, qm_ref, kvm_ref, wout_ref, h_ref, g_ref,
                   router_ref, h2_ref, hn2_ref, meta_ref, cnt_ref, metat_ref, carry_sc, tok_sc):
    @pl.when(pl.program_id(0) == 0)
    def _():
        carry_sc[...] = jnp.zeros_like(carry_sc)

    for k, src_ref in enumerate((o1_ref, l1_ref, o2_ref, l2_ref)):
        d = src_ref.shape[0]
        for s in range(DIL_W // LANES):
            for r in range(d):
                tok_sc[k, s, pl.ds(r, ROW_TILE // d, stride=d), :] = src_ref[r, :, s * LANES:(s + 1) * LANES]

    mixed = []
    for s in range(DIL_W // LANES):
        cols = slice(s * LANES, (s + 1) * LANES)
        o0, o1, o2 = o0_ref[:, cols], tok_sc[0, s], tok_sc[2, s]
        l0, l1, l2 = l0_ref[:, cols], tok_sc[1, s], tok_sc[3, s]
        mx = jnp.maximum(jnp.maximum(l0, l1), l2)
        e0, e1, e2 = jnp.exp(l0 - mx), jnp.exp(l1 - mx), jnp.exp(l2 - mx)
        inv = 1.0 / (e0 + e1 + e2)
        mixed.append(((e0 * inv) * o0 + (e1 * inv) * o1 + (e2 * inv) * o2).astype(BF16))
    cat = jnp.concatenate(mixed + [_to_rows(_mem_attention(qm_ref[...], kvm_ref[...]))], axis=-1)
    out = h_ref[...] + jnp.dot(cat, wout_ref[...], preferred_element_type=F32)
    h2_ref[...] = out
    hn = _rms(out, g_ref[...]).astype(BF16)
    hn2_ref[...] = hn

    logits = _dot_nt(router_ref[...], hn)
    row = lax.broadcasted_iota(jnp.int32, logits.shape, 0)
    masked = jnp.where(row < N_EXPERTS, logits, -jnp.inf)
    v0 = jnp.max(masked, axis=0, keepdims=True)
    i0 = jnp.min(jnp.where(masked == v0, row, ROUTER_ROWS), axis=0, keepdims=True)
    rest = jnp.where(row == i0, -jnp.inf, masked)
    v1 = jnp.max(rest, axis=0, keepdims=True)
    i1 = jnp.min(jnp.where(rest == v1, row, ROUTER_ROWS), axis=0, keepdims=True)
    ex = jnp.exp(v1 - v0)
    inv2 = 1.0 / (1.0 + ex)

    oh0, oh1 = row == i0, row == i1
    sel = (oh0 | oh1).astype(F32)
    tok_r = lax.broadcasted_iota(jnp.int32, (ROW_TILE, ROW_TILE), 0)
    tok_c = lax.broadcasted_iota(jnp.int32, (ROW_TILE, ROW_TILE), 1)
    earlier = (tok_r < tok_c).astype(F32).astype(BF16)
    carry = carry_sc[:, 0:1]
    before = jnp.dot(sel.astype(BF16), earlier, preferred_element_type=F32) + carry
    rank0 = jnp.sum(jnp.where(oh0, before, 0.0), axis=0, keepdims=True)
    rank1 = jnp.sum(jnp.where(oh1, before, 0.0), axis=0, keepdims=True)
    count = jnp.broadcast_to(carry + jnp.sum(sel, axis=1, keepdims=True), carry_sc.shape)
    carry_sc[...] = count
    cnt_ref[...] = count[:SUBLANES]

    fields = (i0.astype(F32), i1.astype(F32), inv2, ex * inv2, rank0, rank1)
    meta_t = jnp.zeros((LANES, ROW_TILE), F32)
    frow = lax.broadcasted_iota(jnp.int32, meta_t.shape, 0)
    for k, f in enumerate(fields):
        meta_t = jnp.where(frow == k, f, meta_t)
    metat_ref[...] = meta_t[:SUBLANES]
    meta_ref[...] = jnp.transpose(meta_t)


META_E0, META_E1, META_G0, META_G1, META_RANK0, META_RANK1 = range(6)
ROUTER_ROWS = 16


def _dilout(os, ls, tok, kvm, wout, h, g_ffn, router):
    row = lambda i: (i, 0)
    const = lambda i: (0, 0)

    def group_spec(a):
        d = a.shape[0]
        if d == 1:
            return pl.BlockSpec((None, ROW_TILE, DIL_W), lambda i: (0, i, 0))
        return pl.BlockSpec((d, ROW_TILE // d, DIL_W), lambda i: (0, i, 0))

    return pl.pallas_call(
        _dilout_kernel,
        grid=(SEQ // ROW_TILE,),
        in_specs=[group_spec(a) for a in (*os, *ls)] + [
                  pl.BlockSpec((ROW_TILE, MEM_W), lambda i: (i, DIL_SLAB // MEM_W)),
                  pl.BlockSpec((N_MEM, 2 * MEM_W), lambda i: (0, 1)),
                  pl.BlockSpec((DIL_W + MEM_W, D_MODEL), const),
                  pl.BlockSpec((ROW_TILE, D_MODEL), row),
                  pl.BlockSpec((1, D_MODEL), const),
                  pl.BlockSpec((ROUTER_ROWS, D_MODEL), const)],
        out_specs=[pl.BlockSpec((ROW_TILE, D_MODEL), row),
                   pl.BlockSpec((ROW_TILE, D_MODEL), row),
                   pl.BlockSpec((ROW_TILE, LANES), row),
                   pl.BlockSpec((SUBLANES, LANES), const),
                   pl.BlockSpec((SUBLANES, ROW_TILE), lambda i: (0, i))],
        out_shape=[jax.ShapeDtypeStruct((SEQ, D_MODEL), F32), jax.ShapeDtypeStruct((SEQ, D_MODEL), BF16),
                   jax.ShapeDtypeStruct((SEQ, LANES), F32), jax.ShapeDtypeStruct((SUBLANES, LANES), F32),
                   jax.ShapeDtypeStruct((SUBLANES, SEQ), F32)],
        scratch_shapes=[pltpu.VMEM((ROUTER_ROWS, LANES), F32),
                        pltpu.VMEM((4, DIL_W // LANES, ROW_TILE, LANES), F32)],
        compiler_params=pltpu.CompilerParams(dimension_semantics=("arbitrary",), vmem_limit_bytes=VMEM_LIMIT),
        name="dil_out",
    )(*os, *ls, tok, kvm, wout, h, g_ffn.reshape(1, D_MODEL), router)


def _swiglu_partial(x, wg, wu, wd):
    a = jnp.dot(x, wg, preferred_element_type=F32)
    b = jnp.dot(x, wu, preferred_element_type=F32)
    hm = (a * jax.nn.sigmoid(a) * b).astype(BF16)
    return jnp.dot(hm, wd, preferred_element_type=F32)


def _ffn_kernel(x_ref, wg_ref, wu_ref, wd_ref, h_ref, wa_ref, o_ref, wa_out, zero_out):
    wa_out[...] = wa_ref[...].astype(BF16)
    zero_out[...] = jnp.zeros(zero_out.shape, zero_out.dtype)
    x = x_ref[...]
    acc = h_ref[...]
    for c in range(D_FF // FFN_COL_TILE):
        cols = slice(c * FFN_COL_TILE, (c + 1) * FFN_COL_TILE)
        acc = acc + _swiglu_partial(x, wg_ref[:, cols], wu_ref[:, cols], wd_ref[cols, :])
    o_ref[...] = acc


def _ffn(x, h, wg, wu, wd, wa):
    tm = ROW_TILE
    steps = SEQ // tm
    resident = lambda shape: pl.BlockSpec(shape, lambda i: (0, 0), pipeline_mode=pl.Buffered(1))
    return pl.pallas_call(
        _ffn_kernel,
        grid=(steps,),
        in_specs=[pl.BlockSpec((tm, D_MODEL), lambda i: (i, 0)),
                  resident((D_MODEL, D_FF)), resident((D_MODEL, D_FF)), resident((D_FF, D_MODEL)),
                  pl.BlockSpec((tm, D_MODEL), lambda i: (i, 0)),
                  _cast_slice_spec(wa, steps)],
        out_specs=[pl.BlockSpec((tm, D_MODEL), lambda i: (i, 0)), _cast_slice_spec(wa, steps),
                   pl.BlockSpec((MOE_SLOTS // steps, PACK_W), lambda i: (i, 0))],
        out_shape=[jax.ShapeDtypeStruct((SEQ, D_MODEL), F32), jax.ShapeDtypeStruct(wa.shape, BF16),
                   jax.ShapeDtypeStruct((MOE_SLOTS, PACK_W), jnp.uint32)],
        compiler_params=pltpu.CompilerParams(dimension_semantics=("arbitrary",), vmem_limit_bytes=VMEM_LIMIT),
        name="ffn",
    )(x, wg, wu, wd, h, wa)


MOE_SUB = 512
MOE_ROWS = 2048
MOE_COL_TILE = D_FF // 2
MOE_SUBS = MOE_ROWS // MOE_SUB
MOE_SLOTS = 2 * SEQ + N_EXPERTS * MOE_SUB
MOE_TILES = MOE_SLOTS // MOE_ROWS
MOE_ITEMS = MOE_TILES + N_EXPERTS - 1
PACK_W = D_MODEL // 2
ITEM_TILE, ITEM_EXPERT, ITEM_LO, ITEM_HI, ITEM_FIRST = range(5)


def _pack_bf16_pairs(x):
    lo = lax.bitcast_convert_type(x[:, :PACK_W].astype(F32), jnp.uint32) >> 16
    hi = lax.bitcast_convert_type(x[:, PACK_W:].astype(F32), jnp.uint32) & jnp.uint32(0xFFFF0000)
    return hi | lo


def _unpack_bf16_pairs(pk):
    lo = lax.bitcast_convert_type(pk << 16, F32).astype(BF16)
    hi = lax.bitcast_convert_type(pk & jnp.uint32(0xFFFF0000), F32).astype(BF16)
    return lo, hi


def _vmem_row(ref, group, sub):
    return ref.at[group, pl.ds(sub, 1), :]


def _hbm_row(ref, row):
    return ref.at[pl.ds(row, 1), :]


def _scatter_kernel(pos0_ref, pos1_ref, x_ref, xs_in_ref, xs_ref, pk_sc, sem):
    del xs_in_ref
    pk_sc[...] = _pack_bf16_pairs(x_ref[...]).reshape(pk_sc.shape)

    def issue(g, c):
        for u in range(SUBLANES):
            t = g * SUBLANES + u
            pltpu.make_async_copy(_vmem_row(pk_sc, g, u), _hbm_row(xs_ref, pos0_ref[t]), sem).start(priority=0)
            pltpu.make_async_copy(_vmem_row(pk_sc, g, u), _hbm_row(xs_ref, pos1_ref[t]), sem).start(priority=1)
        return c

    def drain(g, c):
        for _ in range(2 * SUBLANES):
            pltpu.make_async_copy(_vmem_row(pk_sc, 0, 0), _hbm_row(xs_ref, 0), sem).wait()
        return c

    lax.fori_loop(0, ROW_TILE // SUBLANES, issue, 0)
    lax.fori_loop(0, ROW_TILE // SUBLANES, drain, 0)


def _moe_scatter(x, pos0, pos1, zero_slots):
    smem_tile = pl.BlockSpec((ROW_TILE,), lambda i: (i,), memory_space=pltpu.SMEM)
    return pl.pallas_call(
        _scatter_kernel,
        grid=(SEQ // ROW_TILE,),
        in_specs=[smem_tile, smem_tile,
                  pl.BlockSpec((ROW_TILE, D_MODEL), lambda i: (i, 0)),
                  pl.BlockSpec(memory_space=pl.ANY)],
        out_specs=pl.BlockSpec(memory_space=pl.ANY),
        out_shape=jax.ShapeDtypeStruct((MOE_SLOTS, PACK_W), jnp.uint32),
        scratch_shapes=[pltpu.VMEM((ROW_TILE // SUBLANES, SUBLANES, PACK_W), jnp.uint32),
                        pltpu.SemaphoreType.DMA],
        input_output_aliases={3: 0},
        compiler_params=pltpu.CompilerParams(dimension_semantics=("arbitrary",), vmem_limit_bytes=VMEM_LIMIT),
        name="moe_scatter",
    )(pos0, pos1, x, zero_slots)


def _moe_kernel(items_ref, xs_ref, wg_ref, wu_ref, wd_ref, y_ref, xb_sc):
    w = pl.program_id(0)
    j = pl.program_id(1)
    lo = items_ref[ITEM_LO, w]
    hi = items_ref[ITEM_HI, w]

    @pl.when((j == 0) & (hi > lo))
    def _():
        xlo, xhi = _unpack_bf16_pairs(xs_ref[...])
        xb_sc[:, :PACK_W] = xlo
        xb_sc[:, PACK_W:] = xhi

    @pl.when((j == 0) & (items_ref[ITEM_FIRST, w] == 1))
    def _():
        y_ref[...] = jnp.zeros_like(y_ref)

    for sb in range(MOE_SUBS):
        @pl.when((lo <= sb) & (sb < hi))
        def _(sb=sb):
            rows = slice(sb * MOE_SUB, (sb + 1) * MOE_SUB)
            x = xb_sc[rows, :]
            acc = y_ref[rows, :]
            for c0 in range(0, MOE_COL_TILE, FFN_COL_TILE):
                cols = slice(c0, min(c0 + FFN_COL_TILE, MOE_COL_TILE))
                acc = acc + _swiglu_partial(x, wg_ref[0, :, cols], wu_ref[0, :, cols], wd_ref[0, cols, :])
            y_ref[rows, :] = acc


def _moe_grouped(items, xs, wg, wu, wd):
    tf = MOE_COL_TILE
    nf = D_FF // tf
    jf = lambda w, j, it: jnp.where(it[ITEM_HI, w] > it[ITEM_LO, w], j, nf - 1)
    grid_spec = pltpu.PrefetchScalarGridSpec(
        num_scalar_prefetch=1,
        grid=(MOE_ITEMS, nf),
        in_specs=[pl.BlockSpec((MOE_ROWS, PACK_W), lambda w, j, it: (it[ITEM_TILE, w], 0)),
                  pl.BlockSpec((1, D_MODEL, tf), lambda w, j, it: (it[ITEM_EXPERT, w], 0, jf(w, j, it))),
                  pl.BlockSpec((1, D_MODEL, tf), lambda w, j, it: (it[ITEM_EXPERT, w], 0, jf(w, j, it))),
                  pl.BlockSpec((1, tf, D_MODEL), lambda w, j, it: (it[ITEM_EXPERT, w], jf(w, j, it), 0))],
        out_specs=pl.BlockSpec((MOE_ROWS, D_MODEL), lambda w, j, it: (it[ITEM_TILE, w], 0)),
        scratch_shapes=[pltpu.VMEM((MOE_ROWS, D_MODEL), BF16)],
    )
    return pl.pallas_call(
        _moe_kernel,
        grid_spec=grid_spec,
        out_shape=jax.ShapeDtypeStruct((MOE_SLOTS, D_MODEL), F32),
        compiler_params=pltpu.CompilerParams(dimension_semantics=("arbitrary", "arbitrary"),
                                             vmem_limit_bytes=VMEM_LIMIT),
        name="moe_grouped",
    )(items, xs, wg, wu, wd)


def _combine_kernel(pos0_ref, pos1_ref, meta_ref, h_ref, gf_ref, y_ref, o_ref, y0_sc, y1_sc, sem):
    def issue(g, c):
        for u in range(SUBLANES):
            t = g * SUBLANES + u
            pltpu.make_async_copy(_hbm_row(y_ref, pos0_ref[t]), _vmem_row(y0_sc, g, u), sem).start(priority=0)
            pltpu.make_async_copy(_hbm_row(y_ref, pos1_ref[t]), _vmem_row(y1_sc, g, u), sem).start(priority=1)
        return c

    def drain(g, c):
        for _ in range(SUBLANES):
            pltpu.make_async_copy(_hbm_row(y_ref, 0), _vmem_row(y0_sc, 0, 0), sem).wait()
            pltpu.make_async_copy(_hbm_row(y_ref, 0), _vmem_row(y1_sc, 0, 0), sem).wait()
        return c

    lax.fori_loop(0, ROW_TILE // SUBLANES, issue, 0)
    lax.fori_loop(0, ROW_TILE // SUBLANES, drain, 0)
    meta = meta_ref[...]
    g0 = meta[:, META_G0:META_G0 + 1]
    g1 = meta[:, META_G1:META_G1 + 1]
    y0 = y0_sc[...].reshape(ROW_TILE, D_MODEL)
    y1 = y1_sc[...].reshape(ROW_TILE, D_MODEL)
    o_ref[...] = _rms(h_ref[...] + (g0 * y0 + g1 * y1), gf_ref[...])


def _moe_combine(pos0, pos1, meta, h, g_final, y):
    smem_tile = pl.BlockSpec((ROW_TILE,), lambda i: (i,), memory_space=pltpu.SMEM)
    tiles = (ROW_TILE // SUBLANES, SUBLANES, D_MODEL)
    return pl.pallas_call(
        _combine_kernel,
        grid=(SEQ // ROW_TILE,),
        in_specs=[smem_tile, smem_tile,
                  pl.BlockSpec((ROW_TILE, LANES), lambda i: (i, 0)),
                  pl.BlockSpec((ROW_TILE, D_MODEL), lambda i: (i, 0)),
                  pl.BlockSpec((1, D_MODEL), lambda i: (0, 0)),
                  pl.BlockSpec(memory_space=pl.ANY)],
        out_specs=pl.BlockSpec((ROW_TILE, D_MODEL), lambda i: (i, 0)),
        out_shape=jax.ShapeDtypeStruct((SEQ, D_MODEL), F32),
        scratch_shapes=[pltpu.VMEM(tiles, F32), pltpu.VMEM(tiles, F32), pltpu.SemaphoreType.DMA],
        compiler_params=pltpu.CompilerParams(dimension_semantics=("arbitrary",), vmem_limit_bytes=VMEM_LIMIT),
        name="moe_combine",
    )(pos0, pos1, meta, h, g_final.reshape(1, D_MODEL), y)


def _moe_plan(meta_t, counts):
    i32 = jnp.int32
    cnt = counts[:N_EXPERTS, 0].astype(i32)
    padded = (cnt + MOE_SUB - 1) // MOE_SUB * MOE_SUB
    start = jnp.cumsum(padded) - padded

    def slot(e_row, rank_row):
        e = meta_t[e_row].astype(i32)
        base = sum(jnp.where(e == k, start[k], 0) for k in range(N_EXPERTS))
        return base + meta_t[rank_row].astype(i32)

    pos0, pos1 = slot(META_E0, META_RANK0), slot(META_E1, META_RANK1)

    first_sb, end_sb = start // MOE_SUB, (start + padded) // MOE_SUB
    t0 = jnp.arange(MOE_TILES, dtype=i32)[:, None] * MOE_SUBS
    lo = jnp.clip(first_sb[None, :] - t0, 0, MOE_SUBS)
    hi = jnp.clip(end_sb[None, :] - t0, 0, MOE_SUBS)
    active = hi > lo
    none = ~jnp.any(active, axis=1, keepdims=True)
    active = jnp.concatenate([active, none], axis=1)
    lo = jnp.concatenate([lo, jnp.zeros_like(t0)], axis=1)
    hi = jnp.concatenate([hi, jnp.zeros_like(t0)], axis=1)
    first = active & (jnp.cumsum(active.astype(i32), axis=1) == 1)
    ncol = N_EXPERTS + 1
    flat = jnp.arange(MOE_TILES * ncol, dtype=i32)
    key = jnp.where(active.reshape(-1), flat, MOE_TILES * ncol)
    order = jnp.argsort(key)[:MOE_ITEMS]
    live = key[order] < MOE_TILES * ncol
    tile = jnp.where(live, order // ncol, MOE_TILES - 1)
    lo_i = jnp.where(live, lo.reshape(-1)[order], 0)
    hi_i = jnp.where(live, hi.reshape(-1)[order], 0)
    first_i = jnp.where(live, first.reshape(-1)[order], False).astype(i32)
    idx = jnp.arange(MOE_ITEMS, dtype=i32)
    src = jnp.maximum(lax.cummax(jnp.where(hi_i > lo_i, idx, -1), axis=0), 0)
    expert = jnp.minimum(order % ncol, N_EXPERTS - 1)[src]
    items = jnp.stack([tile, expert, lo_i, hi_i, first_i]).astype(i32)
    return pos0, pos1, items


def _take_runs(w, idx, axis):
    idx = list(idx)
    runs, start = [], 0
    for i in range(1, len(idx) + 1):
        if i == len(idx) or idx[i] != idx[i - 1] + 1:
            runs.append(lax.slice_in_dim(w, idx[start], idx[i - 1] + 1, axis=axis))
            start = i
    return jnp.concatenate(runs, axis=axis)


def _swa_weight_layout():
    head = lambda base, j: list(range(base + j * HEAD_DIM, base + (j + 1) * HEAD_DIM))
    k0, v0, m0 = SWA_Q, SWA_Q + SWA_KV, SWA_Q + 2 * SWA_KV
    cols, rows = [], []
    for a, b in SWA_PAIR_SLABS:
        cols += head(0, a) + head(0, b)
        rows += head(0, a) + head(0, b)
    for base in (k0, v0):
        cols += head(base, 0) + head(base, 1) + head(base, 2) + head(base, 2)
    cols += list(range(m0, m0 + MEM_W))
    rows += list(range(SWA_Q, SWA_Q + MEM_W))
    scale = np.ones((len(cols),), np.float32)
    scale[:SWA_Q] = Q_SCALE
    scale[SWA_M0:] = Q_SCALE
    assert len(cols) == SWA_PW and len(rows) == D_MODEL
    return np.asarray(cols), scale, np.asarray(rows)


def _dil_weight_layout():
    n_grp = len(DIL_GROUPS) * DIL_SLAB
    cols = list(range(DIL_SLAB)) + list(range(n_grp, n_grp + MEM_W)) + list(range(DIL_SLAB, n_grp))
    scale = np.ones((len(cols),), np.float32)
    for c0 in (0, DIL_SLAB, TOK_W, TOK_W + DIL_SLAB):
        scale[c0:c0 + DIL_W] = Q_SCALE
    return np.asarray(cols), scale


def kernel(x, mem, rel_bias_table, mem_norm, norm_mix, norm_ffn, final_norm, swa_w_in, swa_sinks, swa_w_mem_kv,
           swa_w_out, dil_w_in, dil_w_mem_kv, dil_w_out, ffn_gate, ffn_up, ffn_down, router, moe_gate, moe_up,
           moe_down):
    assert x.shape == (1, SEQ, D_MODEL) and mem.shape == (1, N_MEM, D_MODEL)
    assert norm_mix.shape == (2, D_MODEL) and swa_w_in.shape == (1, D_MODEL, SWA_IN)
    assert dil_w_in.shape == (1, D_MODEL, DIL_IN) and moe_gate.shape == (1, N_EXPERTS, D_MODEL, D_FF)
    bf = lambda a: a.astype(BF16)
    h0 = x.reshape(SEQ, D_MODEL)

    kvm = _memkv(mem[0], mem_norm, bf(jnp.concatenate([swa_w_mem_kv[0], dil_w_mem_kv[0]], axis=1)))

    cols0, scale0, rows0 = _swa_weight_layout()
    h1, hn1, moe_up_b, moe_down_b, ffn_gate_b, ffn_up_b, ffn_down_b = _swa_layer(
        bf(_take_runs(swa_w_in[0], cols0, 1) * scale0), norm_mix[0], kvm, rel_bias_table, swa_sinks[0],
        bf(_take_runs(swa_w_out[0], rows0, 0)), h0, norm_ffn[0],
        [moe_up[0], moe_down[0], ffn_gate, ffn_up, ffn_down])
    h2, moe_gate_b, xs_zero = _ffn(hn1, h1, ffn_gate_b[0], ffn_up_b[0], ffn_down_b[0], moe_gate[0])

    cols1, scale1 = _dil_weight_layout()
    tok, g1, g2 = _dilproj(h2, norm_mix[1], bf(_take_runs(dil_w_in[0], cols1, 1) * scale1))
    groups = _dil_attention((tok[None], g1, g2), rel_bias_table)
    router_p = jnp.pad(bf(router[0].T), ((0, ROUTER_ROWS - N_EXPERTS), (0, 0)))
    h3, hn3, meta, counts, meta_t = _dilout([g[0] for g in groups], [g[1] for g in groups], tok, kvm,
                                            bf(dil_w_out[0]), h2, norm_ffn[1], router_p)
    pos0, pos1, items = _moe_plan(meta_t, counts)
    xs = _moe_scatter(hn3, pos0, pos1, xs_zero)
    y = _moe_grouped(items, xs, moe_gate_b, moe_up_b, moe_down_b)
    out = _moe_combine(pos0, pos1, meta, h3, final_norm, y)
    return out.reshape(1, SEQ, D_MODEL)
```

```python
import functools
import math

import jax
import jax.numpy as jnp
import numpy as np
from jax import lax
from jax.experimental import pallas as pl
from jax.experimental.pallas import tpu as pltpu

F32 = jnp.float32
BF16 = jnp.bfloat16

D_MODEL = 1024
SEQ = 16384
HEAD_DIM = 64
N_MIX_HEADS = 12
SWA_KV_HEADS = 3
SWA_GROUP = N_MIX_HEADS // SWA_KV_HEADS
SWA_WINDOW = 128
DIL_GROUPS = ((128, 1), (512, 4), (2048, 16))
DIL_HEADS = 4
N_MEM = 256
MEM_HEADS = 4
BLOCK = 128
N_BUCKETS = 32
MAX_DISTANCE = 2048
D_FF = 3584
N_EXPERTS = 8
EPS = 1e-5
NEG = -1e30
Q_SCALE = HEAD_DIM ** -0.5

SWA_Q = N_MIX_HEADS * HEAD_DIM
SWA_KV = SWA_KV_HEADS * HEAD_DIM
SWA_IN = SWA_Q + 2 * SWA_KV + MEM_HEADS * HEAD_DIM
DIL_W = DIL_HEADS * HEAD_DIM
DIL_IN = len(DIL_GROUPS) * 3 * DIL_W + MEM_HEADS * HEAD_DIM
MEM_W = MEM_HEADS * HEAD_DIM

LANES = 128
SUBLANES = 8
VMEM_LIMIT = 56 * 1024 * 1024

ROW_TILE = 512
BLOCKS_PER_TILE = ROW_TILE // BLOCK
FFN_COL_TILE = 256


def _bucket_map(dil):
    qi = np.arange(BLOCK)[:, None]
    kj = np.arange(2 * BLOCK)[None, :]
    d = np.maximum((qi + BLOCK - kj) * dil, 0)
    max_exact = N_BUCKETS // 2
    ratio = np.maximum(d, 1).astype(np.float32) / np.float32(max_exact)
    large = max_exact + (np.log(ratio) / np.float32(math.log(MAX_DISTANCE / max_exact))
                         * np.float32(N_BUCKETS - max_exact)).astype(np.int32)
    return np.where(d < max_exact, d, np.minimum(large, N_BUCKETS - 1)).astype(np.int32)


def _rms(x, g):
    ms = jnp.mean(x * x, axis=-1, keepdims=True)
    return x * lax.rsqrt(ms + EPS) * g


def _dot_nt(a, b):
    return lax.dot_general(a, b, (((1,), (1,)), ((), ())), preferred_element_type=F32)


def _dot_tn(a, b):
    return lax.dot_general(a, b, (((0,), (0,)), ((), ())), preferred_element_type=F32)


def _fill_bias(bias_sc, tab_ref, bucket, heads):
    for j, h in enumerate(heads):
        def body(k, b, h=h):
            return jnp.where(bucket == k, tab_ref[k, h], b)
        bias_sc[:, j * BLOCK:(j + 1) * BLOCK] = lax.fori_loop(0, N_BUCKETS, body, jnp.zeros(bucket.shape, F32))


def _fill_row(row_sc, values):
    blk = lax.broadcasted_iota(jnp.int32, row_sc.shape, 1) // BLOCK
    row = jnp.zeros(row_sc.shape, F32)
    for j, v in enumerate(values):
        row = jnp.where(blk == j, v, row)
    row_sc[...] = row


def _folded_bucket_map(dil):
    full = _bucket_map(dil)
    qi = np.arange(BLOCK)[:, None]
    c = np.arange(BLOCK)[None, :]
    return np.ascontiguousarray(np.where(c <= qi, full[:, BLOCK:], full[:, :BLOCK]).T.astype(np.int32))


def _diag_bucket(dil):
    return int(_bucket_map(dil)[0, 0])


def _fold_masks(n):
    c = lax.broadcasted_iota(jnp.int32, (BLOCK, n), 0)
    qi = lax.broadcasted_iota(jnp.int32, (BLOCK, n), 1) % BLOCK
    return c <= qi, c == qi


def _pair_scores(k_slab, q_rows):
    lane = lax.broadcasted_iota(jnp.int32, q_rows.shape, 1)
    zero = jnp.zeros_like(q_rows)
    qa = jnp.where(lane < HEAD_DIM, q_rows, zero)
    qb = jnp.where(lane < HEAD_DIM, zero, q_rows)
    return _dot_nt(k_slab, qa), _dot_nt(k_slab, qb)


def _pair_values(v_slab, p_a, p_b):
    n = p_a.shape[1]
    o = _dot_tn(v_slab, jnp.concatenate([p_a, p_b], axis=1))
    row = lax.broadcasted_iota(jnp.int32, (LANES, n), 0)
    return jnp.where(row < HEAD_DIM, o[:, :n], o[:, n:])


def _band_softmax(st, bias, own, eye, has_prev, sink, diag_bias):
    s_prev, s_own = st[:BLOCK], st[BLOCK:]
    t = jnp.where(own, s_own, s_prev) + bias
    if has_prev is not None:
        t = jnp.where(own | has_prev, t, NEG)
    m = jnp.max(t, axis=0, keepdims=True)
    if diag_bias is not None:
        s_d = jnp.sum(jnp.where(eye, s_prev, 0.0), axis=0, keepdims=True) + diag_bias
        if has_prev is not None:
            s_d = jnp.where(has_prev, s_d, NEG)
        m = jnp.maximum(m, s_d)
    if sink is not None:
        m = jnp.maximum(m, sink)
    e = jnp.exp(t - m)
    den = jnp.sum(e, axis=0, keepdims=True)
    if diag_bias is not None:
        e_d = jnp.exp(s_d - m)
        den = den + e_d
    if sink is not None:
        den = den + jnp.exp(sink - m)
    inv = 1.0 / den
    p = e * inv
    p_own = jnp.where(own, p, 0.0)
    p_prev = jnp.where(own, 0.0, p)
    if diag_bias is not None:
        p_prev = jnp.where(eye, e_d * inv, p_prev)
    return jnp.concatenate([p_prev, p_own], axis=0).astype(BF16), m + jnp.log(den)


def _mem_attention(qm, kvm):
    outs = []
    for s in range(MEM_W // LANES):
        sa, sb = _pair_scores(kvm[:, s * LANES:(s + 1) * LANES], qm[:, s * LANES:(s + 1) * LANES])
        ps = []
        for st in (sa, sb):
            e = jnp.exp(st - jnp.max(st, axis=0, keepdims=True))
            ps.append((e * (1.0 / jnp.sum(e, axis=0, keepdims=True))).astype(BF16))
        outs.append(_pair_values(kvm[:, MEM_W + s * LANES:MEM_W + (s + 1) * LANES], ps[0], ps[1]))
    return outs


def _to_rows(slabs_t):
    return jnp.concatenate([jnp.transpose(x).astype(BF16) for x in slabs_t], axis=-1)


def _memkv_kernel(mem_ref, g_ref, w_ref, o_ref):
    mn = _rms(mem_ref[...], g_ref[...]).astype(BF16)
    o_ref[...] = jnp.dot(mn, w_ref[...], preferred_element_type=F32).astype(BF16)


def _memkv(mem, g, w):
    return pl.pallas_call(
        _memkv_kernel,
        out_shape=jax.ShapeDtypeStruct((N_MEM, w.shape[1]), BF16),
        name="mem_kv",
    )(mem, g.reshape(1, D_MODEL), w)


SWA_PAIR_SLABS = [(g, SWA_GROUP + g) for g in range(SWA_GROUP)] + [(8, 9), (10, 11)]
SWA_K0 = SWA_Q
SWA_V0 = SWA_K0 + 2 * LANES
SWA_M0 = SWA_V0 + 2 * LANES
SWA_PW = SWA_M0 + MEM_W
SWA_SCORE_HEADS = [0, 1, 2, 3, 4, 5, 6, 7, 8, 10, 9, 11]
SWA_NQ = N_MIX_HEADS * BLOCK


def _cast_slice_spec(w, steps):
    e, r, c = w.shape
    per_e = steps // e
    assert steps % e == 0 and r % per_e == 0 and (r // per_e) % 16 == 0
    return pl.BlockSpec((1, r // per_e, c), lambda i: (i // per_e, i % per_e, 0))


def _swa_kernel(*refs, n_cast):
    (tab_ref, sink_ref, bucket_ref, gmix_ref, win_ref, kvm_ref, wout_ref, h_ref, g_ref), refs = refs[:9], refs[9:]
    cast_in, (h1_ref, hn1_ref), refs = refs[:n_cast], refs[n_cast:n_cast + 2], refs[n_cast + 2:]
    cast_out, (bias_sc, sink_sc, k_sc, v_sc, cat_sc, p_ref, kprev_sc, vprev_sc) = refs[:n_cast], refs[n_cast:]
    i = pl.program_id(0)
    for src, dst in zip(cast_in, cast_out):
        dst[...] = src[...].astype(BF16)

    @pl.when(i == 0)
    def _():
        _fill_bias(bias_sc, tab_ref, bucket_ref[...], SWA_SCORE_HEADS)
        _fill_row(sink_sc, [sink_ref[h] for h in SWA_SCORE_HEADS])

        kprev_sc[...] = jnp.zeros(kprev_sc.shape, BF16)
        vprev_sc[...] = jnp.zeros(vprev_sc.shape, BF16)

    k_sc[0:BLOCK, :] = kprev_sc[...]
    v_sc[0:BLOCK, :] = vprev_sc[...]

    hn = _rms(h_ref[...], gmix_ref[...]).astype(BF16)
    p_ref[...] = jnp.dot(hn, win_ref[...], preferred_element_type=F32).astype(BF16)

    k_sc[BLOCK:, :] = p_ref[:, SWA_K0:SWA_V0]
    v_sc[BLOCK:, :] = p_ref[:, SWA_V0:SWA_M0]
    own, eye = _fold_masks(SWA_NQ)

    def block_body(b, carry):
        r0 = b * BLOCK
        has_prev = (i > 0) if b == 0 else None
        qb = p_ref[pl.ds(r0, BLOCK), 0:SWA_Q]
        slab = lambda x, s: x[:, s * LANES:(s + 1) * LANES]
        qa = jnp.concatenate([slab(qb, s) for s in range(4)], axis=0)
        qc = jnp.concatenate([slab(qb, s) for s in (4, 5)], axis=0)
        kb = k_sc[pl.ds(r0, 2 * BLOCK), :]
        vb = v_sc[pl.ds(r0, 2 * BLOCK), :]
        s0, s1 = _pair_scores(slab(kb, 0), qa)
        s2, s3 = _pair_scores(slab(kb, 1), qc)
        st = jnp.concatenate([s0, s1, s2, s3], axis=1)
        pt, _ = _band_softmax(st, bias_sc[...], own, eye, has_prev, sink_sc[...], None)
        na, nc = 4 * BLOCK, 2 * BLOCK
        oa = _pair_values(slab(vb, 0), pt[:, 0:na], pt[:, na:2 * na])
        oc = _pair_values(slab(vb, 1), pt[:, 2 * na:2 * na + nc], pt[:, 2 * na + nc:])
        outs = [slab(oa, s) for s in range(4)] + [slab(oc, s) for s in range(2)]
        outs += _mem_attention(p_ref[pl.ds(r0, BLOCK), SWA_M0:], kvm_ref[...])
        cat_sc[pl.ds(r0, BLOCK), :] = _to_rows(outs)
        return carry

    for b in range(BLOCKS_PER_TILE):
        block_body(b, None)

    kprev_sc[...] = p_ref[ROW_TILE - BLOCK:, SWA_K0:SWA_V0]
    vprev_sc[...] = p_ref[ROW_TILE - BLOCK:, SWA_V0:SWA_M0]

    out = h_ref[...] + jnp.dot(cat_sc[...], wout_ref[...], preferred_element_type=F32)
    h1_ref[...] = out
    hn1_ref[...] = _rms(out, g_ref[...]).astype(BF16)


def _swa_layer(w_in, g_mix, kvm, table, sinks, wout, h, g_ffn, casts):
    assert SWA_WINDOW == BLOCK
    kv_w = 2 * LANES
    steps = SEQ // ROW_TILE
    cast_specs = [_cast_slice_spec(w, steps) for w in casts]
    return pl.pallas_call(
        functools.partial(_swa_kernel, n_cast=len(casts)),
        grid=(SEQ // ROW_TILE,),
        in_specs=[
            pl.BlockSpec(memory_space=pltpu.SMEM),
            pl.BlockSpec(memory_space=pltpu.SMEM),
            pl.BlockSpec((BLOCK, BLOCK), lambda i: (0, 0)),
            pl.BlockSpec((1, D_MODEL), lambda i: (0, 0)),
            pl.BlockSpec((D_MODEL, SWA_PW), lambda i: (0, 0)),
            pl.BlockSpec((N_MEM, 2 * MEM_W), lambda i: (0, 0)),
            pl.BlockSpec((D_MODEL, D_MODEL), lambda i: (0, 0)),
            pl.BlockSpec((ROW_TILE, D_MODEL), lambda i: (i, 0)),
            pl.BlockSpec((1, D_MODEL), lambda i: (0, 0)),
        ] + cast_specs,
        out_specs=[pl.BlockSpec((ROW_TILE, D_MODEL), lambda i: (i, 0)),
                   pl.BlockSpec((ROW_TILE, D_MODEL), lambda i: (i, 0))] + cast_specs,
        out_shape=[jax.ShapeDtypeStruct((SEQ, D_MODEL), F32), jax.ShapeDtypeStruct((SEQ, D_MODEL), BF16)]
                  + [jax.ShapeDtypeStruct(w.shape, BF16) for w in casts],
        scratch_shapes=[pltpu.VMEM((BLOCK, SWA_NQ), F32),
                        pltpu.VMEM((1, SWA_NQ), F32),
                        pltpu.VMEM((ROW_TILE + BLOCK, kv_w), BF16),
                        pltpu.VMEM((ROW_TILE + BLOCK, kv_w), BF16),
                        pltpu.VMEM((ROW_TILE, D_MODEL), BF16),
                        pltpu.VMEM((ROW_TILE, SWA_PW), BF16),
                        pltpu.VMEM((BLOCK, kv_w), BF16),
                        pltpu.VMEM((BLOCK, kv_w), BF16)],
        compiler_params=pltpu.CompilerParams(dimension_semantics=("arbitrary",), vmem_limit_bytes=VMEM_LIMIT),
        name="swa_mixer",
    )(table, sinks, jnp.asarray(_folded_bucket_map(1)), g_mix.reshape(1, D_MODEL), w_in, kvm, wout, h,
      g_ffn.reshape(1, D_MODEL), *casts)


DIL_SLAB = 3 * DIL_W
TOK_W = DIL_SLAB + MEM_W
SLABS_PER_GROUP = DIL_SLAB // LANES


def _dilproj_kernel(h_ref, g_ref, w_ref, tok_ref, g1_ref, g2_ref, slab_sc, quarter_sc):
    hn = _rms(h_ref[...], g_ref[...]).astype(BF16)
    res = jnp.dot(hn, w_ref[...], preferred_element_type=F32)
    tok_ref[...] = res[:, :TOK_W].astype(BF16)
    for s in range(2 * SLABS_PER_GROUP):
        slab_sc[s] = res[:, TOK_W + s * LANES:TOK_W + (s + 1) * LANES]
    d1, d2 = DIL_GROUPS[1][1], DIL_GROUPS[2][1]
    assert d2 == d1 * d1
    for s in range(SLABS_PER_GROUP):
        cols = slice(s * LANES, (s + 1) * LANES)
        for r in range(d1):
            g1_ref[r, :, cols] = slab_sc[s, pl.ds(r, ROW_TILE // d1, stride=d1), :].astype(BF16)
        for q in range(d1):
            quarter_sc[s, q] = slab_sc[SLABS_PER_GROUP + s, pl.ds(q, ROW_TILE // d1, stride=d1), :]
        for q in range(d1):
            for a in range(d1):
                rows = quarter_sc[s, q, pl.ds(a, ROW_TILE // d2, stride=d1), :]
                g2_ref[d1 * a + q, :, cols] = rows.astype(BF16)


def _dilproj(h, g, w):
    d1, d2 = DIL_GROUPS[1][1], DIL_GROUPS[2][1]
    return pl.pallas_call(
        _dilproj_kernel,
        grid=(SEQ // ROW_TILE,),
        in_specs=[pl.BlockSpec((ROW_TILE, D_MODEL), lambda i: (i, 0)),
                  pl.BlockSpec((1, D_MODEL), lambda i: (0, 0)),
                  pl.BlockSpec((D_MODEL, DIL_IN), lambda i: (0, 0))],
        out_specs=[pl.BlockSpec((ROW_TILE, TOK_W), lambda i: (i, 0)),
                   pl.BlockSpec((d1, ROW_TILE // d1, DIL_SLAB), lambda i: (0, i, 0)),
                   pl.BlockSpec((d2, ROW_TILE // d2, DIL_SLAB), lambda i: (0, i, 0))],
        out_shape=[jax.ShapeDtypeStruct((SEQ, TOK_W), BF16),
                   jax.ShapeDtypeStruct((d1, SEQ // d1, DIL_SLAB), BF16),
                   jax.ShapeDtypeStruct((d2, SEQ // d2, DIL_SLAB), BF16)],
        scratch_shapes=[pltpu.VMEM((2 * SLABS_PER_GROUP, ROW_TILE, LANES), F32),
                        pltpu.VMEM((SLABS_PER_GROUP, DIL_GROUPS[1][1], ROW_TILE // DIL_GROUPS[1][1], LANES), F32)],
        compiler_params=pltpu.CompilerParams(dimension_semantics=("arbitrary",), vmem_limit_bytes=VMEM_LIMIT),
        name="dil_proj",
    )(h, g.reshape(1, D_MODEL), w)


DIL_TILE = 1024
DIL_STEPS = SEQ // DIL_TILE


def _dil_step(tab_ref, bucket, qkv_ref, prev_ref, ol_ref, bias_sc, diag_sc, k_sc, v_sc,
              *, first, n, head0, diag_bucket):
    heads = [head0 + h for h in range(DIL_HEADS)]
    nq = DIL_HEADS * BLOCK
    q_ref = qkv_ref.at[:, 0:DIL_W]

    @pl.when(first)
    def _():
        _fill_bias(bias_sc, tab_ref, bucket, heads)
        _fill_row(diag_sc, [tab_ref[diag_bucket, h] for h in heads])

    k_sc[0:BLOCK, :] = prev_ref[:, DIL_W:2 * DIL_W]
    k_sc[BLOCK:, :] = qkv_ref[:, DIL_W:2 * DIL_W]
    v_sc[0:BLOCK, :] = prev_ref[:, 2 * DIL_W:]
    v_sc[BLOCK:, :] = qkv_ref[:, 2 * DIL_W:]
    own, eye = _fold_masks(nq)
    upper_rows = lax.broadcasted_iota(jnp.int32, (BLOCK, BLOCK), 0) < HEAD_DIM

    def block_body(b, carry):
        r0 = b * BLOCK
        has_prev = (n > 0) if b == 0 else None
        qb = q_ref[pl.ds(r0, BLOCK), :]
        kb = k_sc[pl.ds(r0, 2 * BLOCK), :]
        vb = v_sc[pl.ds(r0, 2 * BLOCK), :]
        slab = lambda x, s: x[:, s * LANES:(s + 1) * LANES]
        scores = []
        for s in range(DIL_W // LANES):
            scores += _pair_scores(slab(kb, s), slab(qb, s))
        pt, lse = _band_softmax(jnp.concatenate(scores, axis=1), bias_sc[...], own, eye, has_prev, None,
                                diag_sc[...])
        outs, lses = [], []
        for s in range(DIL_W // LANES):
            ca, cb = 2 * s * BLOCK, (2 * s + 1) * BLOCK
            outs.append(jnp.transpose(_pair_values(slab(vb, s), pt[:, ca:ca + BLOCK], pt[:, cb:cb + BLOCK])))
            lse_t = jnp.where(upper_rows, jnp.broadcast_to(lse[:, ca:ca + BLOCK], (BLOCK, BLOCK)),
                              jnp.broadcast_to(lse[:, cb:cb + BLOCK], (BLOCK, BLOCK)))
            lses.append(jnp.transpose(lse_t))
        ol_ref[pl.ds(r0, BLOCK), :] = jnp.concatenate(outs + lses, axis=-1)
        return carry

    for b in range(DIL_TILE // BLOCK):
        block_body(b, None)


def _dil_kernel(*refs):
    n_grp = len(DIL_GROUPS)
    tab_ref, bucket_ref = refs[:2]
    ins, outs, scratch = refs[2:2 + 2 * n_grp], refs[2 + 2 * n_grp:2 + 3 * n_grp], refs[2 + 3 * n_grp:]
    s = pl.program_id(0)
    for gi, (_, d) in enumerate(DIL_GROUPS):
        @pl.when(s // DIL_STEPS == gi)
        def _(gi=gi, d=d):
            local = s - gi * DIL_STEPS
            tiles = SEQ // d // DIL_TILE
            _dil_step(tab_ref, bucket_ref[gi], ins[2 * gi], ins[2 * gi + 1], outs[gi], *scratch,
                      first=local == 0, n=local % tiles, head0=gi * DIL_HEADS, diag_bucket=_diag_bucket(d))


def _dil_attention(qkvs, table):
    in_specs = [pl.BlockSpec(memory_space=pltpu.SMEM),
                pl.BlockSpec((len(DIL_GROUPS), BLOCK, BLOCK), lambda s: (0, 0, 0))]
    out_specs, out_shape, operands = [], [], []
    for gi, ((window, d), qkv) in enumerate(zip(DIL_GROUPS, qkvs)):
        rows = SEQ // d
        tiles = rows // DIL_TILE
        assert qkv.shape[:2] == (d, rows) and window // d == BLOCK and d * tiles == DIL_STEPS

        def pos(s, gi=gi, tiles=tiles):
            local = jnp.clip(s - gi * DIL_STEPS, 0, DIL_STEPS - 1)
            return local // tiles, local % tiles

        def tile(width, pos=pos):
            return pl.BlockSpec((None, DIL_TILE, width), lambda s: (*pos(s), 0))

        def prev_index(s, pos=pos):
            r, n = pos(s)
            return r, jnp.maximum(n * (DIL_TILE // BLOCK) - 1, 0), 0

        in_specs += [tile(DIL_SLAB), pl.BlockSpec((None, BLOCK, DIL_SLAB), prev_index)]
        operands += [qkv, qkv]
        out_specs += [tile(2 * DIL_W)]
        out_shape += [jax.ShapeDtypeStruct((d, rows, 2 * DIL_W), F32)]
    buckets = jnp.asarray(np.stack([_folded_bucket_map(d) for _, d in DIL_GROUPS]))
    return pl.pallas_call(
        _dil_kernel,
        grid=(len(DIL_GROUPS) * DIL_STEPS,),
        in_specs=in_specs,
        out_specs=out_specs,
        out_shape=out_shape,
        scratch_shapes=[pltpu.VMEM((BLOCK, DIL_HEADS * BLOCK), F32),
                        pltpu.VMEM((1, DIL_HEADS * BLOCK), F32),
                        pltpu.VMEM((DIL_TILE + BLOCK, DIL_W), BF16),
                        pltpu.VMEM((DIL_TILE + BLOCK, DIL_W), BF16)],
        compiler_params=pltpu.CompilerParams(dimension_semantics=("arbitrary",), vmem_limit_bytes=VMEM_LIMIT),
        name="dil_attn",
    )(table, buckets, *operands)


def _dilout_kernel(ol0_ref, ol1_ref, ol2_ref, qm_ref, kvm_ref, wout_ref, h_ref, g_ref,
                   router_ref, h2_ref, hn2_ref, meta_ref, cnt_ref, metat_ref, carry_sc, tok_sc):
    @pl.when(pl.program_id(0) == 0)
    def _():
        carry_sc[...] = jnp.zeros_like(carry_sc)

    n_slab = DIL_W // LANES
    for k, src_ref in enumerate((ol1_ref, ol2_ref)):
        d = src_ref.shape[0]
        for s in range(2 * n_slab):
            for r in range(d):
                tok_sc[k, s, pl.ds(r, ROW_TILE // d, stride=d), :] = src_ref[r, :, s * LANES:(s + 1) * LANES]

    mixed = []
    for s in range(n_slab):
        cols = slice(s * LANES, (s + 1) * LANES)
        lcols = slice(DIL_W + s * LANES, DIL_W + (s + 1) * LANES)
        o0, o1, o2 = ol0_ref[:, cols], tok_sc[0, s], tok_sc[1, s]
        l0, l1, l2 = ol0_ref[:, lcols], tok_sc[0, n_slab + s], tok_sc[1, n_slab + s]
        mx = jnp.maximum(jnp.maximum(l0, l1), l2)
        e0, e1, e2 = jnp.exp(l0 - mx), jnp.exp(l1 - mx), jnp.exp(l2 - mx)
        inv = 1.0 / (e0 + e1 + e2)
        mixed.append(((e0 * inv) * o0 + (e1 * inv) * o1 + (e2 * inv) * o2).astype(BF16))
    cat = jnp.concatenate(mixed + [_to_rows(_mem_attention(qm_ref[...], kvm_ref[...]))], axis=-1)
    out = h_ref[...] + jnp.dot(cat, wout_ref[...], preferred_element_type=F32)
    h2_ref[...] = out
    hn = _rms(out, g_ref[...]).astype(BF16)
    hn2_ref[...] = hn

    logits = _dot_nt(router_ref[...], hn)
    row = lax.broadcasted_iota(jnp.int32, logits.shape, 0)
    masked = jnp.where(row < N_EXPERTS, logits, -jnp.inf)
    v0 = jnp.max(masked, axis=0, keepdims=True)
    i0 = jnp.min(jnp.where(masked == v0, row, ROUTER_ROWS), axis=0, keepdims=True)
    rest = jnp.where(row == i0, -jnp.inf, masked)
    v1 = jnp.max(rest, axis=0, keepdims=True)
    i1 = jnp.min(jnp.where(rest == v1, row, ROUTER_ROWS), axis=0, keepdims=True)
    ex = jnp.exp(v1 - v0)
    inv2 = 1.0 / (1.0 + ex)

    oh0, oh1 = row == i0, row == i1
    sel = (oh0 | oh1).astype(F32)
    tok_r = lax.broadcasted_iota(jnp.int32, (ROW_TILE, ROW_TILE), 0)
    tok_c = lax.broadcasted_iota(jnp.int32, (ROW_TILE, ROW_TILE), 1)
    earlier = (tok_r < tok_c).astype(F32).astype(BF16)
    carry = carry_sc[:, 0:1]
    before = jnp.dot(sel.astype(BF16), earlier, preferred_element_type=F32) + carry
    rank0 = jnp.sum(jnp.where(oh0, before, 0.0), axis=0, keepdims=True)
    rank1 = jnp.sum(jnp.where(oh1, before, 0.0), axis=0, keepdims=True)
    count = jnp.broadcast_to(carry + jnp.sum(sel, axis=1, keepdims=True), carry_sc.shape)
    carry_sc[...] = count
    cnt_ref[...] = count[:SUBLANES]

    fields = (i0.astype(F32), i1.astype(F32), inv2, ex * inv2, rank0, rank1)
    meta_t = jnp.zeros((LANES, ROW_TILE), F32)
    frow = lax.broadcasted_iota(jnp.int32, meta_t.shape, 0)
    for k, f in enumerate(fields):
        meta_t = jnp.where(frow == k, f, meta_t)
    metat_ref[...] = meta_t[:SUBLANES]
    meta_ref[...] = jnp.transpose(meta_t)


META_E0, META_E1, META_G0, META_G1, META_RANK0, META_RANK1 = range(6)
ROUTER_ROWS = 16


def _dilout(ols, tok, kvm, wout, h, g_ffn, router):
    row = lambda i: (i, 0)
    const = lambda i: (0, 0)

    def group_spec(a):
        d = a.shape[0]
        if d == 1:
            return pl.BlockSpec((None, ROW_TILE, 2 * DIL_W), lambda i: (0, i, 0))
        return pl.BlockSpec((d, ROW_TILE // d, 2 * DIL_W), lambda i: (0, i, 0))

    return pl.pallas_call(
        _dilout_kernel,
        grid=(SEQ // ROW_TILE,),
        in_specs=[group_spec(a) for a in ols] + [
                  pl.BlockSpec((ROW_TILE, MEM_W), lambda i: (i, DIL_SLAB // MEM_W)),
                  pl.BlockSpec((N_MEM, 2 * MEM_W), lambda i: (0, 1)),
                  pl.BlockSpec((DIL_W + MEM_W, D_MODEL), const),
                  pl.BlockSpec((ROW_TILE, D_MODEL), row),
                  pl.BlockSpec((1, D_MODEL), const),
                  pl.BlockSpec((ROUTER_ROWS, D_MODEL), const)],
        out_specs=[pl.BlockSpec((ROW_TILE, D_MODEL), row),
                   pl.BlockSpec((ROW_TILE, D_MODEL), row),
                   pl.BlockSpec((ROW_TILE, LANES), row),
                   pl.BlockSpec((SUBLANES, LANES), const),
                   pl.BlockSpec((SUBLANES, ROW_TILE), lambda i: (0, i))],
        out_shape=[jax.ShapeDtypeStruct((SEQ, D_MODEL), F32), jax.ShapeDtypeStruct((SEQ, D_MODEL), BF16),
                   jax.ShapeDtypeStruct((SEQ, LANES), F32), jax.ShapeDtypeStruct((SUBLANES, LANES), F32),
                   jax.ShapeDtypeStruct((SUBLANES, SEQ), F32)],
        scratch_shapes=[pltpu.VMEM((ROUTER_ROWS, LANES), F32),
                        pltpu.VMEM((2, 2 * DIL_W // LANES, ROW_TILE, LANES), F32)],
        compiler_params=pltpu.CompilerParams(dimension_semantics=("arbitrary",), vmem_limit_bytes=VMEM_LIMIT),
        name="dil_out",
    )(*ols, tok, kvm, wout, h, g_ffn.reshape(1, D_MODEL), router)


def _swiglu_partial(x, wg, wu, wd):
    a = jnp.dot(x, wg, preferred_element_type=F32)
    b = jnp.dot(x, wu, preferred_element_type=F32)
    hm = (a * jax.nn.sigmoid(a) * b).astype(BF16)
    return jnp.dot(hm, wd, preferred_element_type=F32)


def _ffn_kernel(x_ref, wg_ref, wu_ref, wd_ref, h_ref, wa_ref, o_ref, wa_out, zero_out):
    wa_out[...] = wa_ref[...].astype(BF16)
    zero_out[...] = jnp.zeros(zero_out.shape, zero_out.dtype)
    x = x_ref[...]
    acc = h_ref[...]
    for c in range(D_FF // FFN_COL_TILE):
        cols = slice(c * FFN_COL_TILE, (c + 1) * FFN_COL_TILE)
        acc = acc + _swiglu_partial(x, wg_ref[:, cols], wu_ref[:, cols], wd_ref[cols, :])
    o_ref[...] = acc


def _ffn(x, h, wg, wu, wd, wa):
    tm = ROW_TILE
    steps = SEQ // tm
    resident = lambda shape: pl.BlockSpec(shape, lambda i: (0, 0), pipeline_mode=pl.Buffered(1))
    return pl.pallas_call(
        _ffn_kernel,
        grid=(steps,),
        in_specs=[pl.BlockSpec((tm, D_MODEL), lambda i: (i, 0)),
                  resident((D_MODEL, D_FF)), resident((D_MODEL, D_FF)), resident((D_FF, D_MODEL)),
                  pl.BlockSpec((tm, D_MODEL), lambda i: (i, 0)),
                  _cast_slice_spec(wa, steps)],
        out_specs=[pl.BlockSpec((tm, D_MODEL), lambda i: (i, 0)), _cast_slice_spec(wa, steps),
                   pl.BlockSpec((MOE_SLOTS // steps, PACK_W), lambda i: (i, 0))],
        out_shape=[jax.ShapeDtypeStruct((SEQ, D_MODEL), F32), jax.ShapeDtypeStruct(wa.shape, BF16),
                   jax.ShapeDtypeStruct((MOE_SLOTS, PACK_W), jnp.uint32)],
        compiler_params=pltpu.CompilerParams(dimension_semantics=("arbitrary",), vmem_limit_bytes=VMEM_LIMIT),
        name="ffn",
    )(x, wg, wu, wd, h, wa)


MOE_SUB = 512
MOE_ROWS = 2048
MOE_COL_TILE = D_FF // 2
MOE_SUBS = MOE_ROWS // MOE_SUB
MOE_SLOTS = 2 * SEQ + N_EXPERTS * MOE_SUB
MOE_TILES = MOE_SLOTS // MOE_ROWS
MOE_ITEMS = MOE_TILES + N_EXPERTS - 1
PACK_W = D_MODEL // 2
ITEM_TILE, ITEM_EXPERT, ITEM_LO, ITEM_HI, ITEM_FIRST = range(5)
TOKEN_TILE = 1024


def _pack_bf16_pairs(x):
    lo = lax.bitcast_convert_type(x[:, :PACK_W].astype(F32), jnp.uint32) >> 16
    hi = lax.bitcast_convert_type(x[:, PACK_W:].astype(F32), jnp.uint32) & jnp.uint32(0xFFFF0000)
    return hi | lo


def _unpack_bf16_pairs(pk):
    lo = lax.bitcast_convert_type(pk << 16, F32).astype(BF16)
    hi = lax.bitcast_convert_type(pk & jnp.uint32(0xFFFF0000), F32).astype(BF16)
    return lo, hi


def _vmem_row(ref, group, sub):
    return ref.at[group, pl.ds(sub, 1), :]


def _hbm_row(ref, row):
    return ref.at[pl.ds(row, 1), :]


def _scatter_kernel(pos0_ref, pos1_ref, x_ref, xs_in_ref, xs_ref, pk_sc, sem):
    del xs_in_ref
    pk_sc[...] = _pack_bf16_pairs(x_ref[...]).reshape(pk_sc.shape)

    def issue(g, c):
        for u in range(SUBLANES):
            t = g * SUBLANES + u
            pltpu.make_async_copy(_vmem_row(pk_sc, g, u), _hbm_row(xs_ref, pos0_ref[t]), sem).start(priority=0)
            pltpu.make_async_copy(_vmem_row(pk_sc, g, u), _hbm_row(xs_ref, pos1_ref[t]), sem).start(priority=1)
        return c

    def drain(g, c):
        for _ in range(2 * SUBLANES):
            pltpu.make_async_copy(_vmem_row(pk_sc, 0, 0), _hbm_row(xs_ref, 0), sem).wait()
        return c

    lax.fori_loop(0, TOKEN_TILE // SUBLANES, issue, 0)
    lax.fori_loop(0, TOKEN_TILE // SUBLANES, drain, 0)


def _moe_scatter(x, pos0, pos1, zero_slots):
    smem_tile = pl.BlockSpec((TOKEN_TILE,), lambda i: (i,), memory_space=pltpu.SMEM)
    return pl.pallas_call(
        _scatter_kernel,
        grid=(SEQ // TOKEN_TILE,),
        in_specs=[smem_tile, smem_tile,
                  pl.BlockSpec((TOKEN_TILE, D_MODEL), lambda i: (i, 0)),
                  pl.BlockSpec(memory_space=pl.ANY)],
        out_specs=pl.BlockSpec(memory_space=pl.ANY),
        out_shape=jax.ShapeDtypeStruct((MOE_SLOTS, PACK_W), jnp.uint32),
        scratch_shapes=[pltpu.VMEM((TOKEN_TILE // SUBLANES, SUBLANES, PACK_W), jnp.uint32),
                        pltpu.SemaphoreType.DMA],
        input_output_aliases={3: 0},
        compiler_params=pltpu.CompilerParams(dimension_semantics=("arbitrary",), vmem_limit_bytes=VMEM_LIMIT),
        name="moe_scatter",
    )(pos0, pos1, x, zero_slots)


def _moe_kernel(items_ref, xs_ref, wg_ref, wu_ref, wd_ref, y_ref, xb_sc):
    w = pl.program_id(0)
    j = pl.program_id(1)
    lo = items_ref[ITEM_LO, w]
    hi = items_ref[ITEM_HI, w]

    @pl.when((j == 0) & (hi > lo))
    def _():
        xlo, xhi = _unpack_bf16_pairs(xs_ref[...])
        xb_sc[:, :PACK_W] = xlo
        xb_sc[:, PACK_W:] = xhi

    @pl.when((j == 0) & (items_ref[ITEM_FIRST, w] == 1))
    def _():
        y_ref[...] = jnp.zeros_like(y_ref)

    for sb in range(MOE_SUBS):
        @pl.when((lo <= sb) & (sb < hi))
        def _(sb=sb):
            rows = slice(sb * MOE_SUB, (sb + 1) * MOE_SUB)
            x = xb_sc[rows, :]
            acc = y_ref[rows, :]
            for c0 in range(0, MOE_COL_TILE, FFN_COL_TILE):
                cols = slice(c0, min(c0 + FFN_COL_TILE, MOE_COL_TILE))
                acc = acc + _swiglu_partial(x, wg_ref[0, :, cols], wu_ref[0, :, cols], wd_ref[0, cols, :])
            y_ref[rows, :] = acc


def _moe_grouped(items, xs, wg, wu, wd):
    tf = MOE_COL_TILE
    nf = D_FF // tf
    jf = lambda w, j, it: jnp.where(it[ITEM_HI, w] > it[ITEM_LO, w], j, nf - 1)
    grid_spec = pltpu.PrefetchScalarGridSpec(
        num_scalar_prefetch=1,
        grid=(MOE_ITEMS, nf),
        in_specs=[pl.BlockSpec((MOE_ROWS, PACK_W), lambda w, j, it: (it[ITEM_TILE, w], 0)),
                  pl.BlockSpec((1, D_MODEL, tf), lambda w, j, it: (it[ITEM_EXPERT, w], 0, jf(w, j, it))),
                  pl.BlockSpec((1, D_MODEL, tf), lambda w, j, it: (it[ITEM_EXPERT, w], 0, jf(w, j, it))),
                  pl.BlockSpec((1, tf, D_MODEL), lambda w, j, it: (it[ITEM_EXPERT, w], jf(w, j, it), 0))],
        out_specs=pl.BlockSpec((MOE_ROWS, D_MODEL), lambda w, j, it: (it[ITEM_TILE, w], 0)),
        scratch_shapes=[pltpu.VMEM((MOE_ROWS, D_MODEL), BF16)],
    )
    return pl.pallas_call(
        _moe_kernel,
        grid_spec=grid_spec,
        out_shape=jax.ShapeDtypeStruct((MOE_SLOTS, D_MODEL), F32),
        compiler_params=pltpu.CompilerParams(dimension_semantics=("arbitrary", "arbitrary"),
                                             vmem_limit_bytes=VMEM_LIMIT),
        name="moe_grouped",
    )(items, xs, wg, wu, wd)


def _combine_kernel(pos0_ref, pos1_ref, meta_ref, h_ref, gf_ref, y_ref, o_ref, y0_sc, y1_sc, sem):
    def issue(g, c):
        for u in range(SUBLANES):
            t = g * SUBLANES + u
            pltpu.make_async_copy(_hbm_row(y_ref, pos0_ref[t]), _vmem_row(y0_sc, g, u), sem).start(priority=0)
            pltpu.make_async_copy(_hbm_row(y_ref, pos1_ref[t]), _vmem_row(y1_sc, g, u), sem).start(priority=1)
        return c

    def drain(g, c):
        for _ in range(SUBLANES):
            pltpu.make_async_copy(_hbm_row(y_ref, 0), _vmem_row(y0_sc, 0, 0), sem).wait()
            pltpu.make_async_copy(_hbm_row(y_ref, 0), _vmem_row(y1_sc, 0, 0), sem).wait()
        return c

    lax.fori_loop(0, TOKEN_TILE // SUBLANES, issue, 0)
    lax.fori_loop(0, TOKEN_TILE // SUBLANES, drain, 0)
    meta = meta_ref[...]
    g0 = meta[:, META_G0:META_G0 + 1]
    g1 = meta[:, META_G1:META_G1 + 1]
    y0 = y0_sc[...].reshape(TOKEN_TILE, D_MODEL)
    y1 = y1_sc[...].reshape(TOKEN_TILE, D_MODEL)
    o_ref[...] = _rms(h_ref[...] + (g0 * y0 + g1 * y1), gf_ref[...])


def _moe_combine(pos0, pos1, meta, h, g_final, y):
    smem_tile = pl.BlockSpec((TOKEN_TILE,), lambda i: (i,), memory_space=pltpu.SMEM)
    tiles = (TOKEN_TILE // SUBLANES, SUBLANES, D_MODEL)
    return pl.pallas_call(
        _combine_kernel,
        grid=(SEQ // TOKEN_TILE,),
        in_specs=[smem_tile, smem_tile,
                  pl.BlockSpec((TOKEN_TILE, LANES), lambda i: (i, 0)),
                  pl.BlockSpec((TOKEN_TILE, D_MODEL), lambda i: (i, 0)),
                  pl.BlockSpec((1, D_MODEL), lambda i: (0, 0)),
                  pl.BlockSpec(memory_space=pl.ANY)],
        out_specs=pl.BlockSpec((TOKEN_TILE, D_MODEL), lambda i: (i, 0)),
        out_shape=jax.ShapeDtypeStruct((SEQ, D_MODEL), F32),
        scratch_shapes=[pltpu.VMEM(tiles, F32), pltpu.VMEM(tiles, F32), pltpu.SemaphoreType.DMA],
        compiler_params=pltpu.CompilerParams(dimension_semantics=("arbitrary",), vmem_limit_bytes=VMEM_LIMIT),
        name="moe_combine",
    )(pos0, pos1, meta, h, g_final.reshape(1, D_MODEL), y)


def _moe_plan(meta_t, counts):
    i32 = jnp.int32
    cnt = counts[:N_EXPERTS, 0].astype(i32)
    padded = (cnt + MOE_SUB - 1) // MOE_SUB * MOE_SUB
    start = jnp.cumsum(padded) - padded

    def slot(e_row, rank_row):
        e = meta_t[e_row].astype(i32)
        base = sum(jnp.where(e == k, start[k], 0) for k in range(N_EXPERTS))
        return base + meta_t[rank_row].astype(i32)

    pos0, pos1 = slot(META_E0, META_RANK0), slot(META_E1, META_RANK1)

    first_sb, end_sb = start // MOE_SUB, (start + padded) // MOE_SUB
    t0 = jnp.arange(MOE_TILES, dtype=i32)[:, None] * MOE_SUBS
    lo = jnp.clip(first_sb[None, :] - t0, 0, MOE_SUBS)
    hi = jnp.clip(end_sb[None, :] - t0, 0, MOE_SUBS)
    active = hi > lo
    none = ~jnp.any(active, axis=1, keepdims=True)
    active = jnp.concatenate([active, none], axis=1)
    lo = jnp.concatenate([lo, jnp.zeros_like(t0)], axis=1)
    hi = jnp.concatenate([hi, jnp.zeros_like(t0)], axis=1)
    first = active & (jnp.cumsum(active.astype(i32), axis=1) == 1)
    ncol = N_EXPERTS + 1
    flat = jnp.arange(MOE_TILES * ncol, dtype=i32)
    key = jnp.where(active.reshape(-1), flat, MOE_TILES * ncol)
    order = jnp.argsort(key)[:MOE_ITEMS]
    live = key[order] < MOE_TILES * ncol
    tile = jnp.where(live, order // ncol, MOE_TILES - 1)
    lo_i = jnp.where(live, lo.reshape(-1)[order], 0)
    hi_i = jnp.where(live, hi.reshape(-1)[order], 0)
    first_i = jnp.where(live, first.reshape(-1)[order], False).astype(i32)
    idx = jnp.arange(MOE_ITEMS, dtype=i32)
    src = jnp.maximum(lax.cummax(jnp.where(hi_i > lo_i, idx, -1), axis=0), 0)
    expert = jnp.minimum(order % ncol, N_EXPERTS - 1)[src]
    items = jnp.stack([tile, expert, lo_i, hi_i, first_i]).astype(i32)
    return pos0, pos1, items


def _take_runs(w, idx, axis):
    idx = list(idx)
    runs, start = [], 0
    for i in range(1, len(idx) + 1):
        if i == len(idx) or idx[i] != idx[i - 1] + 1:
            runs.append(lax.slice_in_dim(w, idx[start], idx[i - 1] + 1, axis=axis))
            start = i
    return jnp.concatenate(runs, axis=axis)


def _swa_weight_layout():
    head = lambda base, j: list(range(base + j * HEAD_DIM, base + (j + 1) * HEAD_DIM))
    k0, v0, m0 = SWA_Q, SWA_Q + SWA_KV, SWA_Q + 2 * SWA_KV
    cols, rows = [], []
    for a, b in SWA_PAIR_SLABS:
        cols += head(0, a) + head(0, b)
        rows += head(0, a) + head(0, b)
    for base in (k0, v0):
        cols += head(base, 0) + head(base, 1) + head(base, 2) + head(base, 2)
    cols += list(range(m0, m0 + MEM_W))
    rows += list(range(SWA_Q, SWA_Q + MEM_W))
    scale = np.ones((len(cols),), np.float32)
    scale[:SWA_Q] = Q_SCALE
    scale[SWA_M0:] = Q_SCALE
    assert len(cols) == SWA_PW and len(rows) == D_MODEL
    return np.asarray(cols), scale, np.asarray(rows)


def _dil_weight_layout():
    n_grp = len(DIL_GROUPS) * DIL_SLAB
    cols = list(range(DIL_SLAB)) + list(range(n_grp, n_grp + MEM_W)) + list(range(DIL_SLAB, n_grp))
    scale = np.ones((len(cols),), np.float32)
    for c0 in (0, DIL_SLAB, TOK_W, TOK_W + DIL_SLAB):
        scale[c0:c0 + DIL_W] = Q_SCALE
    return np.asarray(cols), scale


def kernel(x, mem, rel_bias_table, mem_norm, norm_mix, norm_ffn, final_norm, swa_w_in, swa_sinks, swa_w_mem_kv,
           swa_w_out, dil_w_in, dil_w_mem_kv, dil_w_out, ffn_gate, ffn_up, ffn_down, router, moe_gate, moe_up,
           moe_down):
    assert x.shape == (1, SEQ, D_MODEL) and mem.shape == (1, N_MEM, D_MODEL)
    assert norm_mix.shape == (2, D_MODEL) and swa_w_in.shape == (1, D_MODEL, SWA_IN)
    assert dil_w_in.shape == (1, D_MODEL, DIL_IN) and moe_gate.shape == (1, N_EXPERTS, D_MODEL, D_FF)
    bf = lambda a: a.astype(BF16)
    h0 = x.reshape(SEQ, D_MODEL)

    kvm = _memkv(mem[0], mem_norm, bf(jnp.concatenate([swa_w_mem_kv[0], dil_w_mem_kv[0]], axis=1)))

    cols0, scale0, rows0 = _swa_weight_layout()
    h1, hn1, moe_up_b, moe_down_b, ffn_gate_b, ffn_up_b, ffn_down_b = _swa_layer(
        bf(_take_runs(swa_w_in[0], cols0, 1) * scale0), norm_mix[0], kvm, rel_bias_table, swa_sinks[0],
        bf(_take_runs(swa_w_out[0], rows0, 0)), h0, norm_ffn[0],
        [moe_up[0], moe_down[0], ffn_gate, ffn_up, ffn_down])
    h2, moe_gate_b, xs_zero = _ffn(hn1, h1, ffn_gate_b[0], ffn_up_b[0], ffn_down_b[0], moe_gate[0])

    cols1, scale1 = _dil_weight_layout()
    tok, g1, g2 = _dilproj(h2, norm_mix[1], bf(_take_runs(dil_w_in[0], cols1, 1) * scale1))
    ols = _dil_attention((tok[None], g1, g2), rel_bias_table)
    router_p = jnp.pad(bf(router[0].T), ((0, ROUTER_ROWS - N_EXPERTS), (0, 0)))
    h3, hn3, meta, counts, meta_t = _dilout(ols, tok, kvm, bf(dil_w_out[0]), h2, norm_ffn[1], router_p)
    pos0, pos1, items = _moe_plan(meta_t, counts)
    xs = _moe_scatter(hn3, pos0, pos1, xs_zero)
    y = _moe_grouped(items, xs, moe_gate_b, moe_up_b, moe_down_b)
    out = _moe_combine(pos0, pos1, meta, h3, final_norm, y)
    return out.reshape(1, SEQ, D_MODEL)
```

```python
import functools
import math

import jax
import jax.numpy as jnp
import numpy as np
from jax import lax
from jax.experimental import pallas as pl
from jax.experimental.pallas import tpu as pltpu

F32 = jnp.float32
BF16 = jnp.bfloat16

D_MODEL = 1024
SEQ = 16384
HEAD_DIM = 64
N_MIX_HEADS = 12
SWA_KV_HEADS = 3
SWA_GROUP = N_MIX_HEADS // SWA_KV_HEADS
SWA_WINDOW = 128
DIL_GROUPS = ((128, 1), (512, 4), (2048, 16))
DIL_HEADS = 4
N_MEM = 256
MEM_HEADS = 4
BLOCK = 128
N_BUCKETS = 32
MAX_DISTANCE = 2048
D_FF = 3584
N_EXPERTS = 8
EPS = 1e-5
NEG = -1e30
Q_SCALE = HEAD_DIM ** -0.5

SWA_Q = N_MIX_HEADS * HEAD_DIM
SWA_KV = SWA_KV_HEADS * HEAD_DIM
SWA_IN = SWA_Q + 2 * SWA_KV + MEM_HEADS * HEAD_DIM
DIL_W = DIL_HEADS * HEAD_DIM
DIL_IN = len(DIL_GROUPS) * 3 * DIL_W + MEM_HEADS * HEAD_DIM
MEM_W = MEM_HEADS * HEAD_DIM

LANES = 128
SUBLANES = 8
VMEM_LIMIT = 56 * 1024 * 1024

ROW_TILE = 512
BLOCKS_PER_TILE = ROW_TILE // BLOCK
FFN_COL_TILE = 256


def _bucket_map(dil):
    qi = np.arange(BLOCK)[:, None]
    kj = np.arange(2 * BLOCK)[None, :]
    d = np.maximum((qi + BLOCK - kj) * dil, 0)
    max_exact = N_BUCKETS // 2
    ratio = np.maximum(d, 1).astype(np.float32) / np.float32(max_exact)
    large = max_exact + (np.log(ratio) / np.float32(math.log(MAX_DISTANCE / max_exact))
                         * np.float32(N_BUCKETS - max_exact)).astype(np.int32)
    return np.where(d < max_exact, d, np.minimum(large, N_BUCKETS - 1)).astype(np.int32)


def _rms(x, g):
    ms = jnp.mean(x * x, axis=-1, keepdims=True)
    return x * lax.rsqrt(ms + EPS) * g


def _dot_nt(a, b):
    return lax.dot_general(a, b, (((1,), (1,)), ((), ())), preferred_element_type=F32)


def _dot_tn(a, b):
    return lax.dot_general(a, b, (((0,), (0,)), ((), ())), preferred_element_type=F32)


def _fill_bias(bias_sc, tab_ref, bucket, heads):
    for j, h in enumerate(heads):
        def body(k, b, h=h):
            return jnp.where(bucket == k, tab_ref[k, h], b)
        bias_sc[:, j * BLOCK:(j + 1) * BLOCK] = lax.fori_loop(0, N_BUCKETS, body, jnp.zeros(bucket.shape, F32))


def _fill_row(row_sc, values):
    blk = lax.broadcasted_iota(jnp.int32, row_sc.shape, 1) // BLOCK
    row = jnp.zeros(row_sc.shape, F32)
    for j, v in enumerate(values):
        row = jnp.where(blk == j, v, row)
    row_sc[...] = row


def _folded_bucket_map(dil):
    full = _bucket_map(dil)
    qi = np.arange(BLOCK)[:, None]
    c = np.arange(BLOCK)[None, :]
    return np.ascontiguousarray(np.where(c <= qi, full[:, BLOCK:], full[:, :BLOCK]).T.astype(np.int32))


def _diag_bucket(dil):
    return int(_bucket_map(dil)[0, 0])


def _fold_masks(n):
    c = lax.broadcasted_iota(jnp.int32, (BLOCK, n), 0)
    qi = lax.broadcasted_iota(jnp.int32, (BLOCK, n), 1) % BLOCK
    return c <= qi, c == qi


def _pair_scores(k_slab, q_rows):
    lane = lax.broadcasted_iota(jnp.int32, q_rows.shape, 1)
    zero = jnp.zeros_like(q_rows)
    qa = jnp.where(lane < HEAD_DIM, q_rows, zero)
    qb = jnp.where(lane < HEAD_DIM, zero, q_rows)
    return _dot_nt(k_slab, qa), _dot_nt(k_slab, qb)


def _pair_values(v_slab, p_a, p_b):
    n = p_a.shape[1]
    o = _dot_tn(v_slab, jnp.concatenate([p_a, p_b], axis=1))
    row = lax.broadcasted_iota(jnp.int32, (LANES, n), 0)
    return jnp.where(row < HEAD_DIM, o[:, :n], o[:, n:])


def _band_softmax(st, bias, own, eye, has_prev, sink, diag_bias):
    s_prev, s_own = st[:BLOCK], st[BLOCK:]
    t = jnp.where(own, s_own, s_prev) + bias
    if has_prev is not None:
        t = jnp.where(own | has_prev, t, NEG)
    m = jnp.max(t, axis=0, keepdims=True)
    if diag_bias is not None:
        s_d = jnp.sum(jnp.where(eye, s_prev, 0.0), axis=0, keepdims=True) + diag_bias
        if has_prev is not None:
            s_d = jnp.where(has_prev, s_d, NEG)
        m = jnp.maximum(m, s_d)
    if sink is not None:
        m = jnp.maximum(m, sink)
    e = jnp.exp(t - m)
    den = jnp.sum(e, axis=0, keepdims=True)
    if diag_bias is not None:
        e_d = jnp.exp(s_d - m)
        den = den + e_d
    if sink is not None:
        den = den + jnp.exp(sink - m)
    inv = 1.0 / den
    p = e * inv
    p_own = jnp.where(own, p, 0.0)
    p_prev = jnp.where(own, 0.0, p)
    if diag_bias is not None:
        p_prev = jnp.where(eye, e_d * inv, p_prev)
    return jnp.concatenate([p_prev, p_own], axis=0).astype(BF16), m + jnp.log(den)


def _mem_attention(qm, kvm):
    outs = []
    for s in range(MEM_W // LANES):
        sa, sb = _pair_scores(kvm[:, s * LANES:(s + 1) * LANES], qm[:, s * LANES:(s + 1) * LANES])
        ps = []
        for st in (sa, sb):
            e = jnp.exp(st - jnp.max(st, axis=0, keepdims=True))
            ps.append((e * (1.0 / jnp.sum(e, axis=0, keepdims=True))).astype(BF16))
        outs.append(_pair_values(kvm[:, MEM_W + s * LANES:MEM_W + (s + 1) * LANES], ps[0], ps[1]))
    return outs


def _to_rows(slabs_t):
    return jnp.concatenate([jnp.transpose(x).astype(BF16) for x in slabs_t], axis=-1)


def _memkv_kernel(mem_ref, g_ref, w_ref, o_ref):
    mn = _rms(mem_ref[...], g_ref[...]).astype(BF16)
    o_ref[...] = jnp.dot(mn, w_ref[...], preferred_element_type=F32).astype(BF16)


def _memkv(mem, g, w):
    return pl.pallas_call(
        _memkv_kernel,
        out_shape=jax.ShapeDtypeStruct((N_MEM, w.shape[1]), BF16),
        name="mem_kv",
    )(mem, g.reshape(1, D_MODEL), w)


SWA_PAIR_SLABS = [(g, SWA_GROUP + g) for g in range(SWA_GROUP)] + [(8, 9), (10, 11)]
SWA_K0 = SWA_Q
SWA_V0 = SWA_K0 + 2 * LANES
SWA_M0 = SWA_V0 + 2 * LANES
SWA_PW = SWA_M0 + MEM_W
SWA_SCORE_HEADS = [0, 1, 2, 3, 4, 5, 6, 7, 8, 10, 9, 11]
SWA_NQ = N_MIX_HEADS * BLOCK


def _cast_slice_spec(w, steps):
    e, r, c = w.shape
    per_e = steps // e
    assert steps % e == 0 and r % per_e == 0 and (r // per_e) % 16 == 0
    return pl.BlockSpec((1, r // per_e, c), lambda i: (i // per_e, i % per_e, 0))


def _swa_kernel(*refs, n_cast):
    (tab_ref, sink_ref, bucket_ref, gmix_ref, win_ref, kvm_ref, wout_ref, h_ref, g_ref), refs = refs[:9], refs[9:]
    cast_in, (h1_ref, hn1_ref), refs = refs[:n_cast], refs[n_cast:n_cast + 2], refs[n_cast + 2:]
    cast_out, (bias_sc, sink_sc, k_sc, v_sc, cat_sc, p_ref, kprev_sc, vprev_sc) = refs[:n_cast], refs[n_cast:]
    i = pl.program_id(0)
    for src, dst in zip(cast_in, cast_out):
        dst[...] = src[...].astype(BF16)

    @pl.when(i == 0)
    def _():
        _fill_bias(bias_sc, tab_ref, bucket_ref[...], SWA_SCORE_HEADS)
        _fill_row(sink_sc, [sink_ref[h] for h in SWA_SCORE_HEADS])

        kprev_sc[...] = jnp.zeros(kprev_sc.shape, BF16)
        vprev_sc[...] = jnp.zeros(vprev_sc.shape, BF16)

    k_sc[0:BLOCK, :] = kprev_sc[...]
    v_sc[0:BLOCK, :] = vprev_sc[...]

    hn = _rms(h_ref[...], gmix_ref[...]).astype(BF16)
    p_ref[...] = jnp.dot(hn, win_ref[...], preferred_element_type=F32).astype(BF16)

    k_sc[BLOCK:, :] = p_ref[:, SWA_K0:SWA_V0]
    v_sc[BLOCK:, :] = p_ref[:, SWA_V0:SWA_M0]
    own, eye = _fold_masks(SWA_NQ)

    def block_body(b, carry):
        r0 = b * BLOCK
        has_prev = (i > 0) if b == 0 else None
        qb = p_ref[pl.ds(r0, BLOCK), 0:SWA_Q]
        slab = lambda x, s: x[:, s * LANES:(s + 1) * LANES]
        qa = jnp.concatenate([slab(qb, s) for s in range(4)], axis=0)
        qc = jnp.concatenate([slab(qb, s) for s in (4, 5)], axis=0)
        kb = k_sc[pl.ds(r0, 2 * BLOCK), :]
        vb = v_sc[pl.ds(r0, 2 * BLOCK), :]
        s0, s1 = _pair_scores(slab(kb, 0), qa)
        s2, s3 = _pair_scores(slab(kb, 1), qc)
        st = jnp.concatenate([s0, s1, s2, s3], axis=1)
        pt, _ = _band_softmax(st, bias_sc[...], own, eye, has_prev, sink_sc[...], None)
        na, nc = 4 * BLOCK, 2 * BLOCK
        oa = _pair_values(slab(vb, 0), pt[:, 0:na], pt[:, na:2 * na])
        oc = _pair_values(slab(vb, 1), pt[:, 2 * na:2 * na + nc], pt[:, 2 * na + nc:])
        outs = [slab(oa, s) for s in range(4)] + [slab(oc, s) for s in range(2)]
        outs += _mem_attention(p_ref[pl.ds(r0, BLOCK), SWA_M0:], kvm_ref[...])
        cat_sc[pl.ds(r0, BLOCK), :] = _to_rows(outs)
        return carry

    for b in range(BLOCKS_PER_TILE):
        block_body(b, None)

    kprev_sc[...] = p_ref[ROW_TILE - BLOCK:, SWA_K0:SWA_V0]
    vprev_sc[...] = p_ref[ROW_TILE - BLOCK:, SWA_V0:SWA_M0]

    out = h_ref[...] + jnp.dot(cat_sc[...], wout_ref[...], preferred_element_type=F32)
    h1_ref[...] = out
    hn1_ref[...] = _rms(out, g_ref[...]).astype(BF16)


def _swa_layer(w_in, g_mix, kvm, table, sinks, wout, h, g_ffn, casts):
    assert SWA_WINDOW == BLOCK
    kv_w = 2 * LANES
    steps = SEQ // ROW_TILE
    cast_specs = [_cast_slice_spec(w, steps) for w in casts]
    return pl.pallas_call(
        functools.partial(_swa_kernel, n_cast=len(casts)),
        grid=(SEQ // ROW_TILE,),
        in_specs=[
            pl.BlockSpec(memory_space=pltpu.SMEM),
            pl.BlockSpec(memory_space=pltpu.SMEM),
            pl.BlockSpec((BLOCK, BLOCK), lambda i: (0, 0)),
            pl.BlockSpec((1, D_MODEL), lambda i: (0, 0)),
            pl.BlockSpec((D_MODEL, SWA_PW), lambda i: (0, 0)),
            pl.BlockSpec((N_MEM, 2 * MEM_W), lambda i: (0, 0)),
            pl.BlockSpec((D_MODEL, D_MODEL), lambda i: (0, 0)),
            pl.BlockSpec((ROW_TILE, D_MODEL), lambda i: (i, 0)),
            pl.BlockSpec((1, D_MODEL), lambda i: (0, 0)),
        ] + cast_specs,
        out_specs=[pl.BlockSpec((ROW_TILE, D_MODEL), lambda i: (i, 0)),
                   pl.BlockSpec((ROW_TILE, D_MODEL), lambda i: (i, 0))] + cast_specs,
        out_shape=[jax.ShapeDtypeStruct((SEQ, D_MODEL), F32), jax.ShapeDtypeStruct((SEQ, D_MODEL), BF16)]
                  + [jax.ShapeDtypeStruct(w.shape, BF16) for w in casts],
        scratch_shapes=[pltpu.VMEM((BLOCK, SWA_NQ), F32),
                        pltpu.VMEM((1, SWA_NQ), F32),
                        pltpu.VMEM((ROW_TILE + BLOCK, kv_w), BF16),
                        pltpu.VMEM((ROW_TILE + BLOCK, kv_w), BF16),
                        pltpu.VMEM((ROW_TILE, D_MODEL), BF16),
                        pltpu.VMEM((ROW_TILE, SWA_PW), BF16),
                        pltpu.VMEM((BLOCK, kv_w), BF16),
                        pltpu.VMEM((BLOCK, kv_w), BF16)],
        compiler_params=pltpu.CompilerParams(dimension_semantics=("arbitrary",), vmem_limit_bytes=VMEM_LIMIT),
        name="swa_mixer",
    )(table, sinks, jnp.asarray(_folded_bucket_map(1)), g_mix.reshape(1, D_MODEL), w_in, kvm, wout, h,
      g_ffn.reshape(1, D_MODEL), *casts)


DIL_SLAB = 3 * DIL_W
TOK_W = DIL_SLAB + MEM_W
SLABS_PER_GROUP = DIL_SLAB // LANES


def _dilproj_kernel(h_ref, g_ref, w_ref, tok_ref, g1_ref, g2_ref, slab_sc, quarter_sc):
    hn = _rms(h_ref[...], g_ref[...]).astype(BF16)
    res = jnp.dot(hn, w_ref[...], preferred_element_type=F32)
    tok_ref[...] = res[:, :TOK_W].astype(BF16)
    for s in range(2 * SLABS_PER_GROUP):
        slab_sc[s] = res[:, TOK_W + s * LANES:TOK_W + (s + 1) * LANES]
    d1, d2 = DIL_GROUPS[1][1], DIL_GROUPS[2][1]
    assert d2 == d1 * d1
    for s in range(SLABS_PER_GROUP):
        cols = slice(s * LANES, (s + 1) * LANES)
        for r in range(d1):
            g1_ref[r, :, cols] = slab_sc[s, pl.ds(r, ROW_TILE // d1, stride=d1), :].astype(BF16)
        for q in range(d1):
            quarter_sc[s, q] = slab_sc[SLABS_PER_GROUP + s, pl.ds(q, ROW_TILE // d1, stride=d1), :]
        for q in range(d1):
            for a in range(d1):
                rows = quarter_sc[s, q, pl.ds(a, ROW_TILE // d2, stride=d1), :]
                g2_ref[d1 * a + q, :, cols] = rows.astype(BF16)


def _dilproj(h, g, w):
    d1, d2 = DIL_GROUPS[1][1], DIL_GROUPS[2][1]
    return pl.pallas_call(
        _dilproj_kernel,
        grid=(SEQ // ROW_TILE,),
        in_specs=[pl.BlockSpec((ROW_TILE, D_MODEL), lambda i: (i, 0)),
                  pl.BlockSpec((1, D_MODEL), lambda i: (0, 0)),
                  pl.BlockSpec((D_MODEL, DIL_IN), lambda i: (0, 0))],
        out_specs=[pl.BlockSpec((ROW_TILE, TOK_W), lambda i: (i, 0)),
                   pl.BlockSpec((d1, ROW_TILE // d1, DIL_SLAB), lambda i: (0, i, 0)),
                   pl.BlockSpec((d2, ROW_TILE // d2, DIL_SLAB), lambda i: (0, i, 0))],
        out_shape=[jax.ShapeDtypeStruct((SEQ, TOK_W), BF16),
                   jax.ShapeDtypeStruct((d1, SEQ // d1, DIL_SLAB), BF16),
                   jax.ShapeDtypeStruct((d2, SEQ // d2, DIL_SLAB), BF16)],
        scratch_shapes=[pltpu.VMEM((2 * SLABS_PER_GROUP, ROW_TILE, LANES), F32),
                        pltpu.VMEM((SLABS_PER_GROUP, DIL_GROUPS[1][1], ROW_TILE // DIL_GROUPS[1][1], LANES), F32)],
        compiler_params=pltpu.CompilerParams(dimension_semantics=("arbitrary",), vmem_limit_bytes=VMEM_LIMIT),
        name="dil_proj",
    )(h, g.reshape(1, D_MODEL), w)


DIL_TILE = 1024
DIL_STEPS = SEQ // DIL_TILE


def _dil_step(tab_ref, bucket, qkv_ref, prev_ref, ol_ref, bias_sc, diag_sc, k_sc, v_sc,
              *, first, n, head0, diag_bucket):
    heads = [head0 + h for h in range(DIL_HEADS)]
    nq = DIL_HEADS * BLOCK
    q_ref = qkv_ref.at[:, 0:DIL_W]

    @pl.when(first)
    def _():
        _fill_bias(bias_sc, tab_ref, bucket, heads)
        _fill_row(diag_sc, [tab_ref[diag_bucket, h] for h in heads])

    k_sc[0:BLOCK, :] = prev_ref[:, DIL_W:2 * DIL_W]
    k_sc[BLOCK:, :] = qkv_ref[:, DIL_W:2 * DIL_W]
    v_sc[0:BLOCK, :] = prev_ref[:, 2 * DIL_W:]
    v_sc[BLOCK:, :] = qkv_ref[:, 2 * DIL_W:]
    own, eye = _fold_masks(nq)
    upper_rows = lax.broadcasted_iota(jnp.int32, (BLOCK, BLOCK), 0) < HEAD_DIM

    def block_body(b, carry):
        r0 = b * BLOCK
        has_prev = (n > 0) if b == 0 else None
        qb = q_ref[pl.ds(r0, BLOCK), :]
        kb = k_sc[pl.ds(r0, 2 * BLOCK), :]
        vb = v_sc[pl.ds(r0, 2 * BLOCK), :]
        slab = lambda x, s: x[:, s * LANES:(s + 1) * LANES]
        scores = []
        for s in range(DIL_W // LANES):
            scores += _pair_scores(slab(kb, s), slab(qb, s))
        pt, lse = _band_softmax(jnp.concatenate(scores, axis=1), bias_sc[...], own, eye, has_prev, None,
                                diag_sc[...])
        outs, lses = [], []
        for s in range(DIL_W // LANES):
            ca, cb = 2 * s * BLOCK, (2 * s + 1) * BLOCK
            outs.append(jnp.transpose(_pair_values(slab(vb, s), pt[:, ca:ca + BLOCK], pt[:, cb:cb + BLOCK])))
            lse_t = jnp.where(upper_rows, jnp.broadcast_to(lse[:, ca:ca + BLOCK], (BLOCK, BLOCK)),
                              jnp.broadcast_to(lse[:, cb:cb + BLOCK], (BLOCK, BLOCK)))
            lses.append(jnp.transpose(lse_t))
        ol_ref[pl.ds(r0, BLOCK), :] = jnp.concatenate(outs + lses, axis=-1)
        return carry

    for b in range(DIL_TILE // BLOCK):
        block_body(b, None)


def _dil_kernel(*refs):
    n_grp = len(DIL_GROUPS)
    tab_ref, bucket_ref = refs[:2]
    ins, outs, scratch = refs[2:2 + 2 * n_grp], refs[2 + 2 * n_grp:2 + 3 * n_grp], refs[2 + 3 * n_grp:]
    s = pl.program_id(0)
    for gi, (_, d) in enumerate(DIL_GROUPS):
        @pl.when(s // DIL_STEPS == gi)
        def _(gi=gi, d=d):
            local = s - gi * DIL_STEPS
            tiles = SEQ // d // DIL_TILE
            _dil_step(tab_ref, bucket_ref[gi], ins[2 * gi], ins[2 * gi + 1], outs[gi], *scratch,
                      first=local == 0, n=local % tiles, head0=gi * DIL_HEADS, diag_bucket=_diag_bucket(d))


def _dil_attention(qkvs, table):
    in_specs = [pl.BlockSpec(memory_space=pltpu.SMEM),
                pl.BlockSpec((len(DIL_GROUPS), BLOCK, BLOCK), lambda s: (0, 0, 0))]
    out_specs, out_shape, operands = [], [], []
    for gi, ((window, d), qkv) in enumerate(zip(DIL_GROUPS, qkvs)):
        rows = SEQ // d
        tiles = rows // DIL_TILE
        assert qkv.shape[:2] == (d, rows) and window // d == BLOCK and d * tiles == DIL_STEPS

        def pos(s, gi=gi, tiles=tiles):
            local = jnp.clip(s - gi * DIL_STEPS, 0, DIL_STEPS - 1)
            return local // tiles, local % tiles

        def tile(width, pos=pos):
            return pl.BlockSpec((None, DIL_TILE, width), lambda s: (*pos(s), 0))

        def prev_index(s, pos=pos):
            r, n = pos(s)
            return r, jnp.maximum(n * (DIL_TILE // BLOCK) - 1, 0), 0

        in_specs += [tile(DIL_SLAB), pl.BlockSpec((None, BLOCK, DIL_SLAB), prev_index)]
        operands += [qkv, qkv]
        out_specs += [tile(2 * DIL_W)]
        out_shape += [jax.ShapeDtypeStruct((d, rows, 2 * DIL_W), F32)]
    buckets = jnp.asarray(np.stack([_folded_bucket_map(d) for _, d in DIL_GROUPS]))
    return pl.pallas_call(
        _dil_kernel,
        grid=(len(DIL_GROUPS) * DIL_STEPS,),
        in_specs=in_specs,
        out_specs=out_specs,
        out_shape=out_shape,
        scratch_shapes=[pltpu.VMEM((BLOCK, DIL_HEADS * BLOCK), F32),
                        pltpu.VMEM((1, DIL_HEADS * BLOCK), F32),
                        pltpu.VMEM((DIL_TILE + BLOCK, DIL_W), BF16),
                        pltpu.VMEM((DIL_TILE + BLOCK, DIL_W), BF16)],
        compiler_params=pltpu.CompilerParams(dimension_semantics=("arbitrary",), vmem_limit_bytes=VMEM_LIMIT),
        name="dil_attn",
    )(table, buckets, *operands)


def _dilout_kernel(ol0_ref, ol1_ref, ol2_ref, qm_ref, kvm_ref, wout_ref, h_ref, g_ref,
                   router_ref, h2_ref, hn2_ref, meta_ref, cnt_ref, metat_ref, carry_sc, tok_sc):
    @pl.when(pl.program_id(0) == 0)
    def _():
        carry_sc[...] = jnp.zeros_like(carry_sc)

    n_slab = DIL_W // LANES
    for k, src_ref in enumerate((ol1_ref, ol2_ref)):
        d = src_ref.shape[0]
        for s in range(2 * n_slab):
            for r in range(d):
                tok_sc[k, s, pl.ds(r, ROW_TILE // d, stride=d), :] = src_ref[r, :, s * LANES:(s + 1) * LANES]

    mixed = []
    for s in range(n_slab):
        cols = slice(s * LANES, (s + 1) * LANES)
        lcols = slice(DIL_W + s * LANES, DIL_W + (s + 1) * LANES)
        o0, o1, o2 = ol0_ref[:, cols], tok_sc[0, s], tok_sc[1, s]
        l0, l1, l2 = ol0_ref[:, lcols], tok_sc[0, n_slab + s], tok_sc[1, n_slab + s]
        mx = jnp.maximum(jnp.maximum(l0, l1), l2)
        e0, e1, e2 = jnp.exp(l0 - mx), jnp.exp(l1 - mx), jnp.exp(l2 - mx)
        inv = 1.0 / (e0 + e1 + e2)
        mixed.append(((e0 * inv) * o0 + (e1 * inv) * o1 + (e2 * inv) * o2).astype(BF16))
    cat = jnp.concatenate(mixed + [_to_rows(_mem_attention(qm_ref[...], kvm_ref[...]))], axis=-1)
    out = h_ref[...] + jnp.dot(cat, wout_ref[...], preferred_element_type=F32)
    h2_ref[...] = out
    hn = _rms(out, g_ref[...]).astype(BF16)
    hn2_ref[...] = hn

    logits = _dot_nt(router_ref[...], hn)
    row = lax.broadcasted_iota(jnp.int32, logits.shape, 0)
    masked = jnp.where(row < N_EXPERTS, logits, -jnp.inf)
    v0 = jnp.max(masked, axis=0, keepdims=True)
    i0 = jnp.min(jnp.where(masked == v0, row, ROUTER_ROWS), axis=0, keepdims=True)
    rest = jnp.where(row == i0, -jnp.inf, masked)
    v1 = jnp.max(rest, axis=0, keepdims=True)
    i1 = jnp.min(jnp.where(rest == v1, row, ROUTER_ROWS), axis=0, keepdims=True)
    ex = jnp.exp(v1 - v0)
    inv2 = 1.0 / (1.0 + ex)

    oh0, oh1 = row == i0, row == i1
    sel = (oh0 | oh1).astype(F32)
    tok_r = lax.broadcasted_iota(jnp.int32, (ROW_TILE, ROW_TILE), 0)
    tok_c = lax.broadcasted_iota(jnp.int32, (ROW_TILE, ROW_TILE), 1)
    earlier = (tok_r < tok_c).astype(F32).astype(BF16)
    carry = carry_sc[:, 0:1]
    before = jnp.dot(sel.astype(BF16), earlier, preferred_element_type=F32) + carry
    rank0 = jnp.sum(jnp.where(oh0, before, 0.0), axis=0, keepdims=True)
    rank1 = jnp.sum(jnp.where(oh1, before, 0.0), axis=0, keepdims=True)
    count = jnp.broadcast_to(carry + jnp.sum(sel, axis=1, keepdims=True), carry_sc.shape)
    carry_sc[...] = count
    cnt_ref[...] = count[:SUBLANES]

    fields = (i0.astype(F32), i1.astype(F32), inv2, ex * inv2, rank0, rank1)
    meta_t = jnp.zeros((LANES, ROW_TILE), F32)
    frow = lax.broadcasted_iota(jnp.int32, meta_t.shape, 0)
    for k, f in enumerate(fields):
        meta_t = jnp.where(frow == k, f, meta_t)
    metat_ref[...] = meta_t[:SUBLANES]
    meta_ref[...] = jnp.transpose(meta_t)


META_E0, META_E1, META_G0, META_G1, META_RANK0, META_RANK1 = range(6)
ROUTER_ROWS = 16


def _dilout(ols, tok, kvm, wout, h, g_ffn, router):
    row = lambda i: (i, 0)
    const = lambda i: (0, 0)

    def group_spec(a):
        d = a.shape[0]
        if d == 1:
            return pl.BlockSpec((None, ROW_TILE, 2 * DIL_W), lambda i: (0, i, 0))
        return pl.BlockSpec((d, ROW_TILE // d, 2 * DIL_W), lambda i: (0, i, 0))

    return pl.pallas_call(
        _dilout_kernel,
        grid=(SEQ // ROW_TILE,),
        in_specs=[group_spec(a) for a in ols] + [
                  pl.BlockSpec((ROW_TILE, MEM_W), lambda i: (i, DIL_SLAB // MEM_W)),
                  pl.BlockSpec((N_MEM, 2 * MEM_W), lambda i: (0, 1)),
                  pl.BlockSpec((DIL_W + MEM_W, D_MODEL), const),
                  pl.BlockSpec((ROW_TILE, D_MODEL), row),
                  pl.BlockSpec((1, D_MODEL), const),
                  pl.BlockSpec((ROUTER_ROWS, D_MODEL), const)],
        out_specs=[pl.BlockSpec((ROW_TILE, D_MODEL), row),
                   pl.BlockSpec((ROW_TILE, D_MODEL), row),
                   pl.BlockSpec((ROW_TILE, LANES), row),
                   pl.BlockSpec((SUBLANES, LANES), const),
                   pl.BlockSpec((SUBLANES, ROW_TILE), lambda i: (0, i))],
        out_shape=[jax.ShapeDtypeStruct((SEQ, D_MODEL), F32), jax.ShapeDtypeStruct((SEQ, D_MODEL), BF16),
                   jax.ShapeDtypeStruct((SEQ, LANES), F32), jax.ShapeDtypeStruct((SUBLANES, LANES), F32),
                   jax.ShapeDtypeStruct((SUBLANES, SEQ), F32)],
        scratch_shapes=[pltpu.VMEM((ROUTER_ROWS, LANES), F32),
                        pltpu.VMEM((2, 2 * DIL_W // LANES, ROW_TILE, LANES), F32)],
        compiler_params=pltpu.CompilerParams(dimension_semantics=("arbitrary",), vmem_limit_bytes=VMEM_LIMIT),
        name="dil_out",
    )(*ols, tok, kvm, wout, h, g_ffn.reshape(1, D_MODEL), router)


def _swiglu_partial(x, wg, wu, wd):
    a = jnp.dot(x, wg, preferred_element_type=F32)
    b = jnp.dot(x, wu, preferred_element_type=F32)
    hm = (a * jax.nn.sigmoid(a) * b).astype(BF16)
    return jnp.dot(hm, wd, preferred_element_type=F32)


def _ffn_kernel(x_ref, wg_ref, wu_ref, wd_ref, h_ref, wa_ref, o_ref, wa_out, zero_out):
    wa_out[...] = wa_ref[...].astype(BF16)
    zero_out[...] = jnp.zeros(zero_out.shape, zero_out.dtype)
    x = x_ref[...]
    acc = h_ref[...]
    for c in range(D_FF // FFN_COL_TILE):
        cols = slice(c * FFN_COL_TILE, (c + 1) * FFN_COL_TILE)
        acc = acc + _swiglu_partial(x, wg_ref[:, cols], wu_ref[:, cols], wd_ref[cols, :])
    o_ref[...] = acc


def _ffn(x, h, wg, wu, wd, wa):
    tm = ROW_TILE
    steps = SEQ // tm
    resident = lambda shape: pl.BlockSpec(shape, lambda i: (0, 0), pipeline_mode=pl.Buffered(1))
    return pl.pallas_call(
        _ffn_kernel,
        grid=(steps,),
        in_specs=[pl.BlockSpec((tm, D_MODEL), lambda i: (i, 0)),
                  resident((D_MODEL, D_FF)), resident((D_MODEL, D_FF)), resident((D_FF, D_MODEL)),
                  pl.BlockSpec((tm, D_MODEL), lambda i: (i, 0)),
                  _cast_slice_spec(wa, steps)],
        out_specs=[pl.BlockSpec((tm, D_MODEL), lambda i: (i, 0)), _cast_slice_spec(wa, steps),
                   pl.BlockSpec((MOE_SLOTS // steps, PACK_W), lambda i: (i, 0))],
        out_shape=[jax.ShapeDtypeStruct((SEQ, D_MODEL), F32), jax.ShapeDtypeStruct(wa.shape, BF16),
                   jax.ShapeDtypeStruct((MOE_SLOTS, PACK_W), jnp.uint32)],
        compiler_params=pltpu.CompilerParams(dimension_semantics=("arbitrary",), vmem_limit_bytes=VMEM_LIMIT),
        name="ffn",
    )(x, wg, wu, wd, h, wa)


MOE_SUB = 512
MOE_ROWS = 2048
MOE_COL_TILE = D_FF // 2
MOE_SUBS = MOE_ROWS // MOE_SUB
MOE_SLOTS = 2 * SEQ + N_EXPERTS * MOE_SUB
MOE_TILES = MOE_SLOTS // MOE_ROWS
MOE_ITEMS = MOE_TILES + N_EXPERTS - 1
PACK_W = D_MODEL // 2
ITEM_TILE, ITEM_EXPERT, ITEM_LO, ITEM_HI, ITEM_FIRST = range(5)
TOKEN_TILE = 2048


def _pack_bf16_pairs(x):
    lo = lax.bitcast_convert_type(x[:, :PACK_W].astype(F32), jnp.uint32) >> 16
    hi = lax.bitcast_convert_type(x[:, PACK_W:].astype(F32), jnp.uint32) & jnp.uint32(0xFFFF0000)
    return hi | lo


def _unpack_bf16_pairs(pk):
    lo = lax.bitcast_convert_type(pk << 16, F32).astype(BF16)
    hi = lax.bitcast_convert_type(pk & jnp.uint32(0xFFFF0000), F32).astype(BF16)
    return lo, hi


def _vmem_row(ref, group, sub):
    return ref.at[group, pl.ds(sub, 1), :]


def _hbm_row(ref, row):
    return ref.at[pl.ds(row, 1), :]


def _scatter_kernel(pos0_ref, pos1_ref, x_ref, xs_in_ref, xs_ref, pk_sc, sem):
    del xs_in_ref
    pk_sc[...] = _pack_bf16_pairs(x_ref[...]).reshape(pk_sc.shape)

    def issue(g, c):
        for u in range(SUBLANES):
            t = g * SUBLANES + u
            pltpu.make_async_copy(_vmem_row(pk_sc, g, u), _hbm_row(xs_ref, pos0_ref[t]), sem).start(priority=0)
            pltpu.make_async_copy(_vmem_row(pk_sc, g, u), _hbm_row(xs_ref, pos1_ref[t]), sem).start(priority=1)
        return c

    def drain(g, c):
        for _ in range(2 * SUBLANES):
            pltpu.make_async_copy(_vmem_row(pk_sc, 0, 0), _hbm_row(xs_ref, 0), sem).wait()
        return c

    lax.fori_loop(0, TOKEN_TILE // SUBLANES, issue, 0)
    lax.fori_loop(0, TOKEN_TILE // SUBLANES, drain, 0)


def _moe_scatter(x, pos0, pos1, zero_slots):
    smem_tile = pl.BlockSpec((TOKEN_TILE,), lambda i: (i,), memory_space=pltpu.SMEM)
    return pl.pallas_call(
        _scatter_kernel,
        grid=(SEQ // TOKEN_TILE,),
        in_specs=[smem_tile, smem_tile,
                  pl.BlockSpec((TOKEN_TILE, D_MODEL), lambda i: (i, 0)),
                  pl.BlockSpec(memory_space=pl.ANY)],
        out_specs=pl.BlockSpec(memory_space=pl.ANY),
        out_shape=jax.ShapeDtypeStruct((MOE_SLOTS, PACK_W), jnp.uint32),
        scratch_shapes=[pltpu.VMEM((TOKEN_TILE // SUBLANES, SUBLANES, PACK_W), jnp.uint32),
                        pltpu.SemaphoreType.DMA],
        input_output_aliases={3: 0},
        compiler_params=pltpu.CompilerParams(dimension_semantics=("arbitrary",), vmem_limit_bytes=VMEM_LIMIT),
        name="moe_scatter",
    )(pos0, pos1, x, zero_slots)


def _moe_kernel(items_ref, xs_ref, wg_ref, wu_ref, wd_ref, y_ref, xb_sc):
    w = pl.program_id(0)
    j = pl.program_id(1)
    lo = items_ref[ITEM_LO, w]
    hi = items_ref[ITEM_HI, w]

    @pl.when((j == 0) & (hi > lo))
    def _():
        xlo, xhi = _unpack_bf16_pairs(xs_ref[...])
        xb_sc[:, :PACK_W] = xlo
        xb_sc[:, PACK_W:] = xhi

    @pl.when((j == 0) & (items_ref[ITEM_FIRST, w] == 1))
    def _():
        y_ref[...] = jnp.zeros_like(y_ref)

    for sb in range(MOE_SUBS):
        @pl.when((lo <= sb) & (sb < hi))
        def _(sb=sb):
            rows = slice(sb * MOE_SUB, (sb + 1) * MOE_SUB)
            x = xb_sc[rows, :]
            acc = y_ref[rows, :]
            for c0 in range(0, MOE_COL_TILE, FFN_COL_TILE):
                cols = slice(c0, min(c0 + FFN_COL_TILE, MOE_COL_TILE))
                acc = acc + _swiglu_partial(x, wg_ref[0, :, cols], wu_ref[0, :, cols], wd_ref[0, cols, :])
            y_ref[rows, :] = acc


def _moe_grouped(items, xs, wg, wu, wd):
    tf = MOE_COL_TILE
    nf = D_FF // tf
    jf = lambda w, j, it: jnp.where(it[ITEM_HI, w] > it[ITEM_LO, w], j, nf - 1)
    grid_spec = pltpu.PrefetchScalarGridSpec(
        num_scalar_prefetch=1,
        grid=(MOE_ITEMS, nf),
        in_specs=[pl.BlockSpec((MOE_ROWS, PACK_W), lambda w, j, it: (it[ITEM_TILE, w], 0)),
                  pl.BlockSpec((1, D_MODEL, tf), lambda w, j, it: (it[ITEM_EXPERT, w], 0, jf(w, j, it))),
                  pl.BlockSpec((1, D_MODEL, tf), lambda w, j, it: (it[ITEM_EXPERT, w], 0, jf(w, j, it))),
                  pl.BlockSpec((1, tf, D_MODEL), lambda w, j, it: (it[ITEM_EXPERT, w], jf(w, j, it), 0))],
        out_specs=pl.BlockSpec((MOE_ROWS, D_MODEL), lambda w, j, it: (it[ITEM_TILE, w], 0)),
        scratch_shapes=[pltpu.VMEM((MOE_ROWS, D_MODEL), BF16)],
    )
    return pl.pallas_call(
        _moe_kernel,
        grid_spec=grid_spec,
        out_shape=jax.ShapeDtypeStruct((MOE_SLOTS, D_MODEL), F32),
        compiler_params=pltpu.CompilerParams(dimension_semantics=("arbitrary", "arbitrary"),
                                             vmem_limit_bytes=VMEM_LIMIT),
        name="moe_grouped",
    )(items, xs, wg, wu, wd)


def _combine_kernel(pos0_ref, pos1_ref, meta_ref, h_ref, gf_ref, y_ref, o_ref, y0_sc, y1_sc, sem):
    def issue(g, c):
        for u in range(SUBLANES):
            t = g * SUBLANES + u
            pltpu.make_async_copy(_hbm_row(y_ref, pos0_ref[t]), _vmem_row(y0_sc, g, u), sem).start(priority=0)
            pltpu.make_async_copy(_hbm_row(y_ref, pos1_ref[t]), _vmem_row(y1_sc, g, u), sem).start(priority=1)
        return c

    def drain(g, c):
        for _ in range(SUBLANES):
            pltpu.make_async_copy(_hbm_row(y_ref, 0), _vmem_row(y0_sc, 0, 0), sem).wait()
            pltpu.make_async_copy(_hbm_row(y_ref, 0), _vmem_row(y1_sc, 0, 0), sem).wait()
        return c

    lax.fori_loop(0, TOKEN_TILE // SUBLANES, issue, 0)
    lax.fori_loop(0, TOKEN_TILE // SUBLANES, drain, 0)
    meta = meta_ref[...]
    g0 = meta[:, META_G0:META_G0 + 1]
    g1 = meta[:, META_G1:META_G1 + 1]
    y0 = y0_sc[...].reshape(TOKEN_TILE, D_MODEL)
    y1 = y1_sc[...].reshape(TOKEN_TILE, D_MODEL)
    o_ref[...] = _rms(h_ref[...] + (g0 * y0 + g1 * y1), gf_ref[...])


def _moe_combine(pos0, pos1, meta, h, g_final, y):
    smem_tile = pl.BlockSpec((TOKEN_TILE,), lambda i: (i,), memory_space=pltpu.SMEM)
    tiles = (TOKEN_TILE // SUBLANES, SUBLANES, D_MODEL)
    return pl.pallas_call(
        _combine_kernel,
        grid=(SEQ // TOKEN_TILE,),
        in_specs=[smem_tile, smem_tile,
                  pl.BlockSpec((TOKEN_TILE, LANES), lambda i: (i, 0)),
                  pl.BlockSpec((TOKEN_TILE, D_MODEL), lambda i: (i, 0)),
                  pl.BlockSpec((1, D_MODEL), lambda i: (0, 0)),
                  pl.BlockSpec(memory_space=pl.ANY)],
        out_specs=pl.BlockSpec((TOKEN_TILE, D_MODEL), lambda i: (i, 0)),
        out_shape=jax.ShapeDtypeStruct((SEQ, D_MODEL), F32),
        scratch_shapes=[pltpu.VMEM(tiles, F32), pltpu.VMEM(tiles, F32), pltpu.SemaphoreType.DMA],
        compiler_params=pltpu.CompilerParams(dimension_semantics=("arbitrary",), vmem_limit_bytes=VMEM_LIMIT),
        name="moe_combine",
    )(pos0, pos1, meta, h, g_final.reshape(1, D_MODEL), y)


def _moe_plan(meta_t, counts):
    i32 = jnp.int32
    cnt = counts[:N_EXPERTS, 0].astype(i32)
    padded = (cnt + MOE_SUB - 1) // MOE_SUB * MOE_SUB
    start = jnp.cumsum(padded) - padded

    def slot(e_row, rank_row):
        e = meta_t[e_row].astype(i32)
        base = sum(jnp.where(e == k, start[k], 0) for k in range(N_EXPERTS))
        return base + meta_t[rank_row].astype(i32)

    pos0, pos1 = slot(META_E0, META_RANK0), slot(META_E1, META_RANK1)

    first_sb, end_sb = start // MOE_SUB, (start + padded) // MOE_SUB
    t0 = jnp.arange(MOE_TILES, dtype=i32)[:, None] * MOE_SUBS
    lo = jnp.clip(first_sb[None, :] - t0, 0, MOE_SUBS)
    hi = jnp.clip(end_sb[None, :] - t0, 0, MOE_SUBS)
    active = hi > lo
    none = ~jnp.any(active, axis=1, keepdims=True)
    active = jnp.concatenate([active, none], axis=1)
    lo = jnp.concatenate([lo, jnp.zeros_like(t0)], axis=1)
    hi = jnp.concatenate([hi, jnp.zeros_like(t0)], axis=1)
    first = active & (jnp.cumsum(active.astype(i32), axis=1) == 1)
    ncol = N_EXPERTS + 1
    flat = jnp.arange(MOE_TILES * ncol, dtype=i32)
    key = jnp.where(active.reshape(-1), flat, MOE_TILES * ncol)
    order = jnp.argsort(key)[:MOE_ITEMS]
    live = key[order] < MOE_TILES * ncol
    tile = jnp.where(live, order // ncol, MOE_TILES - 1)
    lo_i = jnp.where(live, lo.reshape(-1)[order], 0)
    hi_i = jnp.where(live, hi.reshape(-1)[order], 0)
    first_i = jnp.where(live, first.reshape(-1)[order], False).astype(i32)
    idx = jnp.arange(MOE_ITEMS, dtype=i32)
    src = jnp.maximum(lax.cummax(jnp.where(hi_i > lo_i, idx, -1), axis=0), 0)
    expert = jnp.minimum(order % ncol, N_EXPERTS - 1)[src]
    items = jnp.stack([tile, expert, lo_i, hi_i, first_i]).astype(i32)
    return pos0, pos1, items


def _take_runs(w, idx, axis):
    idx = list(idx)
    runs, start = [], 0
    for i in range(1, len(idx) + 1):
        if i == len(idx) or idx[i] != idx[i - 1] + 1:
            runs.append(lax.slice_in_dim(w, idx[start], idx[i - 1] + 1, axis=axis))
            start = i
    return jnp.concatenate(runs, axis=axis)


def _swa_weight_layout():
    head = lambda base, j: list(range(base + j * HEAD_DIM, base + (j + 1) * HEAD_DIM))
    k0, v0, m0 = SWA_Q, SWA_Q + SWA_KV, SWA_Q + 2 * SWA_KV
    cols, rows = [], []
    for a, b in SWA_PAIR_SLABS:
        cols += head(0, a) + head(0, b)
        rows += head(0, a) + head(0, b)
    for base in (k0, v0):
        cols += head(base, 0) + head(base, 1) + head(base, 2) + head(base, 2)
    cols += list(range(m0, m0 + MEM_W))
    rows += list(range(SWA_Q, SWA_Q + MEM_W))
    scale = np.ones((len(cols),), np.float32)
    scale[:SWA_Q] = Q_SCALE
    scale[SWA_M0:] = Q_SCALE
    assert len(cols) == SWA_PW and len(rows) == D_MODEL
    return np.asarray(cols), scale, np.asarray(rows)


def _dil_weight_layout():
    n_grp = len(DIL_GROUPS) * DIL_SLAB
    cols = list(range(DIL_SLAB)) + list(range(n_grp, n_grp + MEM_W)) + list(range(DIL_SLAB, n_grp))
    scale = np.ones((len(cols),), np.float32)
    for c0 in (0, DIL_SLAB, TOK_W, TOK_W + DIL_SLAB):
        scale[c0:c0 + DIL_W] = Q_SCALE
    return np.asarray(cols), scale


def kernel(x, mem, rel_bias_table, mem_norm, norm_mix, norm_ffn, final_norm, swa_w_in, swa_sinks, swa_w_mem_kv,
           swa_w_out, dil_w_in, dil_w_mem_kv, dil_w_out, ffn_gate, ffn_up, ffn_down, router, moe_gate, moe_up,
           moe_down):
    assert x.shape == (1, SEQ, D_MODEL) and mem.shape == (1, N_MEM, D_MODEL)
    assert norm_mix.shape == (2, D_MODEL) and swa_w_in.shape == (1, D_MODEL, SWA_IN)
    assert dil_w_in.shape == (1, D_MODEL, DIL_IN) and moe_gate.shape == (1, N_EXPERTS, D_MODEL, D_FF)
    bf = lambda a: a.astype(BF16)
    h0 = x.reshape(SEQ, D_MODEL)

    kvm = _memkv(mem[0], mem_norm, bf(jnp.concatenate([swa_w_mem_kv[0], dil_w_mem_kv[0]], axis=1)))

    cols0, scale0, rows0 = _swa_weight_layout()
    h1, hn1, moe_up_b, moe_down_b, ffn_gate_b, ffn_up_b, ffn_down_b = _swa_layer(
        bf(_take_runs(swa_w_in[0], cols0, 1) * scale0), norm_mix[0], kvm, rel_bias_table, swa_sinks[0],
        bf(_take_runs(swa_w_out[0], rows0, 0)), h0, norm_ffn[0],
        [moe_up[0], moe_down[0], ffn_gate, ffn_up, ffn_down])
    h2, moe_gate_b, xs_zero = _ffn(hn1, h1, ffn_gate_b[0], ffn_up_b[0], ffn_down_b[0], moe_gate[0])

    cols1, scale1 = _dil_weight_layout()
    tok, g1, g2 = _dilproj(h2, norm_mix[1], bf(_take_runs(dil_w_in[0], cols1, 1) * scale1))
    ols = _dil_attention((tok[None], g1, g2), rel_bias_table)
    router_p = jnp.pad(bf(router[0].T), ((0, ROUTER_ROWS - N_EXPERTS), (0, 0)))
    h3, hn3, meta, counts, meta_t = _dilout(ols, tok, kvm, bf(dil_w_out[0]), h2, norm_ffn[1], router_p)
    pos0, pos1, items = _moe_plan(meta_t, counts)
    xs = _moe_scatter(hn3, pos0, pos1, xs_zero)
    y = _moe_grouped(items, xs, moe_gate_b, moe_up_b, moe_down_b)
    out = _moe_combine(pos0, pos1, meta, h3, final_norm, y)
    return out.reshape(1, SEQ, D_MODEL)
```
